```python
import math
import jax, jax.numpy as jnp
from jax import lax
import numpy as np

D_MODEL = 2048
BATCH = 8
SEQ = 2048
DEPTH = 4
DEC_BATCH = 8
DEC_SEQ = 16
PAST_LEN = 4096

CHUNK = 64
N_EVEN = (DEPTH + 1) // 2
N_ODD = DEPTH // 2
C_CONV = D_MODEL // 2
CONV_W = 31
HD_B = 64
H_B = (D_MODEL // 2) // HD_B
KV_B = H_B // 4
G_B = H_B // KV_B
WINDOW = 128
WIN_CHUNKS = WINDOW // CHUNK
E_IN = 2 * C_CONV + H_B * HD_B + 2 * KV_B * HD_B
E_OUT = C_CONV + H_B * HD_B
HD_C = 128
H_C = D_MODEL // HD_C
O_IN = 3 * H_C * HD_C + H_C
Q_BLOCK = 128
D_FF = ((8 * D_MODEL + 3 * 256 - 1) // (3 * 256)) * 256
ROPE_THETA = 10000.0
EPS = 1e-6
NEG = -1e30

kernel_name = "hybrid_streaming_conv_swa_fox_step"


def _rmsnorm(x, g):
    xf = x.astype(jnp.float32)
    y = xf * lax.rsqrt(jnp.mean(xf * xf, axis=-1, keepdims=True) + EPS)
    return (y * g.astype(jnp.float32)).astype(x.dtype)


def _layernorm(x, g, b):
    xf = x.astype(jnp.float32)
    mu = jnp.mean(xf, axis=-1, keepdims=True)
    var = jnp.mean(jnp.square(xf - mu), axis=-1, keepdims=True)
    y = (xf - mu) * lax.rsqrt(var + EPS) * g.astype(jnp.float32) + b.astype(jnp.float32)
    return y.astype(x.dtype)


def _rope(x, pos):
    half = x.shape[-1] // 2
    freqs = ROPE_THETA ** (-jnp.arange(half, dtype=jnp.float32) / half)
    ang = pos.astype(jnp.float32)[:, None] * freqs[None, :]
    cos = jnp.cos(ang)[None, :, None, :]
    sin = jnp.sin(ang)[None, :, None, :]
    xf = x.astype(jnp.float32)
    x1, x2 = xf[..., :half], xf[..., half:]
    return jnp.concatenate([x1 * cos - x2 * sin, x2 * cos + x1 * sin], axis=-1).astype(x.dtype)


def _adaln(c, w, b):
    mod = jax.nn.silu(c) @ w + b
    return jnp.split(mod, 6, axis=-1)


def _modulate(x, g, shift, scale):
    return _rmsnorm(x, g) * (1 + scale[:, None, :]) + shift[:, None, :]


def _swiglu(h, w_gate, w_up, w_down):
    return (jax.nn.silu(h @ w_gate) * (h @ w_up)) @ w_down


def _sink_softmax(s, sink):
    m = jnp.maximum(jnp.max(s, axis=-1, keepdims=True), sink)
    p = jnp.exp(s - m)
    return p / (jnp.sum(p, axis=-1, keepdims=True) + jnp.exp(sink - m))


def _even_project(h, w_in, pos):
    B, T, _ = h.shape
    z = h @ w_in
    a = z[..., :C_CONV]
    gt = z[..., C_CONV:2 * C_CONV]
    o = 2 * C_CONV
    q = z[..., o:o + H_B * HD_B].reshape(B, T, H_B, HD_B)
    o += H_B * HD_B
    k = z[..., o:o + KV_B * HD_B].reshape(B, T, KV_B, HD_B)
    o += KV_B * HD_B
    v = z[..., o:].reshape(B, T, KV_B, HD_B)
    u = a * jax.nn.sigmoid(gt)
    return u, _rope(q, pos), _rope(k, pos), v


def _conv_module(u, hist, w_dw, b_dw, ln_g, ln_b):
    u_ext = jnp.concatenate([hist, u], axis=1)
    y = lax.conv_general_dilated(u_ext, w_dw[:, None, :], window_strides=(1,), padding='VALID',
                                 dimension_numbers=('NWC', 'WIO', 'NWC'),
                                 feature_group_count=C_CONV) + b_dw
    y = jax.nn.silu(_layernorm(y, ln_g, ln_b))
    return y, u_ext[:, -(CONV_W - 1):]


def _window_attn_prompt(q, k, v, sinks):
    B, T = q.shape[:2]
    NC = T // CHUNK
    L = (WIN_CHUNKS + 1) * CHUNK
    qc = q.reshape(B, NC, CHUNK, KV_B, G_B, HD_B)
    pad = jnp.zeros((B, WIN_CHUNKS, CHUNK, KV_B, HD_B), k.dtype)
    kp = jnp.concatenate([pad, k.reshape(B, NC, CHUNK, KV_B, HD_B)], axis=1)
    vp = jnp.concatenate([pad, v.reshape(B, NC, CHUNK, KV_B, HD_B)], axis=1)
    kb = jnp.concatenate([kp[:, i:i + NC] for i in range(WIN_CHUNKS + 1)], axis=2)
    vb = jnp.concatenate([vp[:, i:i + NC] for i in range(WIN_CHUNKS + 1)], axis=2)
    valid = (jnp.arange(NC)[:, None] - WIN_CHUNKS + jnp.arange(L)[None, :] // CHUNK) >= 0
    s = jnp.einsum('bnqkgd,bnlkd->bnkgql', qc, kb, preferred_element_type=jnp.float32) / math.sqrt(HD_B)
    s = jnp.where(valid[None, :, None, None, None, :], s, NEG)
    sink = sinks.astype(jnp.float32).reshape(KV_B, G_B)[None, None, :, :, None, None]
    p = _sink_softmax(s, sink)
    o = jnp.einsum('bnkgql,bnlkd->bnqkgd', p.astype(v.dtype), vb)
    return o.reshape(B, T, H_B * HD_B)


def _window_attn_sample(q, k, v, k_cache, v_cache, sinks):
    B, T = q.shape[:2]
    k_all = jnp.concatenate([k_cache, k], axis=1)
    v_all = jnp.concatenate([v_cache, v], axis=1)
    qg = q.reshape(B, T, KV_B, G_B, HD_B)
    s = jnp.einsum('btkgd,bskd->bkgts', qg, k_all, preferred_element_type=jnp.float32) / math.sqrt(HD_B)
    sink = sinks.astype(jnp.float32).reshape(KV_B, G_B)[None, :, :, None, None]
    p = _sink_softmax(s, sink)
    o = jnp.einsum('bkgts,bskd->btkgd', p.astype(v.dtype), v_all)
    return o.reshape(B, T, H_B * HD_B)


def _even_prompt(h, w_in, w_dw, b_dw, ln_g, ln_b, sinks, w_out):
    B, T, _ = h.shape
    u, q, k, v = _even_project(h, w_in, jnp.arange(T))
    hist = jnp.zeros((B, CONV_W - 1, C_CONV), u.dtype)
    conv_y, conv_state = _conv_module(u, hist, w_dw, b_dw, ln_g, ln_b)
    att = _window_attn_prompt(q, k, v, sinks)
    y = jnp.concatenate([conv_y, att], axis=-1) @ w_out
    win = min(WINDOW, T)
    return y, conv_state, k[:, T - win:], v[:, T - win:]


def _even_sample(h, conv_cache, k_cache, v_cache, past_len, w_in, w_dw, b_dw, ln_g, ln_b, sinks, w_out):
    T = h.shape[1]
    u, q, k, v = _even_project(h, w_in, past_len + jnp.arange(T))
    conv_y, conv_state = _conv_module(u, conv_cache.astype(u.dtype), w_dw, b_dw, ln_g, ln_b)
    att = _window_attn_sample(q, k, v, k_cache, v_cache, sinks)
    y = jnp.concatenate([conv_y, att], axis=-1) @ w_out
    return y, conv_state, k, v


def _odd_project(h, w_in, b_f):
    B, T, _ = h.shape
    z = h @ w_in
    n = H_C * HD_C
    q = z[..., :n].reshape(B, T, H_C, HD_C)
    k = z[..., n:2 * n].reshape(B, T, H_C, HD_C)
    v = z[..., 2 * n:3 * n].reshape(B, T, H_C, HD_C)
    logf = jax.nn.log_sigmoid(z[..., 3 * n:].astype(jnp.float32) + b_f.astype(jnp.float32))
    return q, k, v, logf


def _fox_prompt(q, k, v, logf):
    B, T = q.shape[:2]
    NB = T // Q_BLOCK
    F = jnp.cumsum(logf, axis=1)
    Fk = F.transpose(0, 2, 1)[:, :, None, :]
    kpos = jnp.arange(T)
    qb = q.reshape(B, NB, Q_BLOCK, H_C, HD_C).transpose(1, 0, 2, 3, 4)
    Fq = F.reshape(B, NB, Q_BLOCK, H_C).transpose(1, 0, 3, 2)

    def block(args):
        qi, fi, i = args
        s = jnp.einsum('bqhd,bshd->bhqs', qi, k, preferred_element_type=jnp.float32) / math.sqrt(HD_C)
        s = s + fi[..., None] - Fk
        qpos = i * Q_BLOCK + jnp.arange(Q_BLOCK)
        s = jnp.where(kpos[None, :] <= qpos[:, None], s, NEG)
        p = jax.nn.softmax(s, axis=-1)
        return jnp.einsum('bhqs,bshd->bqhd', p.astype(v.dtype), v)

    o = lax.map(block, (qb, Fq, jnp.arange(NB)))
    return o.transpose(1, 0, 2, 3, 4).reshape(B, T, H_C * HD_C)


def _fox_sample(q, k, v, logf, k_cache, v_cache, logf_cache):
    B, T = q.shape[:2]
    P = k_cache.shape[1]
    k_all = jnp.concatenate([k_cache, k], axis=1)
    v_all = jnp.concatenate([v_cache, v], axis=1)
    F = jnp.cumsum(jnp.concatenate([logf_cache.astype(jnp.float32), logf], axis=1), axis=1)
    s = jnp.einsum('bthd,bshd->bhts', q, k_all, preferred_element_type=jnp.float32) / math.sqrt(HD_C)
    s = s + F[:, P:].transpose(0, 2, 1)[..., None] - F.transpose(0, 2, 1)[:, :, None, :]
    allowed = jnp.arange(P + T)[None, :] <= (P + jnp.arange(T))[:, None]
    s = jnp.where(allowed, s, NEG)
    p = jax.nn.softmax(s, axis=-1)
    o = jnp.einsum('bhts,bshd->bthd', p.astype(v.dtype), v_all)
    return o.reshape(B, T, H_C * HD_C)


def setup_inputs(seed: int = 0) -> dict:
    key = jax.random.key(seed)
    ks = iter(jax.random.split(key, 40))

    def nrm(shape, s=1.0):
        return s * jax.random.normal(next(ks), shape, jnp.float32)

    D = D_MODEL
    win_rows = min(WINDOW, PAST_LEN)
    return {
        'x_prompt': nrm((BATCH, SEQ, D)),
        'x_sample': nrm((DEC_BATCH, DEC_SEQ, D)),
        'c_prompt': nrm((BATCH, D)),
        'c_sample': nrm((DEC_BATCH, D)),
        'cache_conv': nrm((N_EVEN, DEC_BATCH, CONV_W - 1, C_CONV)),
        'cache_win_k': nrm((N_EVEN, DEC_BATCH, win_rows, KV_B, HD_B)),
        'cache_win_v': nrm((N_EVEN, DEC_BATCH, win_rows, KV_B, HD_B)),
        'cache_fox_k': nrm((N_ODD, DEC_BATCH, PAST_LEN, H_C, HD_C)),
        'cache_fox_v': nrm((N_ODD, DEC_BATCH, PAST_LEN, H_C, HD_C)),
        'cache_fox_logf': jax.nn.log_sigmoid(nrm((N_ODD, DEC_BATCH, PAST_LEN, H_C)) + 2.0),
        'ada_w': nrm((DEPTH, D, 6 * D), 0.5 * D ** -0.5),
        'ada_b': nrm((DEPTH, 6 * D), 0.02),
        'norm_mix_g': 1.0 + nrm((DEPTH, D), 0.02),
        'norm_ffn_g': 1.0 + nrm((DEPTH, D), 0.02),
        'e_w_in': nrm((N_EVEN, D, E_IN), D ** -0.5),
        'e_w_dw': nrm((N_EVEN, CONV_W, C_CONV), CONV_W ** -0.5),
        'e_b_dw': nrm((N_EVEN, C_CONV), 0.02),
        'e_ln_g': 1.0 + nrm((N_EVEN, C_CONV), 0.02),
        'e_ln_b': nrm((N_EVEN, C_CONV), 0.02),
        'e_sinks': nrm((N_EVEN, H_B), 0.5),
        'e_w_out': nrm((N_EVEN, E_OUT, D), E_OUT ** -0.5),
        'o_w_in': nrm((N_ODD, D, O_IN), D ** -0.5),
        'o_b_f': 2.0 + nrm((N_ODD, H_C), 0.5),
        'o_w_out': nrm((N_ODD, H_C * HD_C, D), (H_C * HD_C) ** -0.5),
        'ffn_w_gate': nrm((DEPTH, D, D_FF), D ** -0.5),
        'ffn_w_up': nrm((DEPTH, D, D_FF), D ** -0.5),
        'ffn_w_down': nrm((DEPTH, D_FF, D), D_FF ** -0.5),
        'final_g': 1.0 + nrm((D,), 0.02),
    }


def reference(x_prompt, x_sample, c_prompt, c_sample, cache_conv, cache_win_k, cache_win_v,
              cache_fox_k, cache_fox_v, cache_fox_logf, ada_w, ada_b, norm_mix_g, norm_ffn_g,
              e_w_in, e_w_dw, e_b_dw, e_ln_g, e_ln_b, e_sinks, e_w_out, o_w_in, o_b_f, o_w_out,
              ffn_w_gate, ffn_w_up, ffn_w_down, final_g):
    past_len = cache_fox_k.shape[2]
    xp, xs = x_prompt, x_sample
    conv_p, conv_s, wk_p, wk_s, wv_p, wv_s = [], [], [], [], [], []
    fk_p, fk_s, fv_p, fv_s, fl_p, fl_s = [], [], [], [], [], []
    for l in range(DEPTH):
        mp = _adaln(c_prompt, ada_w[l], ada_b[l])
        ms = _adaln(c_sample, ada_w[l], ada_b[l])
        hp = _modulate(xp, norm_mix_g[l], mp[0], mp[1])
        hs = _modulate(xs, norm_mix_g[l], ms[0], ms[1])
        if l % 2 == 0:
            e = l // 2
            yp, cst, kk, vv = _even_prompt(hp, e_w_in[e], e_w_dw[e], e_b_dw[e], e_ln_g[e], e_ln_b[e],
                                           e_sinks[e], e_w_out[e])
            conv_p.append(cst); wk_p.append(kk); wv_p.append(vv)
            ys, cst, kk, vv = _even_sample(hs, cache_conv[e], cache_win_k[e], cache_win_v[e], past_len,
                                           e_w_in[e], e_w_dw[e], e_b_dw[e], e_ln_g[e], e_ln_b[e],
                                           e_sinks[e], e_w_out[e])
            conv_s.append(cst); wk_s.append(kk); wv_s.append(vv)
        else:
            o = l // 2
            q, k, v, lf = _odd_project(hp, o_w_in[o], o_b_f[o])
            yp = _fox_prompt(q, k, v, lf) @ o_w_out[o]
            fk_p.append(k); fv_p.append(v); fl_p.append(lf)
            q, k, v, lf = _odd_project(hs, o_w_in[o], o_b_f[o])
            ys = _fox_sample(q, k, v, lf, cache_fox_k[o], cache_fox_v[o], cache_fox_logf[o]) @ o_w_out[o]
            fk_s.append(k); fv_s.append(v); fl_s.append(lf)
        xp = xp + mp[2][:, None, :] * yp
        xs = xs + ms[2][:, None, :] * ys
        hp = _modulate(xp, norm_ffn_g[l], mp[3], mp[4])
        hs = _modulate(xs, norm_ffn_g[l], ms[3], ms[4])
        xp = xp + mp[5][:, None, :] * _swiglu(hp, ffn_w_gate[l], ffn_w_up[l], ffn_w_down[l])
        xs = xs + ms[5][:, None, :] * _swiglu(hs, ffn_w_gate[l], ffn_w_up[l], ffn_w_down[l])
    y_prompt = _rmsnorm(xp, final_g)
    y_sample = _rmsnorm(xs, final_g)
    conv_state_prompt = jnp.stack(conv_p)
    win_k_prompt = jnp.stack(wk_p)
    win_v_prompt = jnp.stack(wv_p)
    fox_k_prompt = jnp.stack(fk_p)
    fox_v_prompt = jnp.stack(fv_p)
    fox_logf_prompt = jnp.stack(fl_p)
    conv_state_sample = jnp.stack(conv_s)
    win_k_sample = jnp.stack(wk_s)
    win_v_sample = jnp.stack(wv_s)
    fox_k_sample = jnp.stack(fk_s)
    fox_v_sample = jnp.stack(fv_s)
    fox_logf_sample = jnp.stack(fl_s)
    return (y_prompt, y_sample, conv_state_prompt, win_k_prompt, win_v_prompt, fox_k_prompt, fox_v_prompt,
            fox_logf_prompt, conv_state_sample, win_k_sample, win_v_sample, fox_k_sample, fox_v_sample,
            fox_logf_sample)
```

```python
import functools
import math

import jax
import jax.numpy as jnp
from jax import lax
from jax.experimental import pallas as pl
from jax.experimental.pallas import tpu as pltpu

F32 = jnp.float32
BF16 = jnp.bfloat16

D_MODEL = 2048
CHUNK = 64
C_CONV = D_MODEL // 2
CONV_W = 31
HIST_PAD = 32
HD_B = 64
H_B = (D_MODEL // 2) // HD_B
KV_B = H_B // 4
G_B = H_B // KV_B
WINDOW = 128
E_IN = 2 * C_CONV + H_B * HD_B + 2 * KV_B * HD_B
Q_OFF = 2 * C_CONV
K_OFF = Q_OFF + H_B * HD_B
V_OFF = K_OFF + KV_B * HD_B
HD_C = 128
H_C = D_MODEL // HD_C
ROPE_THETA = 10000.0
EPS = 1e-6
NEG = -1e30
LANES = 128
VMEM_LIMIT = 56 * 1024 * 1024


def _params(*sem):
    return pltpu.CompilerParams(dimension_semantics=sem, vmem_limit_bytes=VMEM_LIMIT)


def _tile(n, pref):
    t = min(n, pref)
    while n % t:
        t //= 2
    return t


def _bcast_rows(vec, rows):
    nb, _, w = vec.shape
    if nb == 1:
        return vec[0]
    return jnp.broadcast_to(vec, (nb, rows // nb, w)).reshape(rows, w)


def _mod_spec(layer, chunk, nb, width, index_fn):
    return pl.BlockSpec((None, None, nb, 1, width),
                        lambda *g: (layer, chunk) + tuple(index_fn(*g)))


def _adaln_kernel(c_ref, w_ref, b_ref, o_ref):
    c = c_ref[...]
    a = (c * jax.nn.sigmoid(c)).astype(BF16)
    o_ref[...] = jnp.dot(a, w_ref[...].astype(BF16), preferred_element_type=F32) + b_ref[...]


def _adaln(c_all, ada_w, ada_b):
    n_layers, d, n = ada_w.shape
    nb = c_all.shape[0]
    tn = 1024
    per = d // tn
    out = pl.pallas_call(
        _adaln_kernel,
        grid=(n_layers, n // tn),
        in_specs=[pl.BlockSpec((nb, d), lambda l, j: (0, 0)),
                  pl.BlockSpec((None, d, tn), lambda l, j: (l, 0, j)),
                  pl.BlockSpec((None, 1, tn), lambda l, j: (l, 0, j))],
        out_specs=pl.BlockSpec((None, None, nb, tn), lambda l, j: (l, j // per, 0, j % per)),
        out_shape=jax.ShapeDtypeStruct((n_layers, n // d, nb, d), F32),
        compiler_params=_params("arbitrary", "arbitrary"),
    )(c_all, ada_w, ada_b.reshape(n_layers, 1, n))
    return out.reshape(n_layers, n // d, nb, 1, d)


def _modulate_kernel(x_ref, g_ref, sh_ref, sc_ref, o_ref):
    x = x_ref[...]
    y = x * lax.rsqrt(jnp.mean(x * x, axis=-1, keepdims=True) + EPS)
    y = y * g_ref[...]
    o_ref[...] = (y * (1.0 + sc_ref[0]) + sh_ref[0]).astype(o_ref.dtype)


def _modulate(x, g, mod, layer, chunk, row_off):
    b, t, d = x.shape
    tt = _tile(t, 512)
    return pl.pallas_call(
        _modulate_kernel,
        grid=(b, t // tt),
        in_specs=[pl.BlockSpec((None, tt, d), lambda bi, i: (bi, i, 0)),
                  pl.BlockSpec((1, d), lambda bi, i: (0, 0)),
                  _mod_spec(layer, chunk, 1, d, lambda bi, i: (bi + row_off, 0, 0)),
                  _mod_spec(layer, chunk + 1, 1, d, lambda bi, i: (bi + row_off, 0, 0))],
        out_specs=pl.BlockSpec((None, tt, d), lambda bi, i: (bi, i, 0)),
        out_shape=jax.ShapeDtypeStruct((b, t, d), BF16),
        compiler_params=_params("arbitrary", "arbitrary"),
    )(x, g.reshape(1, d), mod, mod)


def _rmsnorm_kernel(x_ref, g_ref, o_ref):
    x = x_ref[...]
    o_ref[...] = x * lax.rsqrt(jnp.mean(x * x, axis=-1, keepdims=True) + EPS) * g_ref[...]


def _rmsnorm(x, g):
    b, t, d = x.shape
    tt = _tile(t, 512)
    return pl.pallas_call(
        _rmsnorm_kernel,
        grid=(b, t // tt),
        in_specs=[pl.BlockSpec((None, tt, d), lambda bi, i: (bi, i, 0)),
                  pl.BlockSpec((1, d), lambda bi, i: (0, 0))],
        out_specs=pl.BlockSpec((None, tt, d), lambda bi, i: (bi, i, 0)),
        out_shape=jax.ShapeDtypeStruct((b, t, d), F32),
        compiler_params=_params("arbitrary", "arbitrary"),
    )(x, g.reshape(1, d))


def _mm_kernel(a_ref, w_ref, o_ref):
    o_ref[...] = jnp.dot(a_ref[...], w_ref[...], preferred_element_type=F32).astype(o_ref.dtype)


def _matmul(a, w, out_dtype):
    m, k = a.shape
    n = w.shape[1]
    tm, tn = _tile(m, 1024), _tile(n, 1024)
    return pl.pallas_call(
        _mm_kernel,
        grid=(m // tm, n // tn),
        in_specs=[pl.BlockSpec((tm, k), lambda i, j: (i, 0)),
                  pl.BlockSpec((k, tn), lambda i, j: (0, j))],
        out_specs=pl.BlockSpec((tm, tn), lambda i, j: (i, j)),
        out_shape=jax.ShapeDtypeStruct((m, n), out_dtype),
        compiler_params=_params("arbitrary", "arbitrary"),
    )(a, w)


def _rope_tables(pos):
    half = HD_B // 2
    freqs = ROPE_THETA ** (-jnp.arange(half, dtype=F32) / half)
    ang = pos.astype(F32)[:, None] * freqs[None, :]
    cos, sin = jnp.cos(ang), jnp.sin(ang)
    zero = jnp.zeros_like(sin)
    rep = LANES // HD_B
    cos_t = jnp.tile(jnp.concatenate([cos, cos], axis=1), (1, rep))
    sa = jnp.tile(jnp.concatenate([-sin, zero], axis=1), (1, rep))
    sb = jnp.tile(jnp.concatenate([zero, sin], axis=1), (1, rep))
    return cos_t, sa, sb


def _rope(x, cos, sa, sb):
    half = HD_B // 2
    outs = []
    for g in range(x.shape[1] // LANES):
        xg = x[:, g * LANES:(g + 1) * LANES]
        outs.append(xg * cos + pltpu.roll(xg, LANES - half, 1) * sa + pltpu.roll(xg, half, 1) * sb)
    return outs[0] if len(outs) == 1 else jnp.concatenate(outs, axis=1)


def _even_proj_kernel(a_ref, w_ref, cos_ref, sa_ref, sb_ref, o_ref, *, tn):
    j = pl.program_id(1)
    z = jnp.dot(a_ref[...], w_ref[...], preferred_element_type=F32)
    q_lo, k_tile = Q_OFF // tn, K_OFF // tn
    kw = KV_B * HD_B

    @pl.when(j < q_lo)
    def _():
        o_ref[...] = z

    @pl.when(jnp.logical_and(j >= q_lo, j < k_tile))
    def _():
        o_ref[...] = _rope(z, cos_ref[...], sa_ref[...], sb_ref[...])

    @pl.when(j == k_tile)
    def _():
        o_ref[:, :kw] = _rope(z[:, :kw], cos_ref[...], sa_ref[...], sb_ref[...])
        o_ref[:, kw:] = z[:, kw:]


def _even_proj(a, w, tables, period):
    m, k = a.shape
    n = w.shape[1]
    tn = 2 * KV_B * HD_B
    tm = _tile(period, 1024) if m > period else m
    per_blocks = max(period // tm, 1)
    tab_spec = pl.BlockSpec((tm, LANES), lambda i, j: (i % per_blocks, 0))
    return pl.pallas_call(
        functools.partial(_even_proj_kernel, tn=tn),
        grid=(m // tm, n // tn),
        in_specs=[pl.BlockSpec((tm, k), lambda i, j: (i, 0)),
                  pl.BlockSpec((k, tn), lambda i, j: (0, j)),
                  tab_spec, tab_spec, tab_spec],
        out_specs=pl.BlockSpec((tm, tn), lambda i, j: (i, j)),
        out_shape=jax.ShapeDtypeStruct((m, n), F32),
        compiler_params=_params("arbitrary", "arbitrary"),
    )(a, w, *tables)


def _residual_kernel(a_ref, w_ref, x_ref, gate_ref, o_ref):
    y = jnp.dot(a_ref[...], w_ref[...], preferred_element_type=F32)
    o_ref[...] = x_ref[...] + _bcast_rows(gate_ref[...], y.shape[0]) * y


def _residual_matmul(a, w, x, mod, layer, chunk, rows_per_batch, row_off):
    m, k = a.shape
    n = w.shape[1]
    tn = _tile(n, 1024)
    if m > rows_per_batch:
        tm = _tile(rows_per_batch, 1024) if m > 1024 else m
    else:
        tm = m
    nb = max(tm // rows_per_batch, 1)
    if nb == 1:
        gate_spec = _mod_spec(layer, chunk, 1, tn,
                              lambda i, j: ((i * tm) // rows_per_batch + row_off, 0, j))
    else:
        gate_spec = _mod_spec(layer, chunk, nb, tn, lambda i, j: (row_off // nb + i, 0, j))
    return pl.pallas_call(
        _residual_kernel,
        grid=(m // tm, n // tn),
        in_specs=[pl.BlockSpec((tm, k), lambda i, j: (i, 0)),
                  pl.BlockSpec((k, tn), lambda i, j: (0, j)),
                  pl.BlockSpec((tm, tn), lambda i, j: (i, j)),
                  gate_spec],
        out_specs=pl.BlockSpec((tm, tn), lambda i, j: (i, j)),
        out_shape=jax.ShapeDtypeStruct((m, n), F32),
        compiler_params=_params("arbitrary", "arbitrary"),
    )(a, w, x, mod)


def _ffn_kernel(h_ref, x_ref, gate_ref, wg_ref, wu_ref, wd_ref, o_ref, acc_ref):
    f = pl.program_id(1)

    @pl.when(f == 0)
    def _():
        acc_ref[...] = jnp.zeros_like(acc_ref)

    h = h_ref[...]
    g = jnp.dot(h, wg_ref[...], preferred_element_type=F32)
    u = jnp.dot(h, wu_ref[...], preferred_element_type=F32)
    a = (g * jax.nn.sigmoid(g) * u).astype(BF16)
    acc_ref[...] += jnp.dot(a, wd_ref[...], preferred_element_type=F32)

    @pl.when(f == pl.num_programs(1) - 1)
    def _():
        o_ref[...] = x_ref[...] + _bcast_rows(gate_ref[...], o_ref.shape[0]) * acc_ref[...]


def _ffn(h, x, wg, wu, wd, mod, layer, rows_per_batch, row_off):
    m, d = h.shape
    ff = wg.shape[1]
    tf = _tile(ff, 512)
    tm = _tile(rows_per_batch, 512) if m > rows_per_batch and m > 512 else m
    nb = max(tm // rows_per_batch, 1)
    if nb == 1:
        gate_spec = _mod_spec(layer, 5, 1, d, lambda i, f: ((i * tm) // rows_per_batch + row_off, 0, 0))
    else:
        gate_spec = _mod_spec(layer, 5, nb, d, lambda i, f: (row_off // nb + i, 0, 0))
    return pl.pallas_call(
        _ffn_kernel,
        grid=(m // tm, ff // tf),
        in_specs=[pl.BlockSpec((tm, d), lambda i, f: (i, 0)),
                  pl.BlockSpec((tm, d), lambda i, f: (i, 0)),
                  gate_spec,
                  pl.BlockSpec((d, tf), lambda i, f: (0, f)),
                  pl.BlockSpec((d, tf), lambda i, f: (0, f)),
                  pl.BlockSpec((tf, d), lambda i, f: (f, 0))],
        out_specs=pl.BlockSpec((tm, d), lambda i, f: (i, 0)),
        out_shape=jax.ShapeDtypeStruct((m, d), F32),
        scratch_shapes=[pltpu.VMEM((tm, d), F32)],
        compiler_params=_params("arbitrary", "arbitrary"),
    )(h, x, mod, wg, wu, wd)


def _conv_ln_silu(uext_ref, wdw_ref, bdw_ref, lng_ref, lnb_ref, y_ref, rows):
    base = HIST_PAD - (CONV_W - 1)
    for c in range(C_CONV // LANES):
        cs = slice(c * LANES, (c + 1) * LANES)
        acc = jnp.zeros((rows, LANES), F32)
        for j in range(CONV_W):
            acc = acc + wdw_ref[j:j + 1, cs] * uext_ref[base + j:base + j + rows, cs]
        y_ref[:, cs] = acc + bdw_ref[:, cs]
    y = y_ref[...]
    mu = jnp.mean(y, axis=-1, keepdims=True)
    yc = y - mu
    var = jnp.mean(yc * yc, axis=-1, keepdims=True)
    yn = yc * lax.rsqrt(var + EPS) * lng_ref[...] + lnb_ref[...]
    return yn * jax.nn.sigmoid(yn)


def _sink_attention(q, k, v, sink_col, mask):
    s = lax.dot_general(q, k, (((1,), (1,)), ((), ())), preferred_element_type=F32) / math.sqrt(HD_B)
    if mask is not None:
        s = jnp.where(mask, s, NEG)
    m = jnp.maximum(jnp.max(s, axis=-1, keepdims=True), sink_col)
    p = jnp.exp(s - m)
    den = jnp.sum(p, axis=-1, keepdims=True) + jnp.exp(sink_col - m)
    return jnp.dot(p.astype(BF16), v, preferred_element_type=F32) / den


def _sink_column(sink_ref, kv_head, rows):
    return jnp.concatenate(
        [jnp.full((rows, 1), sink_ref[kv_head * G_B + g], F32) for g in range(G_B)], axis=0)


def _even_mix_kernel(sink_ref, zc_ref, zh_ref, zkv_ref, wdw_ref, bdw_ref, lng_ref, lnb_ref,
                     cat_ref, cst_ref, uext_ref, y_ref, *, tt):
    i = pl.program_id(1)
    first = i == 0

    uext_ref[HIST_PAD:, :] = zc_ref[:, 0:C_CONV] * jax.nn.sigmoid(zc_ref[:, C_CONV:2 * C_CONV])
    hist = zh_ref[:, 0:C_CONV] * jax.nn.sigmoid(zh_ref[:, C_CONV:2 * C_CONV])
    uext_ref[0:HIST_PAD, :] = jnp.where(first, 0.0, hist)
    conv_y = _conv_ln_silu(uext_ref, wdw_ref, bdw_ref, lng_ref, lnb_ref, y_ref, tt)
    cat_ref[:, 0:C_CONV] = conv_y.astype(BF16)

    @pl.when(i == pl.num_programs(1) - 1)
    def _():
        cst_ref[...] = uext_ref[tt:tt + HIST_PAD, :]

    kw = KV_B * HD_B
    sub = 2 * CHUNK
    kext = jnp.concatenate([zkv_ref[:, 0:kw], zc_ref[:, K_OFF:K_OFF + kw]], axis=0).astype(BF16)
    vext = jnp.concatenate([zkv_ref[:, kw:2 * kw], zc_ref[:, V_OFF:V_OFF + kw]], axis=0).astype(BF16)
    rows = G_B * sub
    qc = (lax.broadcasted_iota(jnp.int32, (rows, 2 * sub), 0) % sub) // CHUNK
    kc = lax.broadcasted_iota(jnp.int32, (rows, 2 * sub), 1) // CHUNK
    band = jnp.logical_and(kc >= qc, kc <= qc + WINDOW // CHUNK)
    band_first = jnp.logical_and(band, jnp.logical_or(jnp.logical_not(first), kc >= WINDOW // CHUNK))
    for s in range(tt // sub):
        r0 = s * sub
        mask = band_first if s == 0 else band
        for kh in range(KV_B):
            hs = slice(kh * HD_B, (kh + 1) * HD_B)
            q = jnp.concatenate(
                [zc_ref[r0:r0 + sub, Q_OFF + (kh * G_B + g) * HD_B:Q_OFF + (kh * G_B + g + 1) * HD_B]
                 for g in range(G_B)], axis=0).astype(BF16)
            o = _sink_attention(q, kext[r0:r0 + 2 * sub, hs], vext[r0:r0 + 2 * sub, hs],
                                _sink_column(sink_ref, kh, sub), mask)
            for gp in range(G_B // 2):
                pair = jnp.concatenate([o[(2 * gp) * sub:(2 * gp + 1) * sub],
                                        o[(2 * gp + 1) * sub:(2 * gp + 2) * sub]], axis=1)
                c0 = C_CONV + (kh * G_B + 2 * gp) * HD_B
                cat_ref[r0:r0 + sub, c0:c0 + 2 * HD_B] = pair.astype(BF16)


def _even_mix(z, sinks, w_dw, b_dw, ln_g, ln_b):
    b, t, _ = z.shape
    tt = _tile(t, 256)
    hb = tt // HIST_PAD
    kb = tt // WINDOW
    kvw = 2 * KV_B * HD_B
    vec = pl.BlockSpec((1, C_CONV), lambda bi, i: (0, 0))
    return pl.pallas_call(
        functools.partial(_even_mix_kernel, tt=tt),
        grid=(b, t // tt),
        in_specs=[pl.BlockSpec(memory_space=pltpu.SMEM),
                  pl.BlockSpec((None, tt, E_IN), lambda bi, i: (bi, i, 0)),
                  pl.BlockSpec((None, HIST_PAD, 2 * C_CONV),
                               lambda bi, i: (bi, jnp.maximum(i * hb - 1, 0), 0)),
                  pl.BlockSpec((None, WINDOW, kvw),
                               lambda bi, i: (bi, jnp.maximum(i * kb - 1, 0), K_OFF // kvw)),
                  pl.BlockSpec((CONV_W, C_CONV), lambda bi, i: (0, 0)),
                  vec, vec, vec],
        out_specs=[pl.BlockSpec((None, tt, D_MODEL), lambda bi, i: (bi, i, 0)),
                   pl.BlockSpec((None, HIST_PAD, C_CONV), lambda bi, i: (bi, 0, 0))],
        out_shape=[jax.ShapeDtypeStruct((b, t, D_MODEL), BF16),
                   jax.ShapeDtypeStruct((b, HIST_PAD, C_CONV), F32)],
        scratch_shapes=[pltpu.VMEM((HIST_PAD + tt, C_CONV), F32),
                        pltpu.VMEM((tt, C_CONV), F32)],
        compiler_params=_params("arbitrary", "arbitrary"),
    )(sinks, z, z, z, w_dw, b_dw.reshape(1, -1), ln_g.reshape(1, -1), ln_b.reshape(1, -1))


def _even_mix_sample_kernel(sink_ref, z_ref, hist_ref, kc_ref, vc_ref, wdw_ref, bdw_ref, lng_ref,
                            lnb_ref, cat_ref, cst_ref, uext_ref, y_ref, *, t):
    uext_ref[0:HIST_PAD, :] = hist_ref[...]
    uext_ref[HIST_PAD:, :] = z_ref[:, 0:C_CONV] * jax.nn.sigmoid(z_ref[:, C_CONV:2 * C_CONV])
    conv_y = _conv_ln_silu(uext_ref, wdw_ref, bdw_ref, lng_ref, lnb_ref, y_ref, t)
    cat_ref[:, 0:C_CONV] = conv_y.astype(BF16)
    cst_ref[...] = uext_ref[t:t + HIST_PAD, :]

    kw = KV_B * HD_B
    kall = jnp.concatenate([kc_ref[...], z_ref[:, K_OFF:K_OFF + kw]], axis=0).astype(BF16)
    vall = jnp.concatenate([vc_ref[...], z_ref[:, V_OFF:V_OFF + kw]], axis=0).astype(BF16)
    for kh in range(KV_B):
        hs = slice(kh * HD_B, (kh + 1) * HD_B)
        q = jnp.concatenate(
            [z_ref[:, Q_OFF + (kh * G_B + g) * HD_B:Q_OFF + (kh * G_B + g + 1) * HD_B]
             for g in range(G_B)], axis=0).astype(BF16)
        o = _sink_attention(q, kall[:, hs], vall[:, hs], _sink_column(sink_ref, kh, t), None)
        for gp in range(G_B // 2):
            pair = jnp.concatenate([o[(2 * gp) * t:(2 * gp + 1) * t],
                                    o[(2 * gp + 1) * t:(2 * gp + 2) * t]], axis=1)
            c0 = C_CONV + (kh * G_B + 2 * gp) * HD_B
            cat_ref[:, c0:c0 + 2 * HD_B] = pair.astype(BF16)


def _even_mix_sample(z, hist, k_cache, v_cache, sinks, w_dw, b_dw, ln_g, ln_b):
    b, t, _ = z.shape
    win = k_cache.shape[1]
    kw = KV_B * HD_B
    vec = pl.BlockSpec((1, C_CONV), lambda bi: (0, 0))
    return pl.pallas_call(
        functools.partial(_even_mix_sample_kernel, t=t),
        grid=(b,),
        in_specs=[pl.BlockSpec(memory_space=pltpu.SMEM),
                  pl.BlockSpec((None, t, E_IN), lambda bi: (bi, 0, 0)),
                  pl.BlockSpec((None, HIST_PAD, C_CONV), lambda bi: (bi, 0, 0)),
                  pl.BlockSpec((None, win, kw), lambda bi: (bi, 0, 0)),
                  pl.BlockSpec((None, win, kw), lambda bi: (bi, 0, 0)),
                  pl.BlockSpec((CONV_W, C_CONV), lambda bi: (0, 0)),
                  vec, vec, vec],
        out_specs=[pl.BlockSpec((None, t, D_MODEL), lambda bi: (bi, 0, 0)),
                   pl.BlockSpec((None, HIST_PAD, C_CONV), lambda bi: (bi, 0, 0))],
        out_shape=[jax.ShapeDtypeStruct((b, t, D_MODEL), BF16),
                   jax.ShapeDtypeStruct((b, HIST_PAD, C_CONV), F32)],
        scratch_shapes=[pltpu.VMEM((HIST_PAD + t, C_CONV), F32),
                        pltpu.VMEM((t, C_CONV), F32)],
        compiler_params=_params("arbitrary"),
    )(sinks, z, hist, k_cache.reshape(b, win, kw), v_cache.reshape(b, win, kw),
      w_dw, b_dw.reshape(1, -1), ln_g.reshape(1, -1), ln_b.reshape(1, -1))


def _scan_kernel(x_ref, b_ref, lf_ref, f_ref, ft_ref, carry_ref, *, tt, log_sigmoid):
    i = pl.program_id(1)

    @pl.when(i == 0)
    def _():
        carry_ref[...] = jnp.zeros_like(carry_ref)

    x = x_ref[...]
    if log_sigmoid:
        x = x + b_ref[...]
        x = jnp.minimum(x, 0.0) - jnp.log1p(jnp.exp(-jnp.abs(x)))
    lf_ref[...] = x[:, :H_C]
    row = lax.broadcasted_iota(jnp.int32, x.shape, 0)
    step = 1
    while step < tt:
        x = x + jnp.where(row >= step, pltpu.roll(x, step, 0), 0.0)
        step *= 2
    f = x + carry_ref[...]
    carry_ref[...] = f[tt - 1:tt, :]
    f_ref[...] = f
    if tt < LANES:
        f = jnp.concatenate([f, jnp.zeros((LANES - tt, LANES), F32)], axis=0)
    ft_ref[...] = f.T[:H_C, :tt]


def _forget_scan(x, bias, log_sigmoid):
    b, t, _ = x.shape
    tt = _tile(t, 512)
    return pl.pallas_call(
        functools.partial(_scan_kernel, tt=tt, log_sigmoid=log_sigmoid),
        grid=(b, t // tt),
        in_specs=[pl.BlockSpec((None, tt, LANES), lambda bi, i: (bi, i, 0)),
                  pl.BlockSpec((1, LANES), lambda bi, i: (0, 0))],
        out_specs=[pl.BlockSpec((None, tt, H_C), lambda bi, i: (bi, i, 0)),
                   pl.BlockSpec((None, tt, LANES), lambda bi, i: (bi, i, 0)),
                   pl.BlockSpec((None, H_C, tt), lambda bi, i: (bi, 0, i))],
        out_shape=[jax.ShapeDtypeStruct((b, t, H_C), F32),
                   jax.ShapeDtypeStruct((b, t, LANES), F32),
                   jax.ShapeDtypeStruct((b, H_C, t), F32)],
        scratch_shapes=[pltpu.VMEM((1, LANES), F32)],
        compiler_params=_params("arbitrary", "arbitrary"),
    )(x, bias)


def _online_update(s, v, m_ref, l_ref, acc_ref):
    m_prev = m_ref[...]
    m_new = jnp.maximum(m_prev, jnp.max(s, axis=-1, keepdims=True))
    alpha = jnp.exp(m_prev - m_new)
    p = jnp.exp(s - m_new)
    l_ref[...] = alpha * l_ref[...] + jnp.sum(p, axis=-1, keepdims=True)
    acc_ref[...] = alpha * acc_ref[...] + jnp.dot(p.astype(BF16), v, preferred_element_type=F32)
    m_ref[...] = m_new


def _fox_kernel(q_ref, k_ref, v_ref, fq_ref, fk_ref, o_ref, m_ref, l_ref, acc_ref, *, tq):
    h = pl.program_id(1)
    i = pl.program_id(2)
    q = q_ref[...]
    lane = lax.broadcasted_iota(jnp.int32, (tq, LANES), 1)
    fq = jnp.sum(jnp.where(lane == h, fq_ref[...], 0.0), axis=-1, keepdims=True)
    m_ref[...] = jnp.full_like(m_ref, NEG)
    l_ref[...] = jnp.zeros_like(l_ref)
    acc_ref[...] = jnp.zeros_like(acc_ref)

    def logits(j):
        start = pl.multiple_of(j * tq, tq)
        kj = k_ref[pl.ds(start, tq), :].astype(BF16)
        s = lax.dot_general(q, kj, (((1,), (1,)), ((), ())), preferred_element_type=F32)
        s = s / math.sqrt(HD_C) + fq - fk_ref[j]
        return s, v_ref[pl.ds(start, tq), :].astype(BF16)

    def body(j, carry):
        s, vj = logits(j)
        _online_update(s, vj, m_ref, l_ref, acc_ref)
        return carry

    lax.fori_loop(0, i, body, 0)
    s, vj = logits(i)
    causal = (lax.broadcasted_iota(jnp.int32, (tq, tq), 1)
              <= lax.broadcasted_iota(jnp.int32, (tq, tq), 0))
    _online_update(jnp.where(causal, s, NEG), vj, m_ref, l_ref, acc_ref)
    o_ref[...] = (acc_ref[...] / l_ref[...]).astype(o_ref.dtype)


def _fox_prompt(q, k, v, f_rows, f_heads):
    b, t, _ = q.shape
    tq = _tile(t, 512)
    nq = t // tq
    fk = f_heads.reshape(b, H_C, nq, 1, tq)
    return pl.pallas_call(
        functools.partial(_fox_kernel, tq=tq),
        grid=(b, H_C, nq),
        in_specs=[pl.BlockSpec((None, tq, HD_C), lambda bi, h, i: (bi, i, h)),
                  pl.BlockSpec((None, t, HD_C), lambda bi, h, i: (bi, 0, h)),
                  pl.BlockSpec((None, t, HD_C), lambda bi, h, i: (bi, 0, h)),
                  pl.BlockSpec((None, tq, LANES), lambda bi, h, i: (bi, i, 0)),
                  pl.BlockSpec((None, None, nq, 1, tq), lambda bi, h, i: (bi, h, 0, 0, 0))],
        out_specs=pl.BlockSpec((None, tq, HD_C), lambda bi, h, i: (bi, i, h)),
        out_shape=jax.ShapeDtypeStruct((b, t, H_C * HD_C), BF16),
        scratch_shapes=[pltpu.VMEM((tq, 1), F32), pltpu.VMEM((tq, 1), F32),
                        pltpu.VMEM((tq, HD_C), F32)],
        compiler_params=_params("arbitrary", "arbitrary", "arbitrary"),
    )(q, k, v, f_rows, fk)


def _fox_sample_kernel(q_ref, kc_ref, vc_ref, kn_ref, vn_ref, fct_ref, fn_ref, fnt_ref,
                       o_ref, m_ref, l_ref, acc_ref, *, t, tk):
    j = pl.program_id(1)
    last = pl.num_programs(1) - 1

    @pl.when(j == 0)
    def _():
        m_ref[...] = jnp.full_like(m_ref, NEG)
        l_ref[...] = jnp.zeros_like(l_ref)
        acc_ref[...] = jnp.zeros_like(acc_ref)

    f_end = fnt_ref[:, t:t + 1]
    for h in range(H_C):
        hs = slice(h * HD_C, (h + 1) * HD_C)
        s = lax.dot_general(q_ref[:, hs], kc_ref[:, hs].astype(BF16), (((1,), (1,)), ((), ())),
                            preferred_element_type=F32)
        s = s / math.sqrt(HD_C) + (fn_ref[:, h:h + 1] + f_end[h:h + 1, :]) - fct_ref[h:h + 1, :]
        _online_update(s, vc_ref[:, hs].astype(BF16), m_ref.at[h], l_ref.at[h], acc_ref.at[h])

    @pl.when(j == last)
    def _():
        causal = (lax.broadcasted_iota(jnp.int32, (t, t), 1)
                  <= lax.broadcasted_iota(jnp.int32, (t, t), 0))
        for h in range(H_C):
            hs = slice(h * HD_C, (h + 1) * HD_C)
            q = q_ref[:, hs]
            s = lax.dot_general(q, kn_ref[:, hs].astype(BF16), (((1,), (1,)), ((), ())),
                                preferred_element_type=F32)
            s = s / math.sqrt(HD_C) + fn_ref[:, h:h + 1] - fnt_ref[h:h + 1, 0:t]
            _online_update(jnp.where(causal, s, NEG), vn_ref[:, hs].astype(BF16),
                           m_ref.at[h], l_ref.at[h], acc_ref.at[h])
            o_ref[:, hs] = (acc_ref[h] / l_ref[h]).astype(o_ref.dtype)


def _fox_sample(q, k_new, v_new, k_cache, v_cache, fc_heads, fn_rows, fn_heads_ext):
    b, t, d = q.shape
    p = k_cache.shape[1]
    tk = _tile(p, 512)
    return pl.pallas_call(
        functools.partial(_fox_sample_kernel, t=t, tk=tk),
        grid=(b, p // tk),
        in_specs=[pl.BlockSpec((None, t, d), lambda bi, j: (bi, 0, 0)),
                  pl.BlockSpec((None, tk, d), lambda bi, j: (bi, j, 0)),
                  pl.BlockSpec((None, tk, d), lambda bi, j: (bi, j, 0)),
                  pl.BlockSpec((None, t, d), lambda bi, j: (bi, 0, 0)),
                  pl.BlockSpec((None, t, d), lambda bi, j: (bi, 0, 0)),
                  pl.BlockSpec((None, H_C, tk), lambda bi, j: (bi, 0, j)),
                  pl.BlockSpec((None, t, LANES), lambda bi, j: (bi, 0, 0)),
                  pl.BlockSpec((None, H_C, t + 1), lambda bi, j: (bi, 0, 0))],
        out_specs=pl.BlockSpec((None, t, d), lambda bi, j: (bi, 0, 0)),
        out_shape=jax.ShapeDtypeStruct((b, t, d), BF16),
        scratch_shapes=[pltpu.VMEM((H_C, t, 1), F32), pltpu.VMEM((H_C, t, 1), F32),
                        pltpu.VMEM((H_C, t, HD_C), F32)],
        compiler_params=_params("arbitrary", "arbitrary"),
    )(q, k_cache.reshape(b, p, d), v_cache.reshape(b, p, d), k_new, v_new,
      fc_heads, fn_rows, fn_heads_ext)


def kernel(x_prompt, x_sample, c_prompt, c_sample, cache_conv, cache_win_k, cache_win_v, cache_fox_k, cache_fox_v, cache_fox_logf, ada_w, ada_b, norm_mix_g, norm_ffn_g, e_w_in, e_w_dw, e_b_dw, e_ln_g, e_ln_b, e_sinks, e_w_out, o_w_in, o_b_f, o_w_out, ffn_w_gate, ffn_w_up, ffn_w_down, final_g):
    bp, tp, d = x_prompt.shape
    bs, ts, _ = x_sample.shape
    depth = ada_w.shape[0]
    past_len = cache_fox_k.shape[2]
    nqkv = H_C * HD_C

    mod = _adaln(jnp.concatenate([c_prompt, c_sample], axis=0), ada_w, ada_b)
    tab_p = _rope_tables(jnp.arange(tp))
    tab_s = tuple(jnp.tile(x, (bs, 1)) for x in _rope_tables(past_len + jnp.arange(ts)))

    xp, xs = x_prompt, x_sample
    outs = {name: [] for name in ("conv_p", "wk_p", "wv_p", "fk_p", "fv_p", "fl_p",
                                  "conv_s", "wk_s", "wv_s", "fk_s", "fv_s", "fl_s")}
    for l in range(depth):
        hp = _modulate(xp, norm_mix_g[l], mod, l, 0, 0).reshape(bp * tp, d)
        hs = _modulate(xs, norm_mix_g[l], mod, l, 0, bp).reshape(bs * ts, d)
        if l % 2 == 0:
            e = l // 2
            w_in = e_w_in[e].astype(BF16)
            w_out = e_w_out[e].astype(BF16)
            conv = (e_w_dw[e], e_b_dw[e], e_ln_g[e], e_ln_b[e])
            zp = _even_proj(hp, w_in, tab_p, tp).reshape(bp, tp, E_IN)
            cat, cst = _even_mix(zp, e_sinks[e], *conv)
            win = min(WINDOW, tp)
            outs["conv_p"].append(cst[:, HIST_PAD - (CONV_W - 1):])
            outs["wk_p"].append(zp[:, tp - win:, K_OFF:V_OFF].reshape(bp, win, KV_B, HD_B))
            outs["wv_p"].append(zp[:, tp - win:, V_OFF:].reshape(bp, win, KV_B, HD_B))
            yp = cat.reshape(bp * tp, d)

            zs = _even_proj(hs, w_in, tab_s, bs * ts).reshape(bs, ts, E_IN)
            hist = jnp.pad(cache_conv[e], ((0, 0), (HIST_PAD - (CONV_W - 1), 0), (0, 0)))
            cat, cst = _even_mix_sample(zs, hist, cache_win_k[e], cache_win_v[e], e_sinks[e], *conv)
            outs["conv_s"].append(cst[:, HIST_PAD - (CONV_W - 1):])
            outs["wk_s"].append(zs[:, :, K_OFF:V_OFF].reshape(bs, ts, KV_B, HD_B))
            outs["wv_s"].append(zs[:, :, V_OFF:].reshape(bs, ts, KV_B, HD_B))
            ys = cat.reshape(bs * ts, d)
        else:
            o = l // 2
            w_in = o_w_in[o]
            wq = w_in[:, :nqkv].astype(BF16)
            wk = w_in[:, nqkv:2 * nqkv].astype(BF16)
            wv = w_in[:, 2 * nqkv:3 * nqkv].astype(BF16)
            wf = jnp.pad(w_in[:, 3 * nqkv:], ((0, 0), (0, LANES - H_C))).astype(BF16)
            bf = jnp.pad(o_b_f[o], (0, LANES - H_C)).reshape(1, LANES)
            w_out = o_w_out[o].astype(BF16)

            q = _matmul(hp, wq, BF16).reshape(bp, tp, nqkv)
            k = _matmul(hp, wk, F32).reshape(bp, tp, nqkv)
            v = _matmul(hp, wv, F32).reshape(bp, tp, nqkv)
            lf, f_rows, f_heads = _forget_scan(_matmul(hp, wf, F32).reshape(bp, tp, LANES), bf, True)
            outs["fk_p"].append(k.reshape(bp, tp, H_C, HD_C))
            outs["fv_p"].append(v.reshape(bp, tp, H_C, HD_C))
            outs["fl_p"].append(lf)
            yp = _fox_prompt(q, k, v, f_rows, f_heads).reshape(bp * tp, d)

            q = _matmul(hs, wq, BF16).reshape(bs, ts, nqkv)
            k = _matmul(hs, wk, F32).reshape(bs, ts, nqkv)
            v = _matmul(hs, wv, F32).reshape(bs, ts, nqkv)
            lf, fn_rows, fn_heads = _forget_scan(_matmul(hs, wf, F32).reshape(bs, ts, LANES), bf, True)
            cache_lf = jnp.pad(cache_fox_logf[o], ((0, 0), (0, 0), (0, LANES - H_C)))
            _, _, fc_heads = _forget_scan(cache_lf, bf, False)
            fn_heads_ext = jnp.concatenate([fn_heads, fc_heads[:, :, past_len - 1:]], axis=2)
            outs["fk_s"].append(k.reshape(bs, ts, H_C, HD_C))
            outs["fv_s"].append(v.reshape(bs, ts, H_C, HD_C))
            outs["fl_s"].append(lf)
            ys = _fox_sample(q, k, v, cache_fox_k[o], cache_fox_v[o], fc_heads, fn_rows,
                             fn_heads_ext).reshape(bs * ts, d)

        xp = _residual_matmul(yp, w_out, xp.reshape(bp * tp, d), mod, l, 2, tp, 0).reshape(bp, tp, d)
        xs = _residual_matmul(ys, w_out, xs.reshape(bs * ts, d), mod, l, 2, ts, bp).reshape(bs, ts, d)

        wg = ffn_w_gate[l].astype(BF16)
        wu = ffn_w_up[l].astype(BF16)
        wd = ffn_w_down[l].astype(BF16)
        hp = _modulate(xp, norm_ffn_g[l], mod, l, 3, 0).reshape(bp * tp, d)
        hs = _modulate(xs, norm_ffn_g[l], mod, l, 3, bp).reshape(bs * ts, d)
        xp = _ffn(hp, xp.reshape(bp * tp, d), wg, wu, wd, mod, l, tp, 0).reshape(bp, tp, d)
        xs = _ffn(hs, xs.reshape(bs * ts, d), wg, wu, wd, mod, l, ts, bp).reshape(bs, ts, d)

    stack = lambda name: jnp.stack(outs[name])
    return (_rmsnorm(xp, final_g), _rmsnorm(xs, final_g),
            stack("conv_p"), stack("wk_p"), stack("wv_p"), stack("fk_p"), stack("fv_p"), stack("fl_p"),
            stack("conv_s"), stack("wk_s"), stack("wv_s"), stack("fk_s"), stack("fv_s"), stack("fl_s"))
```

```python
import functools
import math

import jax
import jax.numpy as jnp
from jax import lax
from jax.experimental import pallas as pl
from jax.experimental.pallas import tpu as pltpu

F32 = jnp.float32
BF16 = jnp.bfloat16

D_MODEL = 2048
CHUNK = 64
C_CONV = D_MODEL // 2
CONV_W = 31
HIST_PAD = 32
HD_B = 64
H_B = (D_MODEL // 2) // HD_B
KV_B = H_B // 4
G_B = H_B // KV_B
WINDOW = 128
E_IN = 2 * C_CONV + H_B * HD_B + 2 * KV_B * HD_B
Q_OFF = 2 * C_CONV
K_OFF = Q_OFF + H_B * HD_B
V_OFF = K_OFF + KV_B * HD_B
HD_C = 128
H_C = D_MODEL // HD_C
ROPE_THETA = 10000.0
EPS = 1e-6
NEG = -1e30
LANES = 128
SUBLANES = 8
VMEM_LIMIT = 56 * 1024 * 1024


def _params(*sem):
    return pltpu.CompilerParams(dimension_semantics=sem, vmem_limit_bytes=VMEM_LIMIT)


def _tile(n, pref):
    t = min(n, pref)
    while n % t:
        t //= 2
    return t


def _bcast_rows(vec, rows):
    nb, _, w = vec.shape
    if nb == 1:
        return vec[0]
    return jnp.broadcast_to(vec, (nb, rows // nb, w)).reshape(rows, w)


def _mod_spec(layer, chunk, nb, width, index_fn):
    return pl.BlockSpec((None, None, nb, 1, width),
                        lambda *g: (layer, chunk) + tuple(index_fn(*g)))


def _row_tiling(m, rows_per_batch, pref):
    if m > rows_per_batch and m > pref:
        tm = _tile(rows_per_batch, pref)
        return tm, 1, lambda i, row_off: (i * tm) // rows_per_batch + row_off
    nb = max(m // rows_per_batch, 1)
    return m, nb, lambda i, row_off: row_off // nb + i


def _rms_scale(x):
    return lax.rsqrt(jnp.mean(x * x, axis=-1, keepdims=True) + EPS)


def _adaln_kernel(c_ref, w_ref, b_ref, o_ref):
    c = c_ref[...]
    a = (c * jax.nn.sigmoid(c)).astype(BF16)
    o_ref[...] = jnp.dot(a, w_ref[...].astype(BF16), preferred_element_type=F32) + b_ref[...]


def _adaln(c_all, ada_w, ada_b):
    n_layers, d, n = ada_w.shape
    nb = c_all.shape[0]
    tn = 1024
    per = d // tn
    out = pl.pallas_call(
        _adaln_kernel,
        name="adaln",
        grid=(n_layers, n // tn),
        in_specs=[pl.BlockSpec((nb, d), lambda l, j: (0, 0)),
                  pl.BlockSpec((None, d, tn), lambda l, j: (l, 0, j)),
                  pl.BlockSpec((None, 1, tn), lambda l, j: (l, 0, j))],
        out_specs=pl.BlockSpec((None, None, nb, tn), lambda l, j: (l, j // per, 0, j % per)),
        out_shape=jax.ShapeDtypeStruct((n_layers, n // d, nb, d), F32),
        compiler_params=_params("arbitrary", "arbitrary"),
    )(c_all, ada_w, ada_b.reshape(n_layers, 1, n))
    return out.reshape(n_layers, n // d, nb, 1, d)


def _modulate_kernel(x_ref, g_ref, sh_ref, sc_ref, o_ref):
    x = x_ref[...]
    y = x * _rms_scale(x) * (g_ref[...] * (1.0 + sc_ref[0]))
    o_ref[...] = (y + sh_ref[0]).astype(o_ref.dtype)


def _modulate(x, g, mod, layer, chunk, row_off):
    b, t, d = x.shape
    tt = _tile(t, 512)
    return pl.pallas_call(
        _modulate_kernel,
        name="modulate",
        grid=(b, t // tt),
        in_specs=[pl.BlockSpec((None, tt, d), lambda bi, i: (bi, i, 0)),
                  pl.BlockSpec((1, d), lambda bi, i: (0, 0)),
                  _mod_spec(layer, chunk, 1, d, lambda bi, i: (bi + row_off, 0, 0)),
                  _mod_spec(layer, chunk + 1, 1, d, lambda bi, i: (bi + row_off, 0, 0))],
        out_specs=pl.BlockSpec((None, tt, d), lambda bi, i: (bi, i, 0)),
        out_shape=jax.ShapeDtypeStruct((b, t, d), BF16),
        compiler_params=_params("arbitrary", "arbitrary"),
    )(x, g.reshape(1, d), mod, mod)


def _mm_kernel(a_ref, w_ref, o_ref):
    o_ref[...] = jnp.dot(a_ref[...], w_ref[...], preferred_element_type=F32).astype(o_ref.dtype)


def _matmul(a, w, out_dtype):
    m, k = a.shape
    n = w.shape[1]
    tm, tn = _tile(m, 1024), _tile(n, 1024)
    return pl.pallas_call(
        _mm_kernel,
        name="matmul",
        grid=(m // tm, n // tn),
        in_specs=[pl.BlockSpec((tm, k), lambda i, j: (i, 0)),
                  pl.BlockSpec((k, tn), lambda i, j: (0, j))],
        out_specs=pl.BlockSpec((tm, tn), lambda i, j: (i, j)),
        out_shape=jax.ShapeDtypeStruct((m, n), out_dtype),
        compiler_params=_params("arbitrary", "arbitrary"),
    )(a, w)


def _kv_proj_kernel(a_ref, w_ref, *refs, tm):
    rows_ref, heads_ref = refs[-2:]
    y = jnp.dot(a_ref[...], w_ref[...], preferred_element_type=F32)
    rows_ref[...] = y
    for h in range(H_C):
        heads_ref[pl.ds(h, tm, stride=H_C), :] = y[:, h * HD_C:(h + 1) * HD_C]


def _kv_proj(a, w, stacked, layer, n_layers):
    m, k = a.shape
    n = w.shape[1]
    tm = _tile(m, 512)
    in_specs = [pl.BlockSpec((tm, k), lambda i: (i, 0)),
                pl.BlockSpec((k, n), lambda i: (0, 0))]
    args = [a, w]
    aliases = {}
    if stacked is not None:
        in_specs.append(pl.BlockSpec(memory_space=pl.ANY))
        args.append(stacked)
        aliases = {2: 1}
    return pl.pallas_call(
        functools.partial(_kv_proj_kernel, tm=tm),
        name="kv_proj",
        grid=(m // tm,),
        in_specs=in_specs,
        out_specs=[pl.BlockSpec((tm, n), lambda i: (i, 0)),
                   pl.BlockSpec((None, tm * H_C, HD_C), lambda i: (layer, i, 0))],
        out_shape=[jax.ShapeDtypeStruct((m, n), F32),
                   jax.ShapeDtypeStruct((n_layers, m * H_C, HD_C), F32)],
        input_output_aliases=aliases,
        compiler_params=_params("arbitrary"),
    )(*args)


def _rope_tables(pos):
    half = HD_B // 2
    freqs = ROPE_THETA ** (-jnp.arange(half, dtype=F32) / half)
    ang = pos.astype(F32)[:, None] * freqs[None, :]
    cos, sin = jnp.cos(ang), jnp.sin(ang)
    zero = jnp.zeros_like(sin)
    rep = LANES // HD_B
    cos_t = jnp.tile(jnp.concatenate([cos, cos], axis=1), (1, rep))
    sa = jnp.tile(jnp.concatenate([-sin, zero], axis=1), (1, rep))
    sb = jnp.tile(jnp.concatenate([zero, sin], axis=1), (1, rep))
    return cos_t, sa, sb


def _rope(x, cos, sa, sb):
    half = HD_B // 2
    outs = []
    for g in range(x.shape[1] // LANES):
        xg = x[:, g * LANES:(g + 1) * LANES]
        outs.append(xg * cos + pltpu.roll(xg, LANES - half, 1) * sa + pltpu.roll(xg, half, 1) * sb)
    return outs[0] if len(outs) == 1 else jnp.concatenate(outs, axis=1)


def _even_proj_kernel(a_ref, w_ref, cos_ref, sa_ref, sb_ref, o_ref, *, tn):
    j = pl.program_id(1)
    z = jnp.dot(a_ref[...], w_ref[...], preferred_element_type=F32)
    q_lo, k_tile = Q_OFF // tn, K_OFF // tn
    kw = KV_B * HD_B

    @pl.when(j < q_lo)
    def _():
        o_ref[...] = z

    @pl.when(jnp.logical_and(j >= q_lo, j < k_tile))
    def _():
        o_ref[...] = _rope(z, cos_ref[...], sa_ref[...], sb_ref[...])

    @pl.when(j == k_tile)
    def _():
        o_ref[:, :kw] = _rope(z[:, :kw], cos_ref[...], sa_ref[...], sb_ref[...])
        o_ref[:, kw:] = z[:, kw:]


def _even_proj(a, w, tables, period):
    m, k = a.shape
    n = w.shape[1]
    tn = 2 * KV_B * HD_B
    tm = _tile(period, 1024) if m > period else m
    per_blocks = max(period // tm, 1)
    tab_spec = pl.BlockSpec((tm, LANES), lambda i, j: (i % per_blocks, 0))
    return pl.pallas_call(
        functools.partial(_even_proj_kernel, tn=tn),
        name="even_proj",
        grid=(m // tm, n // tn),
        in_specs=[pl.BlockSpec((tm, k), lambda i, j: (i, 0)),
                  pl.BlockSpec((k, tn), lambda i, j: (0, j)),
                  tab_spec, tab_spec, tab_spec],
        out_specs=pl.BlockSpec((tm, tn), lambda i, j: (i, j)),
        out_shape=jax.ShapeDtypeStruct((m, n), F32),
        compiler_params=_params("arbitrary", "arbitrary"),
    )(a, w, *tables)


def _residual_kernel(a_ref, w_ref, x_ref, gate_ref, g_ref, sh_ref, sc_ref, xo_ref, ho_ref, *, tn):
    tm, d = xo_ref.shape
    a = a_ref[...]
    gate = _bcast_rows(gate_ref[...], tm)
    ss = jnp.zeros((tm, 1), F32)
    for j in range(d // tn):
        cs = slice(j * tn, (j + 1) * tn)
        xn = x_ref[:, cs] + gate[:, cs] * jnp.dot(a, w_ref[:, cs], preferred_element_type=F32)
        xo_ref[:, cs] = xn
        ss = ss + jnp.sum(xn * xn, axis=-1, keepdims=True)
    gs = _bcast_rows(g_ref[...] * (1.0 + sc_ref[...]), tm)
    h = xo_ref[...] * lax.rsqrt(ss / d + EPS) * gs + _bcast_rows(sh_ref[...], tm)
    ho_ref[...] = h.astype(ho_ref.dtype)


def _residual_matmul(a, w, x, g, mod, layer, rows_per_batch, row_off):
    m, k = a.shape
    d = w.shape[1]
    tm, nb, row_fn = _row_tiling(m, rows_per_batch, 512)
    vec = lambda chunk: _mod_spec(layer, chunk, nb, d, lambda i: (row_fn(i, row_off), 0, 0))
    return pl.pallas_call(
        functools.partial(_residual_kernel, tn=512),
        name="residual",
        grid=(m // tm,),
        in_specs=[pl.BlockSpec((tm, k), lambda i: (i, 0)),
                  pl.BlockSpec((k, d), lambda i: (0, 0)),
                  pl.BlockSpec((tm, d), lambda i: (i, 0)),
                  vec(2),
                  pl.BlockSpec((1, 1, d), lambda i: (0, 0, 0)),
                  vec(3), vec(4)],
        out_specs=[pl.BlockSpec((tm, d), lambda i: (i, 0)),
                   pl.BlockSpec((tm, d), lambda i: (i, 0))],
        out_shape=[jax.ShapeDtypeStruct((m, d), F32), jax.ShapeDtypeStruct((m, d), BF16)],
        compiler_params=_params("arbitrary"),
    )(a, w, x, mod, g.reshape(1, 1, d), mod, mod)


def _ffn_kernel(h_ref, x_ref, gate_ref, wg_ref, wu_ref, wd_ref, g_ref, *refs, final, rows):
    if final:
        y_ref, acc_ref = refs
    else:
        sh_ref, sc_ref, xo_ref, ho_ref, acc_ref = refs
    f = pl.program_id(1)
    tm, d = acc_ref.shape

    @pl.when(f == 0)
    def _():
        acc_ref[...] = jnp.zeros_like(acc_ref)

    h = h_ref[...]
    g = jnp.dot(h, wg_ref[...], preferred_element_type=F32)
    u = jnp.dot(h, wu_ref[...], preferred_element_type=F32)
    a = (g * jax.nn.sigmoid(g) * u).astype(BF16)
    acc_ref[...] += jnp.dot(a, wd_ref[...], preferred_element_type=F32)

    @pl.when(f == pl.num_programs(1) - 1)
    def _():
        nb = gate_ref.shape[0]
        per = tm // nb

        def chunk(r, carry):
            rs = pl.ds(pl.multiple_of(r * rows, rows), rows)
            b = (r * rows) // per
            xn = x_ref[rs, :] + gate_ref[b] * acc_ref[rs, :]
            if final:
                y_ref[rs, :] = xn * _rms_scale(xn) * g_ref[0]
            else:
                xo_ref[rs, :] = xn
                hn = xn * _rms_scale(xn) * (g_ref[0] * (1.0 + sc_ref[b])) + sh_ref[b]
                ho_ref[rs, :] = hn.astype(ho_ref.dtype)
            return carry

        lax.fori_loop(0, tm // rows, chunk, 0)


def _ffn(h, x, wg, wu, wd, g_next, mod, layer, rows_per_batch, row_off, final):
    m, d = h.shape
    ff = wg.shape[1]
    tf = _tile(ff, 512)
    tm, nb, row_fn = _row_tiling(m, rows_per_batch, 512)
    vec = lambda lyr, chunk: _mod_spec(lyr, chunk, nb, d, lambda i, f: (row_fn(i, row_off), 0, 0))
    rows_spec = pl.BlockSpec((tm, d), lambda i, f: (i, 0))
    in_specs = [rows_spec, rows_spec, vec(layer, 5),
                pl.BlockSpec((d, tf), lambda i, f: (0, f)),
                pl.BlockSpec((d, tf), lambda i, f: (0, f)),
                pl.BlockSpec((tf, d), lambda i, f: (f, 0)),
                pl.BlockSpec((1, 1, d), lambda i, f: (0, 0, 0))]
    args = [h, x, mod, wg, wu, wd, g_next.reshape(1, 1, d)]
    if final:
        out_specs = rows_spec
        out_shape = jax.ShapeDtypeStruct((m, d), F32)
    else:
        in_specs += [vec(layer + 1, 0), vec(layer + 1, 1)]
        args += [mod, mod]
        out_specs = [rows_spec, rows_spec]
        out_shape = [jax.ShapeDtypeStruct((m, d), F32), jax.ShapeDtypeStruct((m, d), BF16)]
    return pl.pallas_call(
        functools.partial(_ffn_kernel, final=final, rows=2 * SUBLANES),
        name="ffn",
        grid=(m // tm, ff // tf),
        in_specs=in_specs,
        out_specs=out_specs,
        out_shape=out_shape,
        scratch_shapes=[pltpu.VMEM((tm, d), F32)],
        compiler_params=_params("arbitrary", "arbitrary"),
    )(*args)


def _glu_rows(z_ref, uext_ref, rows):
    step = min(rows, 4 * SUBLANES)
    for r in range(0, rows, step):
        uext_ref[HIST_PAD + r:HIST_PAD + r + step, :] = (
            z_ref[r:r + step, 0:C_CONV] * jax.nn.sigmoid(z_ref[r:r + step, C_CONV:2 * C_CONV]))


def _conv_ln_silu(uext_ref, wdw_ref, bdw_ref, lng_ref, lnb_ref, y_ref, shift_ref, out_ref, rows):
    base = HIST_PAD - (CONV_W - 1)
    for c in range(C_CONV // LANES):
        cs = slice(c * LANES, (c + 1) * LANES)
        for r in range(SUBLANES):
            n = rows + SUBLANES * ((CONV_W - 1 - r) // SUBLANES)
            shift_ref[r, 0:n, :] = uext_ref[base + r:base + r + n, cs]
        acc = jnp.zeros((rows, LANES), F32)
        for j in range(CONV_W):
            r, a = j % SUBLANES, j // SUBLANES
            acc = acc + wdw_ref[j:j + 1, cs] * shift_ref[r, SUBLANES * a:SUBLANES * a + rows, :]
        y_ref[:, cs] = acc + bdw_ref[:, cs]
    step = min(rows, 4 * SUBLANES)
    for r in range(0, rows, step):
        y = y_ref[r:r + step, :]
        yc = y - jnp.mean(y, axis=-1, keepdims=True)
        var = jnp.mean(yc * yc, axis=-1, keepdims=True)
        yn = yc * lax.rsqrt(var + EPS) * lng_ref[...] + lnb_ref[...]
        out_ref[r:r + step, 0:C_CONV] = (yn * jax.nn.sigmoid(yn)).astype(out_ref.dtype)


def _sink_attention(q, k, v, sink_col, mask):
    s = lax.dot_general(q, k, (((1,), (1,)), ((), ())), preferred_element_type=F32) / math.sqrt(HD_B)
    if mask is not None:
        s = jnp.where(mask, s, NEG)
    m = jnp.maximum(jnp.max(s, axis=-1, keepdims=True), sink_col)
    p = jnp.exp(s - m)
    den = jnp.sum(p, axis=-1, keepdims=True) + jnp.exp(sink_col - m)
    return jnp.dot(p.astype(BF16), v, preferred_element_type=F32) / den


def _sink_column(sink_ref, kv_head, rows):
    return jnp.concatenate(
        [jnp.full((rows, 1), sink_ref[kv_head * G_B + g], F32) for g in range(G_B)], axis=0)


def _group_attention(sink_ref, z_ref, kall, vall, r0, rows, mask, cat_ref):
    for kh in range(KV_B):
        hs = slice(kh * HD_B, (kh + 1) * HD_B)
        q = jnp.concatenate(
            [z_ref[r0:r0 + rows, Q_OFF + (kh * G_B + g) * HD_B:Q_OFF + (kh * G_B + g + 1) * HD_B]
             for g in range(G_B)], axis=0).astype(BF16)
        o = _sink_attention(q, kall[:, hs], vall[:, hs], _sink_column(sink_ref, kh, rows), mask)
        for gp in range(G_B // 2):
            pair = jnp.concatenate([o[(2 * gp) * rows:(2 * gp + 1) * rows],
                                    o[(2 * gp + 1) * rows:(2 * gp + 2) * rows]], axis=1)
            c0 = C_CONV + (kh * G_B + 2 * gp) * HD_B
            cat_ref[r0:r0 + rows, c0:c0 + 2 * HD_B] = pair.astype(cat_ref.dtype)


def _even_mix_kernel(sink_ref, zc_ref, zh_ref, zkv_ref, wdw_ref, bdw_ref, lng_ref, lnb_ref,
                     cat_ref, cst_ref, uext_ref, y_ref, shift_ref, *, tt):
    i = pl.program_id(1)
    first = i == 0

    _glu_rows(zc_ref, uext_ref, tt)
    hist = zh_ref[:, 0:C_CONV] * jax.nn.sigmoid(zh_ref[:, C_CONV:2 * C_CONV])
    uext_ref[0:HIST_PAD, :] = jnp.where(first, 0.0, hist)
    _conv_ln_silu(uext_ref, wdw_ref, bdw_ref, lng_ref, lnb_ref, y_ref, shift_ref, cat_ref, tt)

    @pl.when(i == pl.num_programs(1) - 1)
    def _():
        cst_ref[...] = uext_ref[tt:tt + HIST_PAD, :]

    kw = KV_B * HD_B
    sub = 2 * CHUNK
    kext = jnp.concatenate([zkv_ref[:, 0:kw], zc_ref[:, K_OFF:K_OFF + kw]], axis=0).astype(BF16)
    vext = jnp.concatenate([zkv_ref[:, kw:2 * kw], zc_ref[:, V_OFF:V_OFF + kw]], axis=0).astype(BF16)
    rows = G_B * sub
    qc = (lax.broadcasted_iota(jnp.int32, (rows, 2 * sub), 0) % sub) // CHUNK
    kc = lax.broadcasted_iota(jnp.int32, (rows, 2 * sub), 1) // CHUNK
    band = jnp.logical_and(kc >= qc, kc <= qc + WINDOW // CHUNK)
    band_first = jnp.logical_and(band, jnp.logical_or(jnp.logical_not(first), kc >= WINDOW // CHUNK))
    for s in range(tt // sub):
        r0 = s * sub
        _group_attention(sink_ref, zc_ref, kext[r0:r0 + 2 * sub], vext[r0:r0 + 2 * sub], r0, sub,
                         band_first if s == 0 else band, cat_ref)


def _even_mix(z, sinks, w_dw, b_dw, ln_g, ln_b):
    b, t, _ = z.shape
    tt = _tile(t, 256)
    hb = tt // HIST_PAD
    kb = tt // WINDOW
    kvw = 2 * KV_B * HD_B
    vec = pl.BlockSpec((1, C_CONV), lambda bi, i: (0, 0))
    return pl.pallas_call(
        functools.partial(_even_mix_kernel, tt=tt),
        name="even_mix",
        grid=(b, t // tt),
        in_specs=[pl.BlockSpec(memory_space=pltpu.SMEM),
                  pl.BlockSpec((None, tt, E_IN), lambda bi, i: (bi, i, 0)),
                  pl.BlockSpec((None, HIST_PAD, 2 * C_CONV),
                               lambda bi, i: (bi, jnp.maximum(i * hb - 1, 0), 0)),
                  pl.BlockSpec((None, WINDOW, kvw),
                               lambda bi, i: (bi, jnp.maximum(i * kb - 1, 0), K_OFF // kvw)),
                  pl.BlockSpec((CONV_W, C_CONV), lambda bi, i: (0, 0)),
                  vec, vec, vec],
        out_specs=[pl.BlockSpec((None, tt, D_MODEL), lambda bi, i: (bi, i, 0)),
                   pl.BlockSpec((None, HIST_PAD, C_CONV), lambda bi, i: (bi, 0, 0))],
        out_shape=[jax.ShapeDtypeStruct((b, t, D_MODEL), BF16),
                   jax.ShapeDtypeStruct((b, HIST_PAD, C_CONV), F32)],
        scratch_shapes=[pltpu.VMEM((HIST_PAD + tt, C_CONV), F32),
                        pltpu.VMEM((tt, C_CONV), F32),
                        pltpu.VMEM((SUBLANES, tt + HIST_PAD, LANES), F32)],
        compiler_params=_params("arbitrary", "arbitrary"),
    )(sinks, z, z, z, w_dw, b_dw.reshape(1, -1), ln_g.reshape(1, -1), ln_b.reshape(1, -1))


def _even_mix_sample_kernel(sink_ref, z_ref, hist_ref, kc_ref, vc_ref, wdw_ref, bdw_ref, lng_ref,
                            lnb_ref, cat_ref, cst_ref, uext_ref, y_ref, shift_ref, *, t):
    uext_ref[0:HIST_PAD, :] = hist_ref[...]
    _glu_rows(z_ref, uext_ref, t)
    _conv_ln_silu(uext_ref, wdw_ref, bdw_ref, lng_ref, lnb_ref, y_ref, shift_ref, cat_ref, t)
    cst_ref[...] = uext_ref[t:t + HIST_PAD, :]

    kw = KV_B * HD_B
    kall = jnp.concatenate([kc_ref[...], z_ref[:, K_OFF:K_OFF + kw]], axis=0).astype(BF16)
    vall = jnp.concatenate([vc_ref[...], z_ref[:, V_OFF:V_OFF + kw]], axis=0).astype(BF16)
    _group_attention(sink_ref, z_ref, kall, vall, 0, t, None, cat_ref)


def _even_mix_sample(z, hist, k_cache, v_cache, sinks, w_dw, b_dw, ln_g, ln_b):
    b, t, _ = z.shape
    win = k_cache.shape[1]
    kw = KV_B * HD_B
    vec = pl.BlockSpec((1, C_CONV), lambda bi: (0, 0))
    return pl.pallas_call(
        functools.partial(_even_mix_sample_kernel, t=t),
        name="even_mix_sample",
        grid=(b,),
        in_specs=[pl.BlockSpec(memory_space=pltpu.SMEM),
                  pl.BlockSpec((None, t, E_IN), lambda bi: (bi, 0, 0)),
                  pl.BlockSpec((None, HIST_PAD, C_CONV), lambda bi: (bi, 0, 0)),
                  pl.BlockSpec((None, win, kw), lambda bi: (bi, 0, 0)),
                  pl.BlockSpec((None, win, kw), lambda bi: (bi, 0, 0)),
                  pl.BlockSpec((CONV_W, C_CONV), lambda bi: (0, 0)),
                  vec, vec, vec],
        out_specs=[pl.BlockSpec((None, t, D_MODEL), lambda bi: (bi, 0, 0)),
                   pl.BlockSpec((None, HIST_PAD, C_CONV), lambda bi: (bi, 0, 0))],
        out_shape=[jax.ShapeDtypeStruct((b, t, D_MODEL), BF16),
                   jax.ShapeDtypeStruct((b, HIST_PAD, C_CONV), F32)],
        scratch_shapes=[pltpu.VMEM((HIST_PAD + t, C_CONV), F32),
                        pltpu.VMEM((t, C_CONV), F32),
                        pltpu.VMEM((SUBLANES, t + HIST_PAD, LANES), F32)],
        compiler_params=_params("arbitrary"),
    )(sinks, z, hist, k_cache.reshape(b, win, kw), v_cache.reshape(b, win, kw),
      w_dw, b_dw.reshape(1, -1), ln_g.reshape(1, -1), ln_b.reshape(1, -1))


def _scan_kernel(x_ref, b_ref, lf_ref, f_ref, ft_ref, carry_ref, *, tt, log_sigmoid):
    i = pl.program_id(1)

    @pl.when(i == 0)
    def _():
        carry_ref[...] = jnp.zeros_like(carry_ref)

    x = x_ref[...]
    if log_sigmoid:
        x = x + b_ref[...]
        x = jnp.minimum(x, 0.0) - jnp.log1p(jnp.exp(-jnp.abs(x)))
    lf_ref[...] = x[:, :H_C]
    row = lax.broadcasted_iota(jnp.int32, x.shape, 0)
    step = 1
    while step < tt:
        x = x + jnp.where(row >= step, pltpu.roll(x, step, 0), 0.0)
        step *= 2
    f = x + carry_ref[...]
    carry_ref[...] = f[tt - 1:tt, :]
    f_ref[...] = f
    if tt < LANES:
        f = jnp.concatenate([f, jnp.zeros((LANES - tt, LANES), F32)], axis=0)
    ft_ref[...] = f.T[:H_C, :tt]


def _forget_scan(x, bias, log_sigmoid):
    b, t, _ = x.shape
    tt = _tile(t, 512)
    return pl.pallas_call(
        functools.partial(_scan_kernel, tt=tt, log_sigmoid=log_sigmoid),
        name="forget_scan",
        grid=(b, t // tt),
        in_specs=[pl.BlockSpec((None, tt, LANES), lambda bi, i: (bi, i, 0)),
                  pl.BlockSpec((1, LANES), lambda bi, i: (0, 0))],
        out_specs=[pl.BlockSpec((None, tt, H_C), lambda bi, i: (bi, i, 0)),
                   pl.BlockSpec((None, tt, LANES), lambda bi, i: (bi, i, 0)),
                   pl.BlockSpec((None, H_C, tt), lambda bi, i: (bi, 0, i))],
        out_shape=[jax.ShapeDtypeStruct((b, t, H_C), F32),
                   jax.ShapeDtypeStruct((b, t, LANES), F32),
                   jax.ShapeDtypeStruct((b, H_C, t), F32)],
        scratch_shapes=[pltpu.VMEM((1, LANES), F32)],
        compiler_params=_params("arbitrary", "arbitrary"),
    )(x, bias)


def _lane_fold(x, op):
    out = x[:, 0:LANES]
    for c in range(1, x.shape[1] // LANES):
        out = op(out, x[:, c * LANES:(c + 1) * LANES])
    return out


def _fox_kernel(q_ref, k_ref, v_ref, fq_ref, fk_ref, o_ref, s_ref, stat_ref, acc_ref, *, tq):
    h = pl.program_id(1)
    i = pl.program_id(2)
    q = q_ref[...]
    lane = lax.broadcasted_iota(jnp.int32, (tq, LANES), 1)
    fq = jnp.sum(jnp.where(lane == h, fq_ref[...], 0.0), axis=-1, keepdims=True)
    causal = (lax.broadcasted_iota(jnp.int32, (tq, tq), 1)
              <= lax.broadcasted_iota(jnp.int32, (tq, tq), 0))
    stat_ref[...] = jnp.full_like(stat_ref, NEG)

    def logits(j, masked):
        start = pl.multiple_of(j * tq, tq)
        kj = k_ref[pl.ds(start, tq), :].astype(BF16)
        s = lax.dot_general(q, kj, (((1,), (1,)), ((), ())), preferred_element_type=F32)
        s = s / math.sqrt(HD_C) + fq - fk_ref[j]
        if masked:
            s = jnp.where(causal, s, NEG)
        s_ref[j] = s
        stat_ref[...] = jnp.maximum(stat_ref[...], _lane_fold(s, jnp.maximum))

    def logits_body(j, carry):
        logits(j, False)
        return carry

    lax.fori_loop(0, i, logits_body, 0)
    logits(i, True)

    m = jnp.max(stat_ref[...], axis=-1, keepdims=True)
    m_wide = jnp.broadcast_to(m, (tq, tq))
    stat_ref[...] = jnp.zeros_like(stat_ref)
    acc_ref[...] = jnp.zeros_like(acc_ref)

    def weigh(j, carry):
        start = pl.multiple_of(j * tq, tq)
        p = jnp.exp(s_ref[j] - m_wide)
        stat_ref[...] += _lane_fold(p, jnp.add)
        acc_ref[...] += jnp.dot(p.astype(BF16), v_ref[pl.ds(start, tq), :].astype(BF16),
                                preferred_element_type=F32)
        return carry

    lax.fori_loop(0, i + 1, weigh, 0)
    den = jnp.sum(stat_ref[...], axis=-1, keepdims=True)
    o_ref[...] = (acc_ref[...] / den).astype(o_ref.dtype)


def _fox_prompt(q, k, v, f_rows, f_heads):
    b, t, _ = q.shape
    tq = _tile(t, 512)
    nq = t // tq
    fk = f_heads.reshape(b, H_C, nq, 1, tq)
    return pl.pallas_call(
        functools.partial(_fox_kernel, tq=tq),
        name="fox_prompt",
        grid=(b, H_C, nq),
        in_specs=[pl.BlockSpec((None, tq, HD_C), lambda bi, h, i: (bi, i, h)),
                  pl.BlockSpec((None, t, HD_C), lambda bi, h, i: (bi, 0, h)),
                  pl.BlockSpec((None, t, HD_C), lambda bi, h, i: (bi, 0, h)),
                  pl.BlockSpec((None, tq, LANES), lambda bi, h, i: (bi, i, 0)),
                  pl.BlockSpec((None, None, nq, 1, tq), lambda bi, h, i: (bi, h, 0, 0, 0))],
        out_specs=pl.BlockSpec((None, tq, HD_C), lambda bi, h, i: (bi, i, h)),
        out_shape=jax.ShapeDtypeStruct((b, t, H_C * HD_C), BF16),
        scratch_shapes=[pltpu.VMEM((nq, tq, tq), F32), pltpu.VMEM((tq, LANES), F32),
                        pltpu.VMEM((tq, HD_C), F32)],
        compiler_params=_params("arbitrary", "arbitrary", "arbitrary"),
    )(q, k, v, f_rows, fk)


def _online_update(s, v, m_ref, l_ref, acc_ref):
    m_prev = m_ref[...]
    m_new = jnp.maximum(m_prev, jnp.max(s, axis=-1, keepdims=True))
    alpha = jnp.exp(m_prev - m_new)
    p = jnp.exp(s - m_new)
    l_ref[...] = alpha * l_ref[...] + jnp.sum(p, axis=-1, keepdims=True)
    acc_ref[...] = alpha * acc_ref[...] + jnp.dot(p.astype(BF16), v, preferred_element_type=F32)
    m_ref[...] = m_new


def _fox_sample_kernel(q_ref, kc_ref, vc_ref, kn_ref, vn_ref, fct_ref, fn_ref, fnt_ref,
                       o_ref, m_ref, l_ref, acc_ref, *, t, tk):
    j = pl.program_id(1)
    last = pl.num_programs(1) - 1

    @pl.when(j == 0)
    def _():
        m_ref[...] = jnp.full_like(m_ref, NEG)
        l_ref[...] = jnp.zeros_like(l_ref)
        acc_ref[...] = jnp.zeros_like(acc_ref)

    f_end = fnt_ref[:, t:t + 1]
    for h in range(H_C):
        hs = slice(h * HD_C, (h + 1) * HD_C)
        kh = kc_ref[pl.ds(h, tk, stride=H_C), :].astype(BF16)
        vh = vc_ref[pl.ds(h, tk, stride=H_C), :].astype(BF16)
        s = lax.dot_general(q_ref[:, hs], kh, (((1,), (1,)), ((), ())), preferred_element_type=F32)
        s = s / math.sqrt(HD_C) + (fn_ref[:, h:h + 1] + f_end[h:h + 1, :]) - fct_ref[h:h + 1, :]
        _online_update(s, vh, m_ref.at[h], l_ref.at[h], acc_ref.at[h])

    @pl.when(j == last)
    def _():
        causal = (lax.broadcasted_iota(jnp.int32, (t, t), 1)
                  <= lax.broadcasted_iota(jnp.int32, (t, t), 0))
        for h in range(H_C):
            hs = slice(h * HD_C, (h + 1) * HD_C)
            s = lax.dot_general(q_ref[:, hs], kn_ref[:, hs].astype(BF16), (((1,), (1,)), ((), ())),
                                preferred_element_type=F32)
            s = s / math.sqrt(HD_C) + fn_ref[:, h:h + 1] - fnt_ref[h:h + 1, 0:t]
            _online_update(jnp.where(causal, s, NEG), vn_ref[:, hs].astype(BF16),
                           m_ref.at[h], l_ref.at[h], acc_ref.at[h])
            o_ref[:, hs] = (acc_ref[h] / l_ref[h]).astype(o_ref.dtype)


def _fox_sample(q, k_new, v_new, k_cache, v_cache, layer, fc_heads, fn_rows, fn_heads_ext):
    b, t, d = q.shape
    n_layers, _, p = k_cache.shape[:3]
    tk = _tile(p, 512)
    cache_spec = pl.BlockSpec((None, None, tk * H_C, HD_C), lambda bi, j: (layer, bi, j, 0))
    return pl.pallas_call(
        functools.partial(_fox_sample_kernel, t=t, tk=tk),
        name="fox_sample",
        grid=(b, p // tk),
        in_specs=[pl.BlockSpec((None, t, d), lambda bi, j: (bi, 0, 0)),
                  cache_spec, cache_spec,
                  pl.BlockSpec((None, t, d), lambda bi, j: (bi, 0, 0)),
                  pl.BlockSpec((None, t, d), lambda bi, j: (bi, 0, 0)),
                  pl.BlockSpec((None, H_C, tk), lambda bi, j: (bi, 0, j)),
                  pl.BlockSpec((None, t, LANES), lambda bi, j: (bi, 0, 0)),
                  pl.BlockSpec((None, H_C, t + 1), lambda bi, j: (bi, 0, 0))],
        out_specs=pl.BlockSpec((None, t, d), lambda bi, j: (bi, 0, 0)),
        out_shape=jax.ShapeDtypeStruct((b, t, d), BF16),
        scratch_shapes=[pltpu.VMEM((H_C, t, 1), F32), pltpu.VMEM((H_C, t, 1), F32),
                        pltpu.VMEM((H_C, t, HD_C), F32)],
        compiler_params=_params("arbitrary", "arbitrary"),
    )(q, k_cache.reshape(n_layers, b, p * H_C, HD_C), v_cache.reshape(n_layers, b, p * H_C, HD_C),
      k_new, v_new, fc_heads, fn_rows, fn_heads_ext)


def kernel(x_prompt, x_sample, c_prompt, c_sample, cache_conv, cache_win_k, cache_win_v, cache_fox_k, cache_fox_v, cache_fox_logf, ada_w, ada_b, norm_mix_g, norm_ffn_g, e_w_in, e_w_dw, e_b_dw, e_ln_g, e_ln_b, e_sinks, e_w_out, o_w_in, o_b_f, o_w_out, ffn_w_gate, ffn_w_up, ffn_w_down, final_g):
    bp, tp, d = x_prompt.shape
    bs, ts, _ = x_sample.shape
    depth = ada_w.shape[0]
    n_odd = o_w_in.shape[0]
    past_len = cache_fox_k.shape[2]
    nqkv = H_C * HD_C

    mod = _adaln(jnp.concatenate([c_prompt, c_sample], axis=0), ada_w, ada_b)
    tab_p = _rope_tables(jnp.arange(tp))
    tab_s = tuple(jnp.tile(x, (bs, 1)) for x in _rope_tables(past_len + jnp.arange(ts)))

    xp = x_prompt.reshape(bp * tp, d)
    xs = x_sample.reshape(bs * ts, d)
    hp = _modulate(x_prompt, norm_mix_g[0], mod, 0, 0, 0).reshape(bp * tp, d)
    hs = _modulate(x_sample, norm_mix_g[0], mod, 0, 0, bp).reshape(bs * ts, d)
    outs = {name: [] for name in ("conv_p", "wk_p", "wv_p", "fl_p",
                                  "conv_s", "wk_s", "wv_s", "fk_s", "fv_s", "fl_s")}
    fk_p = fv_p = None
    for l in range(depth):
        if l % 2 == 0:
            e = l // 2
            w_in = e_w_in[e].astype(BF16)
            w_out = e_w_out[e].astype(BF16)
            conv = (e_w_dw[e], e_b_dw[e], e_ln_g[e], e_ln_b[e])
            zp = _even_proj(hp, w_in, tab_p, tp).reshape(bp, tp, E_IN)
            cat, cst = _even_mix(zp, e_sinks[e], *conv)
            win = min(WINDOW, tp)
            outs["conv_p"].append(cst[:, HIST_PAD - (CONV_W - 1):])
            outs["wk_p"].append(zp[:, tp - win:, K_OFF:V_OFF].reshape(bp, win, KV_B, HD_B))
            outs["wv_p"].append(zp[:, tp - win:, V_OFF:].reshape(bp, win, KV_B, HD_B))
            yp = cat.reshape(bp * tp, d)

            zs = _even_proj(hs, w_in, tab_s, bs * ts).reshape(bs, ts, E_IN)
            hist = jnp.pad(cache_conv[e], ((0, 0), (HIST_PAD - (CONV_W - 1), 0), (0, 0)))
            cat, cst = _even_mix_sample(zs, hist, cache_win_k[e], cache_win_v[e], e_sinks[e], *conv)
            outs["conv_s"].append(cst[:, HIST_PAD - (CONV_W - 1):])
            outs["wk_s"].append(zs[:, :, K_OFF:V_OFF].reshape(bs, ts, KV_B, HD_B))
            outs["wv_s"].append(zs[:, :, V_OFF:].reshape(bs, ts, KV_B, HD_B))
            ys = cat.reshape(bs * ts, d)
        else:
            o = l // 2
            w_in = o_w_in[o]
            wq = w_in[:, :nqkv].astype(BF16)
            wk = w_in[:, nqkv:2 * nqkv].astype(BF16)
            wv = w_in[:, 2 * nqkv:3 * nqkv].astype(BF16)
            wf = jnp.pad(w_in[:, 3 * nqkv:], ((0, 0), (0, LANES - H_C))).astype(BF16)
            bf = jnp.pad(o_b_f[o], (0, LANES - H_C)).reshape(1, LANES)
            w_out = o_w_out[o].astype(BF16)

            q = _matmul(hp, wq, BF16).reshape(bp, tp, nqkv)
            k, fk_p = _kv_proj(hp, wk, fk_p, o, n_odd)
            v, fv_p = _kv_proj(hp, wv, fv_p, o, n_odd)
            lf, f_rows, f_heads = _forget_scan(_matmul(hp, wf, F32).reshape(bp, tp, LANES), bf, True)
            outs["fl_p"].append(lf)
            yp = _fox_prompt(q, k.reshape(bp, tp, nqkv), v.reshape(bp, tp, nqkv),
                             f_rows, f_heads).reshape(bp * tp, d)

            q = _matmul(hs, wq, BF16).reshape(bs, ts, nqkv)
            k = _matmul(hs, wk, F32).reshape(bs, ts, nqkv)
            v = _matmul(hs, wv, F32).reshape(bs, ts, nqkv)
            lf, fn_rows, fn_heads = _forget_scan(_matmul(hs, wf, F32).reshape(bs, ts, LANES), bf, True)
            cache_lf = jnp.pad(cache_fox_logf[o], ((0, 0), (0, 0), (0, LANES - H_C)))
            _, _, fc_heads = _forget_scan(cache_lf, bf, False)
            fn_heads_ext = jnp.concatenate([fn_heads, fc_heads[:, :, past_len - 1:]], axis=2)
            outs["fk_s"].append(k.reshape(bs, ts, H_C, HD_C))
            outs["fv_s"].append(v.reshape(bs, ts, H_C, HD_C))
            outs["fl_s"].append(lf)
            ys = _fox_sample(q, k, v, cache_fox_k, cache_fox_v, o, fc_heads, fn_rows,
                             fn_heads_ext).reshape(bs * ts, d)

        xp, hp = _residual_matmul(yp, w_out, xp, norm_ffn_g[l], mod, l, tp, 0)
        xs, hs = _residual_matmul(ys, w_out, xs, norm_ffn_g[l], mod, l, ts, bp)

        wg = ffn_w_gate[l].astype(BF16)
        wu = ffn_w_up[l].astype(BF16)
        wd = ffn_w_down[l].astype(BF16)
        if l + 1 < depth:
            xp, hp = _ffn(hp, xp, wg, wu, wd, norm_mix_g[l + 1], mod, l, tp, 0, False)
            xs, hs = _ffn(hs, xs, wg, wu, wd, norm_mix_g[l + 1], mod, l, ts, bp, False)
        else:
            y_prompt = _ffn(hp, xp, wg, wu, wd, final_g, mod, l, tp, 0, True).reshape(bp, tp, d)
            y_sample = _ffn(hs, xs, wg, wu, wd, final_g, mod, l, ts, bp, True).reshape(bs, ts, d)

    stack = lambda name: jnp.stack(outs[name])
    return (y_prompt, y_sample,
            stack("conv_p"), stack("wk_p"), stack("wv_p"),
            fk_p.reshape(n_odd, bp, tp, H_C, HD_C), fv_p.reshape(n_odd, bp, tp, H_C, HD_C), stack("fl_p"),
            stack("conv_s"), stack("wk_s"), stack("wv_s"), stack("fk_s"), stack("fv_s"), stack("fl_s"))
```

```python
import functools
import math

import jax
import jax.numpy as jnp
from jax import lax
from jax.experimental import pallas as pl
from jax.experimental.pallas import tpu as pltpu

F32 = jnp.float32
BF16 = jnp.bfloat16

D_MODEL = 2048
CHUNK = 64
C_CONV = D_MODEL // 2
CONV_W = 31
HIST_PAD = 32
HD_B = 64
H_B = (D_MODEL // 2) // HD_B
KV_B = H_B // 4
G_B = H_B // KV_B
WINDOW = 128
E_IN = 2 * C_CONV + H_B * HD_B + 2 * KV_B * HD_B
Q_OFF = 2 * C_CONV
K_OFF = Q_OFF + H_B * HD_B
V_OFF = K_OFF + KV_B * HD_B
HD_C = 128
H_C = D_MODEL // HD_C
ROPE_THETA = 10000.0
EPS = 1e-6
NEG = -1e30
LANES = 128
SUBLANES = 8
VMEM_LIMIT = 56 * 1024 * 1024


def _params(*sem):
    return pltpu.CompilerParams(dimension_semantics=sem, vmem_limit_bytes=VMEM_LIMIT)


def _tile(n, pref):
    t = min(n, pref)
    while n % t:
        t //= 2
    return t


def _bcast_rows(vec, rows):
    nb, _, w = vec.shape
    if nb == 1:
        return vec[0]
    return jnp.broadcast_to(vec, (nb, rows // nb, w)).reshape(rows, w)


def _mod_spec(layer, chunk, nb, width, index_fn):
    return pl.BlockSpec((None, None, nb, 1, width),
                        lambda *g: (layer, chunk) + tuple(index_fn(*g)))


def _row_tiling(m, rows_per_batch, pref):
    if m > rows_per_batch and m > pref:
        tm = _tile(rows_per_batch, pref)
        return tm, 1, lambda i, row_off: (i * tm) // rows_per_batch + row_off
    nb = max(m // rows_per_batch, 1)
    return m, nb, lambda i, row_off: row_off // nb + i


def _rms_scale(x):
    return lax.rsqrt(jnp.mean(x * x, axis=-1, keepdims=True) + EPS)


def _adaln_kernel(c_ref, w_ref, b_ref, o_ref):
    c = c_ref[...]
    a = (c * jax.nn.sigmoid(c)).astype(BF16)
    o_ref[...] = jnp.dot(a, w_ref[...].astype(BF16), preferred_element_type=F32) + b_ref[...]


def _adaln(c_all, ada_w, ada_b):
    n_layers, d, n = ada_w.shape
    nb = c_all.shape[0]
    tn = 1024
    per = d // tn
    out = pl.pallas_call(
        _adaln_kernel,
        name="adaln",
        grid=(n_layers, n // tn),
        in_specs=[pl.BlockSpec((nb, d), lambda l, j: (0, 0)),
                  pl.BlockSpec((None, d, tn), lambda l, j: (l, 0, j)),
                  pl.BlockSpec((None, 1, tn), lambda l, j: (l, 0, j))],
        out_specs=pl.BlockSpec((None, None, nb, tn), lambda l, j: (l, j // per, 0, j % per)),
        out_shape=jax.ShapeDtypeStruct((n_layers, n // d, nb, d), F32),
        compiler_params=_params("arbitrary", "arbitrary"),
    )(c_all, ada_w, ada_b.reshape(n_layers, 1, n))
    return out.reshape(n_layers, n // d, nb, 1, d)


def _modulate_kernel(x_ref, g_ref, sh_ref, sc_ref, o_ref):
    x = x_ref[...]
    y = x * _rms_scale(x) * (g_ref[...] * (1.0 + sc_ref[0]))
    o_ref[...] = (y + sh_ref[0]).astype(o_ref.dtype)


def _modulate(x, g, mod, layer, chunk, row_off):
    b, t, d = x.shape
    tt = _tile(t, 512)
    return pl.pallas_call(
        _modulate_kernel,
        name="modulate",
        grid=(b, t // tt),
        in_specs=[pl.BlockSpec((None, tt, d), lambda bi, i: (bi, i, 0)),
                  pl.BlockSpec((1, d), lambda bi, i: (0, 0)),
                  _mod_spec(layer, chunk, 1, d, lambda bi, i: (bi + row_off, 0, 0)),
                  _mod_spec(layer, chunk + 1, 1, d, lambda bi, i: (bi + row_off, 0, 0))],
        out_specs=pl.BlockSpec((None, tt, d), lambda bi, i: (bi, i, 0)),
        out_shape=jax.ShapeDtypeStruct((b, t, d), BF16),
        compiler_params=_params("arbitrary", "arbitrary"),
    )(x, g.reshape(1, d), mod, mod)


def _mm_kernel(a_ref, w_ref, o_ref):
    o_ref[...] = jnp.dot(a_ref[...], w_ref[...], preferred_element_type=F32).astype(o_ref.dtype)


def _matmul(a, w, out_dtype):
    m, k = a.shape
    n = w.shape[1]
    tm, tn = _tile(m, 1024), _tile(n, 1024)
    return pl.pallas_call(
        _mm_kernel,
        name="matmul",
        grid=(m // tm, n // tn),
        in_specs=[pl.BlockSpec((tm, k), lambda i, j: (i, 0)),
                  pl.BlockSpec((k, tn), lambda i, j: (0, j))],
        out_specs=pl.BlockSpec((tm, tn), lambda i, j: (i, j)),
        out_shape=jax.ShapeDtypeStruct((m, n), out_dtype),
        compiler_params=_params("arbitrary", "arbitrary"),
    )(a, w)


def _kv_proj_kernel(a_ref, w_ref, *refs, tm):
    rows_ref, heads_ref = refs[-2:]
    y = jnp.dot(a_ref[...], w_ref[...], preferred_element_type=F32)
    rows_ref[...] = y
    for h in range(H_C):
        heads_ref[pl.ds(h, tm, stride=H_C), :] = y[:, h * HD_C:(h + 1) * HD_C]


def _kv_proj(a, w, stacked, layer, n_layers):
    m, k = a.shape
    n = w.shape[1]
    tm = _tile(m, 512)
    in_specs = [pl.BlockSpec((tm, k), lambda i: (i, 0)),
                pl.BlockSpec((k, n), lambda i: (0, 0))]
    args = [a, w]
    aliases = {}
    if stacked is not None:
        in_specs.append(pl.BlockSpec(memory_space=pl.ANY))
        args.append(stacked)
        aliases = {2: 1}
    return pl.pallas_call(
        functools.partial(_kv_proj_kernel, tm=tm),
        name="kv_proj",
        grid=(m // tm,),
        in_specs=in_specs,
        out_specs=[pl.BlockSpec((tm, n), lambda i: (i, 0)),
                   pl.BlockSpec((None, tm * H_C, HD_C), lambda i: (layer, i, 0))],
        out_shape=[jax.ShapeDtypeStruct((m, n), F32),
                   jax.ShapeDtypeStruct((n_layers, m * H_C, HD_C), F32)],
        input_output_aliases=aliases,
        compiler_params=_params("arbitrary"),
    )(*args)


def _rope_tables(pos):
    half = HD_B // 2
    freqs = ROPE_THETA ** (-jnp.arange(half, dtype=F32) / half)
    ang = pos.astype(F32)[:, None] * freqs[None, :]
    cos, sin = jnp.cos(ang), jnp.sin(ang)
    zero = jnp.zeros_like(sin)
    rep = LANES // HD_B
    cos_t = jnp.tile(jnp.concatenate([cos, cos], axis=1), (1, rep))
    sa = jnp.tile(jnp.concatenate([-sin, zero], axis=1), (1, rep))
    sb = jnp.tile(jnp.concatenate([zero, sin], axis=1), (1, rep))
    return cos_t, sa, sb


def _rope(x, cos, sa, sb):
    half = HD_B // 2
    outs = []
    for g in range(x.shape[1] // LANES):
        xg = x[:, g * LANES:(g + 1) * LANES]
        outs.append(xg * cos + pltpu.roll(xg, LANES - half, 1) * sa + pltpu.roll(xg, half, 1) * sb)
    return outs[0] if len(outs) == 1 else jnp.concatenate(outs, axis=1)


def _even_proj_kernel(a_ref, w_ref, cos_ref, sa_ref, sb_ref, o_ref, *, tn):
    j = pl.program_id(1)
    z = jnp.dot(a_ref[...], w_ref[...], preferred_element_type=F32)
    q_lo, k_tile = Q_OFF // tn, K_OFF // tn
    kw = KV_B * HD_B

    @pl.when(j < q_lo)
    def _():
        o_ref[...] = z

    @pl.when(jnp.logical_and(j >= q_lo, j < k_tile))
    def _():
        o_ref[...] = _rope(z, cos_ref[...], sa_ref[...], sb_ref[...])

    @pl.when(j == k_tile)
    def _():
        o_ref[:, :kw] = _rope(z[:, :kw], cos_ref[...], sa_ref[...], sb_ref[...])
        o_ref[:, kw:] = z[:, kw:]


def _even_proj(a, w, tables, period):
    m, k = a.shape
    n = w.shape[1]
    tn = 2 * KV_B * HD_B
    tm = _tile(period, 1024) if m > period else m
    per_blocks = max(period // tm, 1)
    tab_spec = pl.BlockSpec((tm, LANES), lambda i, j: (i % per_blocks, 0))
    return pl.pallas_call(
        functools.partial(_even_proj_kernel, tn=tn),
        name="even_proj",
        grid=(m // tm, n // tn),
        in_specs=[pl.BlockSpec((tm, k), lambda i, j: (i, 0)),
                  pl.BlockSpec((k, tn), lambda i, j: (0, j)),
                  tab_spec, tab_spec, tab_spec],
        out_specs=pl.BlockSpec((tm, tn), lambda i, j: (i, j)),
        out_shape=jax.ShapeDtypeStruct((m, n), F32),
        compiler_params=_params("arbitrary", "arbitrary"),
    )(a, w, *tables)


def _residual_kernel(a_ref, w_ref, x_ref, gate_ref, g_ref, sh_ref, sc_ref, xo_ref, ho_ref, *, tn):
    tm, d = xo_ref.shape
    a = a_ref[...]
    gate = _bcast_rows(gate_ref[...], tm)
    ss = jnp.zeros((tm, 1), F32)
    for j in range(d // tn):
        cs = slice(j * tn, (j + 1) * tn)
        xn = x_ref[:, cs] + gate[:, cs] * jnp.dot(a, w_ref[:, cs], preferred_element_type=F32)
        xo_ref[:, cs] = xn
        ss = ss + jnp.sum(xn * xn, axis=-1, keepdims=True)
    gs = _bcast_rows(g_ref[...] * (1.0 + sc_ref[...]), tm)
    h = xo_ref[...] * lax.rsqrt(ss / d + EPS) * gs + _bcast_rows(sh_ref[...], tm)
    ho_ref[...] = h.astype(ho_ref.dtype)


def _residual_matmul(a, w, x, g, mod, layer, rows_per_batch, row_off):
    m, k = a.shape
    d = w.shape[1]
    tm, nb, row_fn = _row_tiling(m, rows_per_batch, 512)
    vec = lambda chunk: _mod_spec(layer, chunk, nb, d, lambda i: (row_fn(i, row_off), 0, 0))
    return pl.pallas_call(
        functools.partial(_residual_kernel, tn=512),
        name="residual",
        grid=(m // tm,),
        in_specs=[pl.BlockSpec((tm, k), lambda i: (i, 0)),
                  pl.BlockSpec((k, d), lambda i: (0, 0)),
                  pl.BlockSpec((tm, d), lambda i: (i, 0)),
                  vec(2),
                  pl.BlockSpec((1, 1, d), lambda i: (0, 0, 0)),
                  vec(3), vec(4)],
        out_specs=[pl.BlockSpec((tm, d), lambda i: (i, 0)),
                   pl.BlockSpec((tm, d), lambda i: (i, 0))],
        out_shape=[jax.ShapeDtypeStruct((m, d), F32), jax.ShapeDtypeStruct((m, d), BF16)],
        compiler_params=_params("arbitrary"),
    )(a, w, x, mod, g.reshape(1, 1, d), mod, mod)


def _ffn_kernel(h_ref, x_ref, gate_ref, wg_ref, wu_ref, wd_ref, g_ref, *refs, final, rows):
    if final:
        y_ref, acc_ref = refs
    else:
        sh_ref, sc_ref, xo_ref, ho_ref, acc_ref = refs
    f = pl.program_id(1)
    tm, d = acc_ref.shape

    @pl.when(f == 0)
    def _():
        acc_ref[...] = jnp.zeros_like(acc_ref)

    h = h_ref[...]
    g = jnp.dot(h, wg_ref[...], preferred_element_type=F32)
    u = jnp.dot(h, wu_ref[...], preferred_element_type=F32)
    a = (g * jax.nn.sigmoid(g) * u).astype(BF16)
    acc_ref[...] += jnp.dot(a, wd_ref[...], preferred_element_type=F32)

    @pl.when(f == pl.num_programs(1) - 1)
    def _():
        nb = gate_ref.shape[0]
        per = tm // nb
        for r0 in range(0, tm, rows):
            rs = slice(r0, r0 + rows)
            b = r0 // per
            xn = x_ref[rs, :] + gate_ref[b] * acc_ref[rs, :]
            if final:
                y_ref[rs, :] = xn * _rms_scale(xn) * g_ref[0]
            else:
                xo_ref[rs, :] = xn
                hn = xn * _rms_scale(xn) * (g_ref[0] * (1.0 + sc_ref[b])) + sh_ref[b]
                ho_ref[rs, :] = hn.astype(ho_ref.dtype)


def _ffn(h, x, wg, wu, wd, g_next, mod, layer, rows_per_batch, row_off, final):
    m, d = h.shape
    ff = wg.shape[1]
    tf = _tile(ff, 512)
    tm, nb, row_fn = _row_tiling(m, rows_per_batch, 512)
    vec = lambda lyr, chunk: _mod_spec(lyr, chunk, nb, d, lambda i, f: (row_fn(i, row_off), 0, 0))
    rows_spec = pl.BlockSpec((tm, d), lambda i, f: (i, 0))
    in_specs = [rows_spec, rows_spec, vec(layer, 5),
                pl.BlockSpec((d, tf), lambda i, f: (0, f)),
                pl.BlockSpec((d, tf), lambda i, f: (0, f)),
                pl.BlockSpec((tf, d), lambda i, f: (f, 0)),
                pl.BlockSpec((1, 1, d), lambda i, f: (0, 0, 0))]
    args = [h, x, mod, wg, wu, wd, g_next.reshape(1, 1, d)]
    if final:
        out_specs = rows_spec
        out_shape = jax.ShapeDtypeStruct((m, d), F32)
    else:
        in_specs += [vec(layer + 1, 0), vec(layer + 1, 1)]
        args += [mod, mod]
        out_specs = [rows_spec, rows_spec]
        out_shape = [jax.ShapeDtypeStruct((m, d), F32), jax.ShapeDtypeStruct((m, d), BF16)]
    return pl.pallas_call(
        functools.partial(_ffn_kernel, final=final, rows=2 * SUBLANES),
        name="ffn",
        grid=(m // tm, ff // tf),
        in_specs=in_specs,
        out_specs=out_specs,
        out_shape=out_shape,
        scratch_shapes=[pltpu.VMEM((tm, d), F32)],
        compiler_params=_params("arbitrary", "arbitrary"),
    )(*args)


def _glu_rows(z_ref, uext_ref, rows):
    step = min(rows, 4 * SUBLANES)
    for r in range(0, rows, step):
        uext_ref[HIST_PAD + r:HIST_PAD + r + step, :] = (
            z_ref[r:r + step, 0:C_CONV] * jax.nn.sigmoid(z_ref[r:r + step, C_CONV:2 * C_CONV]))


def _conv_ln_silu(uext_ref, wdw_ref, bdw_ref, lng_ref, lnb_ref, y_ref, shift_ref, out_ref, rows):
    base = HIST_PAD - (CONV_W - 1)
    for c in range(C_CONV // LANES):
        cs = slice(c * LANES, (c + 1) * LANES)
        for r in range(SUBLANES):
            n = rows + SUBLANES * ((CONV_W - 1 - r) // SUBLANES)
            shift_ref[r, 0:n, :] = uext_ref[base + r:base + r + n, cs]
        acc = jnp.zeros((rows, LANES), F32)
        for j in range(CONV_W):
            r, a = j % SUBLANES, j // SUBLANES
            acc = acc + wdw_ref[j:j + 1, cs] * shift_ref[r, SUBLANES * a:SUBLANES * a + rows, :]
        y_ref[:, cs] = acc + bdw_ref[:, cs]
    step = min(rows, 4 * SUBLANES)
    for r in range(0, rows, step):
        y = y_ref[r:r + step, :]
        yc = y - jnp.mean(y, axis=-1, keepdims=True)
        var = jnp.mean(yc * yc, axis=-1, keepdims=True)
        yn = yc * lax.rsqrt(var + EPS) * lng_ref[...] + lnb_ref[...]
        out_ref[r:r + step, 0:C_CONV] = (yn * jax.nn.sigmoid(yn)).astype(out_ref.dtype)


def _sink_attention(q, k, v, sink_col, mask):
    s = lax.dot_general(q, k, (((1,), (1,)), ((), ())), preferred_element_type=F32) / math.sqrt(HD_B)
    if mask is not None:
        s = jnp.where(mask, s, NEG)
    m = jnp.maximum(jnp.max(s, axis=-1, keepdims=True), sink_col)
    p = jnp.exp(s - m)
    den = jnp.sum(p, axis=-1, keepdims=True) + jnp.exp(sink_col - m)
    return jnp.dot(p.astype(BF16), v, preferred_element_type=F32) / den


def _sink_column(sink_ref, kv_head, rows):
    return jnp.concatenate(
        [jnp.full((rows, 1), sink_ref[kv_head * G_B + g], F32) for g in range(G_B)], axis=0)


def _group_attention(sink_ref, z_ref, kall, vall, r0, rows, mask, cat_ref):
    for kh in range(KV_B):
        hs = slice(kh * HD_B, (kh + 1) * HD_B)
        q = jnp.concatenate(
            [z_ref[r0:r0 + rows, Q_OFF + (kh * G_B + g) * HD_B:Q_OFF + (kh * G_B + g + 1) * HD_B]
             for g in range(G_B)], axis=0).astype(BF16)
        o = _sink_attention(q, kall[:, hs], vall[:, hs], _sink_column(sink_ref, kh, rows), mask)
        for gp in range(G_B // 2):
            pair = jnp.concatenate([o[(2 * gp) * rows:(2 * gp + 1) * rows],
                                    o[(2 * gp + 1) * rows:(2 * gp + 2) * rows]], axis=1)
            c0 = C_CONV + (kh * G_B + 2 * gp) * HD_B
            cat_ref[r0:r0 + rows, c0:c0 + 2 * HD_B] = pair.astype(cat_ref.dtype)


def _even_mix_kernel(sink_ref, zc_ref, zh_ref, zkv_ref, wdw_ref, bdw_ref, lng_ref, lnb_ref,
                     cat_ref, cst_ref, uext_ref, y_ref, shift_ref, *, tt):
    i = pl.program_id(1)
    first = i == 0

    _glu_rows(zc_ref, uext_ref, tt)
    hist = zh_ref[:, 0:C_CONV] * jax.nn.sigmoid(zh_ref[:, C_CONV:2 * C_CONV])
    uext_ref[0:HIST_PAD, :] = jnp.where(first, 0.0, hist)
    _conv_ln_silu(uext_ref, wdw_ref, bdw_ref, lng_ref, lnb_ref, y_ref, shift_ref, cat_ref, tt)

    @pl.when(i == pl.num_programs(1) - 1)
    def _():
        cst_ref[...] = uext_ref[tt:tt + HIST_PAD, :]

    kw = KV_B * HD_B
    sub = 2 * CHUNK
    kext = jnp.concatenate([zkv_ref[:, 0:kw], zc_ref[:, K_OFF:K_OFF + kw]], axis=0).astype(BF16)
    vext = jnp.concatenate([zkv_ref[:, kw:2 * kw], zc_ref[:, V_OFF:V_OFF + kw]], axis=0).astype(BF16)
    rows = G_B * sub
    qc = (lax.broadcasted_iota(jnp.int32, (rows, 2 * sub), 0) % sub) // CHUNK
    kc = lax.broadcasted_iota(jnp.int32, (rows, 2 * sub), 1) // CHUNK
    band = jnp.logical_and(kc >= qc, kc <= qc + WINDOW // CHUNK)
    band_first = jnp.logical_and(band, jnp.logical_or(jnp.logical_not(first), kc >= WINDOW // CHUNK))
    for s in range(tt // sub):
        r0 = s * sub
        _group_attention(sink_ref, zc_ref, kext[r0:r0 + 2 * sub], vext[r0:r0 + 2 * sub], r0, sub,
                         band_first if s == 0 else band, cat_ref)


def _even_mix(z, sinks, w_dw, b_dw, ln_g, ln_b):
    b, t, _ = z.shape
    tt = _tile(t, 256)
    hb = tt // HIST_PAD
    kb = tt // WINDOW
    kvw = 2 * KV_B * HD_B
    vec = pl.BlockSpec((1, C_CONV), lambda bi, i: (0, 0))
    return pl.pallas_call(
        functools.partial(_even_mix_kernel, tt=tt),
        name="even_mix",
        grid=(b, t // tt),
        in_specs=[pl.BlockSpec(memory_space=pltpu.SMEM),
                  pl.BlockSpec((None, tt, E_IN), lambda bi, i: (bi, i, 0)),
                  pl.BlockSpec((None, HIST_PAD, 2 * C_CONV),
                               lambda bi, i: (bi, jnp.maximum(i * hb - 1, 0), 0)),
                  pl.BlockSpec((None, WINDOW, kvw),
                               lambda bi, i: (bi, jnp.maximum(i * kb - 1, 0), K_OFF // kvw)),
                  pl.BlockSpec((CONV_W, C_CONV), lambda bi, i: (0, 0)),
                  vec, vec, vec],
        out_specs=[pl.BlockSpec((None, tt, D_MODEL), lambda bi, i: (bi, i, 0)),
                   pl.BlockSpec((None, HIST_PAD, C_CONV), lambda bi, i: (bi, 0, 0))],
        out_shape=[jax.ShapeDtypeStruct((b, t, D_MODEL), BF16),
                   jax.ShapeDtypeStruct((b, HIST_PAD, C_CONV), F32)],
        scratch_shapes=[pltpu.VMEM((HIST_PAD + tt, C_CONV), F32),
                        pltpu.VMEM((tt, C_CONV), F32),
                        pltpu.VMEM((SUBLANES, tt + HIST_PAD, LANES), F32)],
        compiler_params=_params("arbitrary", "arbitrary"),
    )(sinks, z, z, z, w_dw, b_dw.reshape(1, -1), ln_g.reshape(1, -1), ln_b.reshape(1, -1))


def _even_mix_sample_kernel(sink_ref, z_ref, hist_ref, kc_ref, vc_ref, wdw_ref, bdw_ref, lng_ref,
                            lnb_ref, cat_ref, cst_ref, uext_ref, y_ref, shift_ref, *, t):
    uext_ref[0:HIST_PAD, :] = hist_ref[...]
    _glu_rows(z_ref, uext_ref, t)
    _conv_ln_silu(uext_ref, wdw_ref, bdw_ref, lng_ref, lnb_ref, y_ref, shift_ref, cat_ref, t)
    cst_ref[...] = uext_ref[t:t + HIST_PAD, :]

    kw = KV_B * HD_B
    kall = jnp.concatenate([kc_ref[...], z_ref[:, K_OFF:K_OFF + kw]], axis=0).astype(BF16)
    vall = jnp.concatenate([vc_ref[...], z_ref[:, V_OFF:V_OFF + kw]], axis=0).astype(BF16)
    _group_attention(sink_ref, z_ref, kall, vall, 0, t, None, cat_ref)


def _even_mix_sample(z, hist, k_cache, v_cache, sinks, w_dw, b_dw, ln_g, ln_b):
    b, t, _ = z.shape
    win = k_cache.shape[1]
    kw = KV_B * HD_B
    vec = pl.BlockSpec((1, C_CONV), lambda bi: (0, 0))
    return pl.pallas_call(
        functools.partial(_even_mix_sample_kernel, t=t),
        name="even_mix_sample",
        grid=(b,),
        in_specs=[pl.BlockSpec(memory_space=pltpu.SMEM),
                  pl.BlockSpec((None, t, E_IN), lambda bi: (bi, 0, 0)),
                  pl.BlockSpec((None, HIST_PAD, C_CONV), lambda bi: (bi, 0, 0)),
                  pl.BlockSpec((None, win, kw), lambda bi: (bi, 0, 0)),
                  pl.BlockSpec((None, win, kw), lambda bi: (bi, 0, 0)),
                  pl.BlockSpec((CONV_W, C_CONV), lambda bi: (0, 0)),
                  vec, vec, vec],
        out_specs=[pl.BlockSpec((None, t, D_MODEL), lambda bi: (bi, 0, 0)),
                   pl.BlockSpec((None, HIST_PAD, C_CONV), lambda bi: (bi, 0, 0))],
        out_shape=[jax.ShapeDtypeStruct((b, t, D_MODEL), BF16),
                   jax.ShapeDtypeStruct((b, HIST_PAD, C_CONV), F32)],
        scratch_shapes=[pltpu.VMEM((HIST_PAD + t, C_CONV), F32),
                        pltpu.VMEM((t, C_CONV), F32),
                        pltpu.VMEM((SUBLANES, t + HIST_PAD, LANES), F32)],
        compiler_params=_params("arbitrary"),
    )(sinks, z, hist, k_cache.reshape(b, win, kw), v_cache.reshape(b, win, kw),
      w_dw, b_dw.reshape(1, -1), ln_g.reshape(1, -1), ln_b.reshape(1, -1))


def _scan_kernel(x_ref, b_ref, lf_ref, f_ref, ft_ref, carry_ref, *, tt, log_sigmoid):
    i = pl.program_id(1)

    @pl.when(i == 0)
    def _():
        carry_ref[...] = jnp.zeros_like(carry_ref)

    x = x_ref[...]
    if log_sigmoid:
        x = x + b_ref[...]
        x = jnp.minimum(x, 0.0) - jnp.log1p(jnp.exp(-jnp.abs(x)))
    lf_ref[...] = x[:, :H_C]
    row = lax.broadcasted_iota(jnp.int32, x.shape, 0)
    step = 1
    while step < tt:
        x = x + jnp.where(row >= step, pltpu.roll(x, step, 0), 0.0)
        step *= 2
    f = x + carry_ref[...]
    carry_ref[...] = f[tt - 1:tt, :]
    f_ref[...] = f
    if tt < LANES:
        f = jnp.concatenate([f, jnp.zeros((LANES - tt, LANES), F32)], axis=0)
    ft_ref[...] = f.T[:H_C, :tt]


def _forget_scan(x, bias, log_sigmoid):
    b, t, _ = x.shape
    tt = _tile(t, 512)
    return pl.pallas_call(
        functools.partial(_scan_kernel, tt=tt, log_sigmoid=log_sigmoid),
        name="forget_scan",
        grid=(b, t // tt),
        in_specs=[pl.BlockSpec((None, tt, LANES), lambda bi, i: (bi, i, 0)),
                  pl.BlockSpec((1, LANES), lambda bi, i: (0, 0))],
        out_specs=[pl.BlockSpec((None, tt, H_C), lambda bi, i: (bi, i, 0)),
                   pl.BlockSpec((None, tt, LANES), lambda bi, i: (bi, i, 0)),
                   pl.BlockSpec((None, H_C, tt), lambda bi, i: (bi, 0, i))],
        out_shape=[jax.ShapeDtypeStruct((b, t, H_C), F32),
                   jax.ShapeDtypeStruct((b, t, LANES), F32),
                   jax.ShapeDtypeStruct((b, H_C, t), F32)],
        scratch_shapes=[pltpu.VMEM((1, LANES), F32)],
        compiler_params=_params("arbitrary", "arbitrary"),
    )(x, bias)


def _lane_fold(x, op):
    out = x[:, 0:LANES]
    for c in range(1, x.shape[1] // LANES):
        out = op(out, x[:, c * LANES:(c + 1) * LANES])
    return out


def _fox_kernel(q_ref, k_ref, v_ref, fq_ref, fk_ref, o_ref, s_ref, kb_ref, vb_ref, *, tq, t):
    h = pl.program_id(1)
    kb_ref[...] = k_ref[...].astype(BF16)
    vb_ref[...] = v_ref[...].astype(BF16)
    lane = lax.broadcasted_iota(jnp.int32, (tq, LANES), 1)
    causal = (lax.broadcasted_iota(jnp.int32, (tq, tq), 1)
              <= lax.broadcasted_iota(jnp.int32, (tq, tq), 0))
    for qi in reversed(range(t // tq)):
        rows = slice(qi * tq, (qi + 1) * tq)
        base = (qi * (qi + 1)) // 2
        q = q_ref[rows, :]
        fq = jnp.sum(jnp.where(lane == h, fq_ref[rows, :], 0.0), axis=-1, keepdims=True)
        mx = None
        for j in range(qi + 1):
            ks = slice(j * tq, (j + 1) * tq)
            s = lax.dot_general(q, kb_ref[ks, :], (((1,), (1,)), ((), ())), preferred_element_type=F32)
            s = s / math.sqrt(HD_C) + fq - fk_ref[:, ks]
            if j == qi:
                s = jnp.where(causal, s, NEG)
            s_ref[base + j] = s
            fold = _lane_fold(s, jnp.maximum)
            mx = fold if mx is None else jnp.maximum(mx, fold)
        m = jnp.max(mx, axis=-1, keepdims=True)
        den = acc = None
        for j in range(qi + 1):
            p = jnp.exp(s_ref[base + j] - m)
            fold = _lane_fold(p, jnp.add)
            pv = jnp.dot(p.astype(BF16), vb_ref[j * tq:(j + 1) * tq, :], preferred_element_type=F32)
            den = fold if den is None else den + fold
            acc = pv if acc is None else acc + pv
        o_ref[rows, :] = (acc / jnp.sum(den, axis=-1, keepdims=True)).astype(o_ref.dtype)


def _fox_prompt(q, k, v, f_rows, f_heads):
    b, t, _ = q.shape
    tq = _tile(t, 512)
    nq = t // tq
    head_rows = pl.BlockSpec((None, t, HD_C), lambda bi, h: (bi, 0, h))
    return pl.pallas_call(
        functools.partial(_fox_kernel, tq=tq, t=t),
        name="fox_prompt",
        grid=(b, H_C),
        in_specs=[head_rows, head_rows, head_rows,
                  pl.BlockSpec((None, t, LANES), lambda bi, h: (bi, 0, 0)),
                  pl.BlockSpec((None, None, 1, t), lambda bi, h: (bi, h, 0, 0))],
        out_specs=head_rows,
        out_shape=jax.ShapeDtypeStruct((b, t, H_C * HD_C), BF16),
        scratch_shapes=[pltpu.VMEM(((nq * (nq + 1)) // 2, tq, tq), F32),
                        pltpu.VMEM((t, HD_C), BF16), pltpu.VMEM((t, HD_C), BF16)],
        compiler_params=_params("arbitrary", "arbitrary"),
    )(q, k, v, f_rows, f_heads.reshape(b, H_C, 1, t))


def _fox_sample_kernel(q_ref, kc_ref, vc_ref, kn_ref, vn_ref, fct_ref, fnc_ref, fec_ref, fnt_ref,
                       o_ref, s_ref, p_ref, m_ref, l_ref, acc_ref, *, t, tk):
    j = pl.program_id(1)
    contract_last = (((1,), (1,)), ((), ()))

    @pl.when(j == 0)
    def _():
        m_ref[...] = jnp.full_like(m_ref, NEG)
        l_ref[...] = jnp.zeros_like(l_ref)
        acc_ref[...] = jnp.zeros_like(acc_ref)

    def head_rows(h):
        return slice(h * t, (h + 1) * t)

    def update(width):
        s = s_ref[:, 0:width]
        m_prev = m_ref[...]
        m_col = jnp.maximum(m_prev[:, 0:1], jnp.max(s, axis=-1, keepdims=True))
        m_new = jnp.broadcast_to(m_col, m_prev.shape)
        alpha = jnp.exp(m_prev - m_new)
        p = jnp.exp(s - m_col)
        l_ref[...] = alpha * l_ref[...] + jnp.sum(p, axis=-1, keepdims=True)
        acc_ref[...] = alpha * acc_ref[...]
        m_ref[...] = m_new
        p_ref[:, 0:width] = p.astype(BF16)

    fq_cache = fnc_ref[...] + fec_ref[...]
    for h in range(H_C):
        hs = slice(h * HD_C, (h + 1) * HD_C)
        kh = kc_ref[pl.ds(h, tk, stride=H_C), :].astype(BF16)
        s = lax.dot_general(q_ref[:, hs], kh, contract_last, preferred_element_type=F32)
        s_ref[head_rows(h), :] = s / math.sqrt(HD_C) + fq_cache[head_rows(h)] - fct_ref[h:h + 1, :]
    update(tk)
    for h in range(H_C):
        vh = vc_ref[pl.ds(h, tk, stride=H_C), :].astype(BF16)
        acc_ref[head_rows(h), :] += jnp.dot(p_ref[head_rows(h), :], vh, preferred_element_type=F32)

    @pl.when(j == pl.num_programs(1) - 1)
    def _():
        causal = (lax.broadcasted_iota(jnp.int32, (t, t), 1)
                  <= lax.broadcasted_iota(jnp.int32, (t, t), 0))
        fq_new = fnc_ref[...]
        for h in range(H_C):
            hs = slice(h * HD_C, (h + 1) * HD_C)
            s = lax.dot_general(q_ref[:, hs], kn_ref[:, hs].astype(BF16), contract_last,
                                preferred_element_type=F32)
            s = s / math.sqrt(HD_C) + fq_new[head_rows(h)] - fnt_ref[h:h + 1, :]
            s_ref[head_rows(h), 0:t] = jnp.where(causal, s, NEG)
        update(t)
        for h in range(H_C):
            hs = slice(h * HD_C, (h + 1) * HD_C)
            pv = jnp.dot(p_ref[head_rows(h), 0:t], vn_ref[:, hs].astype(BF16), preferred_element_type=F32)
            o_ref[:, hs] = ((acc_ref[head_rows(h), :] + pv) / l_ref[head_rows(h), :]).astype(o_ref.dtype)


def _fox_sample(q, k_new, v_new, k_cache, v_cache, layer, fc_heads, fn_heads):
    b, t, d = q.shape
    n_layers, _, p = k_cache.shape[:3]
    tk = _tile(p, 512)
    rows = H_C * t
    fn_col = fn_heads.reshape(b, rows, 1)
    fc_end_col = jnp.broadcast_to(fc_heads[:, :, p - 1:], (b, H_C, t)).reshape(b, rows, 1)
    cache_spec = pl.BlockSpec((None, None, tk * H_C, HD_C), lambda bi, j: (layer, bi, j, 0))
    new_spec = pl.BlockSpec((None, t, d), lambda bi, j: (bi, 0, 0))
    col_spec = pl.BlockSpec((None, rows, 1), lambda bi, j: (bi, 0, 0))
    return pl.pallas_call(
        functools.partial(_fox_sample_kernel, t=t, tk=tk),
        name="fox_sample",
        grid=(b, p // tk),
        in_specs=[new_spec, cache_spec, cache_spec, new_spec, new_spec,
                  pl.BlockSpec((None, H_C, tk), lambda bi, j: (bi, 0, j)),
                  col_spec, col_spec,
                  pl.BlockSpec((None, H_C, t), lambda bi, j: (bi, 0, 0))],
        out_specs=new_spec,
        out_shape=jax.ShapeDtypeStruct((b, t, d), BF16),
        scratch_shapes=[pltpu.VMEM((rows, tk), F32), pltpu.VMEM((rows, tk), BF16),
                        pltpu.VMEM((rows, LANES), F32), pltpu.VMEM((rows, LANES), F32),
                        pltpu.VMEM((rows, HD_C), F32)],
        compiler_params=_params("arbitrary", "arbitrary"),
    )(q, k_cache.reshape(n_layers, b, p * H_C, HD_C), v_cache.reshape(n_layers, b, p * H_C, HD_C),
      k_new, v_new, fc_heads, fn_col, fc_end_col, fn_heads)


def kernel(x_prompt, x_sample, c_prompt, c_sample, cache_conv, cache_win_k, cache_win_v, cache_fox_k, cache_fox_v, cache_fox_logf, ada_w, ada_b, norm_mix_g, norm_ffn_g, e_w_in, e_w_dw, e_b_dw, e_ln_g, e_ln_b, e_sinks, e_w_out, o_w_in, o_b_f, o_w_out, ffn_w_gate, ffn_w_up, ffn_w_down, final_g):
    bp, tp, d = x_prompt.shape
    bs, ts, _ = x_sample.shape
    depth = ada_w.shape[0]
    n_odd = o_w_in.shape[0]
    past_len = cache_fox_k.shape[2]
    nqkv = H_C * HD_C

    mod = _adaln(jnp.concatenate([c_prompt, c_sample], axis=0), ada_w, ada_b)
    tab_p = _rope_tables(jnp.arange(tp))
    tab_s = tuple(jnp.tile(x, (bs, 1)) for x in _rope_tables(past_len + jnp.arange(ts)))

    xp = x_prompt.reshape(bp * tp, d)
    xs = x_sample.reshape(bs * ts, d)
    hp = _modulate(x_prompt, norm_mix_g[0], mod, 0, 0, 0).reshape(bp * tp, d)
    hs = _modulate(x_sample, norm_mix_g[0], mod, 0, 0, bp).reshape(bs * ts, d)
    outs = {name: [] for name in ("conv_p", "wk_p", "wv_p", "fl_p",
                                  "conv_s", "wk_s", "wv_s", "fk_s", "fv_s", "fl_s")}
    fk_p = fv_p = None
    for l in range(depth):
        if l % 2 == 0:
            e = l // 2
            w_in = e_w_in[e].astype(BF16)
            w_out = e_w_out[e].astype(BF16)
            conv = (e_w_dw[e], e_b_dw[e], e_ln_g[e], e_ln_b[e])
            zp = _even_proj(hp, w_in, tab_p, tp).reshape(bp, tp, E_IN)
            cat, cst = _even_mix(zp, e_sinks[e], *conv)
            win = min(WINDOW, tp)
            outs["conv_p"].append(cst[:, HIST_PAD - (CONV_W - 1):])
            outs["wk_p"].append(zp[:, tp - win:, K_OFF:V_OFF].reshape(bp, win, KV_B, HD_B))
            outs["wv_p"].append(zp[:, tp - win:, V_OFF:].reshape(bp, win, KV_B, HD_B))
            yp = cat.reshape(bp * tp, d)

            zs = _even_proj(hs, w_in, tab_s, bs * ts).reshape(bs, ts, E_IN)
            hist = jnp.pad(cache_conv[e], ((0, 0), (HIST_PAD - (CONV_W - 1), 0), (0, 0)))
            cat, cst = _even_mix_sample(zs, hist, cache_win_k[e], cache_win_v[e], e_sinks[e], *conv)
            outs["conv_s"].append(cst[:, HIST_PAD - (CONV_W - 1):])
            outs["wk_s"].append(zs[:, :, K_OFF:V_OFF].reshape(bs, ts, KV_B, HD_B))
            outs["wv_s"].append(zs[:, :, V_OFF:].reshape(bs, ts, KV_B, HD_B))
            ys = cat.reshape(bs * ts, d)
        else:
            o = l // 2
            w_in = o_w_in[o]
            wq = w_in[:, :nqkv].astype(BF16)
            wk = w_in[:, nqkv:2 * nqkv].astype(BF16)
            wv = w_in[:, 2 * nqkv:3 * nqkv].astype(BF16)
            wf = jnp.pad(w_in[:, 3 * nqkv:], ((0, 0), (0, LANES - H_C))).astype(BF16)
            bf = jnp.pad(o_b_f[o], (0, LANES - H_C)).reshape(1, LANES)
            w_out = o_w_out[o].astype(BF16)

            q = _matmul(hp, wq, BF16).reshape(bp, tp, nqkv)
            k, fk_p = _kv_proj(hp, wk, fk_p, o, n_odd)
            v, fv_p = _kv_proj(hp, wv, fv_p, o, n_odd)
            lf, f_rows, f_heads = _forget_scan(_matmul(hp, wf, F32).reshape(bp, tp, LANES), bf, True)
            outs["fl_p"].append(lf)
            yp = _fox_prompt(q, k.reshape(bp, tp, nqkv), v.reshape(bp, tp, nqkv),
                             f_rows, f_heads).reshape(bp * tp, d)

            q = _matmul(hs, wq, BF16).reshape(bs, ts, nqkv)
            k = _matmul(hs, wk, F32).reshape(bs, ts, nqkv)
            v = _matmul(hs, wv, F32).reshape(bs, ts, nqkv)
            lf, _, fn_heads = _forget_scan(_matmul(hs, wf, F32).reshape(bs, ts, LANES), bf, True)
            cache_lf = jnp.pad(cache_fox_logf[o], ((0, 0), (0, 0), (0, LANES - H_C)))
            _, _, fc_heads = _forget_scan(cache_lf, bf, False)
            outs["fk_s"].append(k.reshape(bs, ts, H_C, HD_C))
            outs["fv_s"].append(v.reshape(bs, ts, H_C, HD_C))
            outs["fl_s"].append(lf)
            ys = _fox_sample(q, k, v, cache_fox_k, cache_fox_v, o, fc_heads,
                             fn_heads).reshape(bs * ts, d)

        xp, hp = _residual_matmul(yp, w_out, xp, norm_ffn_g[l], mod, l, tp, 0)
        xs, hs = _residual_matmul(ys, w_out, xs, norm_ffn_g[l], mod, l, ts, bp)

        wg = ffn_w_gate[l].astype(BF16)
        wu = ffn_w_up[l].astype(BF16)
        wd = ffn_w_down[l].astype(BF16)
        if l + 1 < depth:
            xp, hp = _ffn(hp, xp, wg, wu, wd, norm_mix_g[l + 1], mod, l, tp, 0, False)
            xs, hs = _ffn(hs, xs, wg, wu, wd, norm_mix_g[l + 1], mod, l, ts, bp, False)
        else:
            y_prompt = _ffn(hp, xp, wg, wu, wd, final_g, mod, l, tp, 0, True).reshape(bp, tp, d)
            y_sample = _ffn(hs, xs, wg, wu, wd, final_g, mod, l, ts, bp, True).reshape(bs, ts, d)

    stack = lambda name: jnp.stack(outs[name])
    return (y_prompt, y_sample,
            stack("conv_p"), stack("wk_p"), stack("wv_p"),
            fk_p.reshape(n_odd, bp, tp, H_C, HD_C), fv_p.reshape(n_odd, bp, tp, H_C, HD_C), stack("fl_p"),
            stack("conv_s"), stack("wk_s"), stack("wv_s"), stack("fk_s"), stack("fv_s"), stack("fl_s"))
```

```python
import functools
import math

import jax
import jax.numpy as jnp
from jax import lax
from jax.experimental import pallas as pl
from jax.experimental.pallas import tpu as pltpu

F32 = jnp.float32
BF16 = jnp.bfloat16

D_MODEL = 2048
CHUNK = 64
C_CONV = D_MODEL // 2
CONV_W = 31
HIST_PAD = 32
HD_B = 64
H_B = (D_MODEL // 2) // HD_B
KV_B = H_B // 4
G_B = H_B // KV_B
WINDOW = 128
E_IN = 2 * C_CONV + H_B * HD_B + 2 * KV_B * HD_B
Q_OFF = 2 * C_CONV
K_OFF = Q_OFF + H_B * HD_B
V_OFF = K_OFF + KV_B * HD_B
HD_C = 128
H_C = D_MODEL // HD_C
ROPE_THETA = 10000.0
EPS = 1e-6
NEG = -1e30
LANES = 128
SUBLANES = 8
VMEM_LIMIT = 56 * 1024 * 1024


def _params(*sem):
    return pltpu.CompilerParams(dimension_semantics=sem, vmem_limit_bytes=VMEM_LIMIT)


def _tile(n, pref):
    t = min(n, pref)
    while n % t:
        t //= 2
    return t


def _bcast_rows(vec, rows):
    nb, _, w = vec.shape
    if nb == 1:
        return vec[0]
    return jnp.broadcast_to(vec, (nb, rows // nb, w)).reshape(rows, w)


def _mod_spec(layer, chunk, nb, width, index_fn):
    return pl.BlockSpec((None, None, nb, 1, width),
                        lambda *g: (layer, chunk) + tuple(index_fn(*g)))


def _row_tiling(m, rows_per_batch, pref):
    if m > rows_per_batch and m > pref:
        tm = _tile(rows_per_batch, pref)
        return tm, 1, lambda i, row_off: (i * tm) // rows_per_batch + row_off
    nb = max(m // rows_per_batch, 1)
    return m, nb, lambda i, row_off: row_off // nb + i


def _rms_scale(x):
    return lax.rsqrt(jnp.mean(x * x, axis=-1, keepdims=True) + EPS)


def _adaln_kernel(c_ref, w_ref, b_ref, o_ref):
    c = c_ref[...]
    a = (c * jax.nn.sigmoid(c)).astype(BF16)
    o_ref[...] = jnp.dot(a, w_ref[...].astype(BF16), preferred_element_type=F32) + b_ref[...]


def _adaln(c_all, ada_w, ada_b):
    n_layers, d, n = ada_w.shape
    nb = c_all.shape[0]
    tn = 1024
    per = d // tn
    out = pl.pallas_call(
        _adaln_kernel,
        name="adaln",
        grid=(n_layers, n // tn),
        in_specs=[pl.BlockSpec((nb, d), lambda l, j: (0, 0)),
                  pl.BlockSpec((None, d, tn), lambda l, j: (l, 0, j)),
                  pl.BlockSpec((None, 1, tn), lambda l, j: (l, 0, j))],
        out_specs=pl.BlockSpec((None, None, nb, tn), lambda l, j: (l, j // per, 0, j % per)),
        out_shape=jax.ShapeDtypeStruct((n_layers, n // d, nb, d), F32),
        compiler_params=_params("arbitrary", "arbitrary"),
    )(c_all, ada_w, ada_b.reshape(n_layers, 1, n))
    return out.reshape(n_layers, n // d, nb, 1, d)


def _modulate_kernel(x_ref, g_ref, sh_ref, sc_ref, o_ref):
    x = x_ref[...]
    y = x * _rms_scale(x) * (g_ref[...] * (1.0 + sc_ref[0]))
    o_ref[...] = (y + sh_ref[0]).astype(o_ref.dtype)


def _modulate(x, g, mod, layer, chunk, row_off):
    b, t, d = x.shape
    tt = _tile(t, 512)
    return pl.pallas_call(
        _modulate_kernel,
        name="modulate",
        grid=(b, t // tt),
        in_specs=[pl.BlockSpec((None, tt, d), lambda bi, i: (bi, i, 0)),
                  pl.BlockSpec((1, d), lambda bi, i: (0, 0)),
                  _mod_spec(layer, chunk, 1, d, lambda bi, i: (bi + row_off, 0, 0)),
                  _mod_spec(layer, chunk + 1, 1, d, lambda bi, i: (bi + row_off, 0, 0))],
        out_specs=pl.BlockSpec((None, tt, d), lambda bi, i: (bi, i, 0)),
        out_shape=jax.ShapeDtypeStruct((b, t, d), BF16),
        compiler_params=_params("arbitrary", "arbitrary"),
    )(x, g.reshape(1, d), mod, mod)


def _mm_kernel(a_ref, w_ref, o_ref):
    o_ref[...] = jnp.dot(a_ref[...], w_ref[...], preferred_element_type=F32).astype(o_ref.dtype)


def _matmul(a, w, layer, col0, n, out_dtype):
    m, k = a.shape
    tm, tn = _tile(m, 1024), _tile(n, 1024)
    assert col0 % tn == 0
    return pl.pallas_call(
        _mm_kernel,
        name="matmul",
        grid=(m // tm, n // tn),
        in_specs=[pl.BlockSpec((tm, k), lambda i, j: (i, 0)),
                  pl.BlockSpec((None, k, tn), lambda i, j: (layer, 0, col0 // tn + j))],
        out_specs=pl.BlockSpec((tm, tn), lambda i, j: (i, j)),
        out_shape=jax.ShapeDtypeStruct((m, n), out_dtype),
        compiler_params=_params("arbitrary", "arbitrary"),
    )(a, w)


def _kv_proj_kernel(a_ref, w_ref, *refs, tm, tn):
    rows_ref, heads_ref = refs[-2:]
    a = a_ref[...]
    per = tn // HD_C
    for j in range(w_ref.shape[1] // tn):
        y = jnp.dot(a, w_ref[:, j * tn:(j + 1) * tn], preferred_element_type=F32)
        rows_ref[:, j * tn:(j + 1) * tn] = y
        for hh in range(per):
            heads_ref[pl.ds(j * per + hh, tm, stride=H_C), :] = y[:, hh * HD_C:(hh + 1) * HD_C]


def _kv_proj(a, w, w_layer, col0, stacked, layer, n_layers):
    m, k = a.shape
    n = H_C * HD_C
    assert col0 % n == 0
    tm = _tile(m, 512)
    in_specs = [pl.BlockSpec((tm, k), lambda i: (i, 0)),
                pl.BlockSpec((None, k, n), lambda i: (w_layer, 0, col0 // n))]
    args = [a, w]
    aliases = {}
    if stacked is not None:
        in_specs.append(pl.BlockSpec(memory_space=pl.ANY))
        args.append(stacked)
        aliases = {2: 1}
    return pl.pallas_call(
        functools.partial(_kv_proj_kernel, tm=tm, tn=512),
        name="kv_proj",
        grid=(m // tm,),
        in_specs=in_specs,
        out_specs=[pl.BlockSpec((tm, n), lambda i: (i, 0)),
                   pl.BlockSpec((None, tm * H_C, HD_C), lambda i: (layer, i, 0))],
        out_shape=[jax.ShapeDtypeStruct((m, n), F32),
                   jax.ShapeDtypeStruct((n_layers, m * H_C, HD_C), F32)],
        input_output_aliases=aliases,
        compiler_params=_params("arbitrary"),
    )(*args)


def _rope_tables(pos):
    half = HD_B // 2
    freqs = ROPE_THETA ** (-jnp.arange(half, dtype=F32) / half)
    ang = pos.astype(F32)[:, None] * freqs[None, :]
    cos, sin = jnp.cos(ang), jnp.sin(ang)
    zero = jnp.zeros_like(sin)
    rep = LANES // HD_B
    cos_t = jnp.tile(jnp.concatenate([cos, cos], axis=1), (1, rep))
    sa = jnp.tile(jnp.concatenate([-sin, zero], axis=1), (1, rep))
    sb = jnp.tile(jnp.concatenate([zero, sin], axis=1), (1, rep))
    return cos_t, sa, sb


def _rope(x, cos, sa, sb):
    half = HD_B // 2
    outs = []
    for g in range(x.shape[1] // LANES):
        xg = x[:, g * LANES:(g + 1) * LANES]
        outs.append(xg * cos + pltpu.roll(xg, LANES - half, 1) * sa + pltpu.roll(xg, half, 1) * sb)
    return outs[0] if len(outs) == 1 else jnp.concatenate(outs, axis=1)


def _even_proj_kernel(a_ref, w_ref, cos_ref, sa_ref, sb_ref, o_ref, *, tn):
    j = pl.program_id(1)
    z = jnp.dot(a_ref[...], w_ref[...], preferred_element_type=F32)
    q_lo, k_tile = Q_OFF // tn, K_OFF // tn
    kw = KV_B * HD_B

    @pl.when(j < q_lo)
    def _():
        o_ref[...] = z

    @pl.when(jnp.logical_and(j >= q_lo, j < k_tile))
    def _():
        o_ref[...] = _rope(z, cos_ref[...], sa_ref[...], sb_ref[...])

    @pl.when(j == k_tile)
    def _():
        o_ref[:, :kw] = _rope(z[:, :kw], cos_ref[...], sa_ref[...], sb_ref[...])
        o_ref[:, kw:] = z[:, kw:]


def _even_proj(a, w, layer, tables, period):
    m, k = a.shape
    n = w.shape[2]
    tn = 2 * KV_B * HD_B
    tm = _tile(period, 1024) if m > period else m
    per_blocks = max(period // tm, 1)
    tab_spec = pl.BlockSpec((tm, LANES), lambda i, j: (i % per_blocks, 0))
    return pl.pallas_call(
        functools.partial(_even_proj_kernel, tn=tn),
        name="even_proj",
        grid=(m // tm, n // tn),
        in_specs=[pl.BlockSpec((tm, k), lambda i, j: (i, 0)),
                  pl.BlockSpec((None, k, tn), lambda i, j: (layer, 0, j)),
                  tab_spec, tab_spec, tab_spec],
        out_specs=pl.BlockSpec((tm, tn), lambda i, j: (i, j)),
        out_shape=jax.ShapeDtypeStruct((m, n), F32),
        compiler_params=_params("arbitrary", "arbitrary"),
    )(a, w, *tables)


def _residual_kernel(a_ref, w_ref, x_ref, gate_ref, g_ref, sh_ref, sc_ref, xo_ref, ho_ref, *, tn):
    tm, d = xo_ref.shape
    a = a_ref[...]
    gate = _bcast_rows(gate_ref[...], tm)
    ss = jnp.zeros((tm, 1), F32)
    for j in range(d // tn):
        cs = slice(j * tn, (j + 1) * tn)
        xn = x_ref[:, cs] + gate[:, cs] * jnp.dot(a, w_ref[:, cs], preferred_element_type=F32)
        xo_ref[:, cs] = xn
        ss = ss + jnp.sum(xn * xn, axis=-1, keepdims=True)
    gs = _bcast_rows(g_ref[...] * (1.0 + sc_ref[...]), tm)
    h = xo_ref[...] * lax.rsqrt(ss / d + EPS) * gs + _bcast_rows(sh_ref[...], tm)
    ho_ref[...] = h.astype(ho_ref.dtype)


def _residual_matmul(a, w, w_layer, x, g, mod, layer, rows_per_batch, row_off):
    m, k = a.shape
    d = w.shape[2]
    tm, nb, row_fn = _row_tiling(m, rows_per_batch, 512)
    vec = lambda chunk: _mod_spec(layer, chunk, nb, d, lambda i: (row_fn(i, row_off), 0, 0))
    return pl.pallas_call(
        functools.partial(_residual_kernel, tn=512),
        name="residual",
        grid=(m // tm,),
        in_specs=[pl.BlockSpec((tm, k), lambda i: (i, 0)),
                  pl.BlockSpec((None, k, d), lambda i: (w_layer, 0, 0)),
                  pl.BlockSpec((tm, d), lambda i: (i, 0)),
                  vec(2),
                  pl.BlockSpec((1, 1, d), lambda i: (0, 0, 0)),
                  vec(3), vec(4)],
        out_specs=[pl.BlockSpec((tm, d), lambda i: (i, 0)),
                   pl.BlockSpec((tm, d), lambda i: (i, 0))],
        out_shape=[jax.ShapeDtypeStruct((m, d), F32), jax.ShapeDtypeStruct((m, d), BF16)],
        compiler_params=_params("arbitrary"),
    )(a, w, x, mod, g.reshape(1, 1, d), mod, mod)


def _ffn_kernel(h_ref, x_ref, gate_ref, wg_ref, wu_ref, wd_ref, g_ref, *refs, final, rows):
    if final:
        y_ref, acc_ref = refs
    else:
        sh_ref, sc_ref, xo_ref, ho_ref, acc_ref = refs
    f = pl.program_id(1)
    tm, d = acc_ref.shape

    @pl.when(f == 0)
    def _():
        acc_ref[...] = jnp.zeros_like(acc_ref)

    h = h_ref[...]
    g = jnp.dot(h, wg_ref[...], preferred_element_type=F32)
    u = jnp.dot(h, wu_ref[...], preferred_element_type=F32)
    a = (g * jax.nn.sigmoid(g) * u).astype(BF16)
    acc_ref[...] += jnp.dot(a, wd_ref[...], preferred_element_type=F32)

    @pl.when(f == pl.num_programs(1) - 1)
    def _():
        nb = gate_ref.shape[0]
        per = tm // nb
        for r0 in range(0, tm, rows):
            rs = slice(r0, r0 + rows)
            b = r0 // per
            xn = x_ref[rs, :] + gate_ref[b] * acc_ref[rs, :]
            if final:
                y_ref[rs, :] = xn * _rms_scale(xn) * g_ref[0]
            else:
                xo_ref[rs, :] = xn
                hn = xn * _rms_scale(xn) * (g_ref[0] * (1.0 + sc_ref[b])) + sh_ref[b]
                ho_ref[rs, :] = hn.astype(ho_ref.dtype)


def _ffn(h, x, wg, wu, wd, g_next, mod, layer, rows_per_batch, row_off, final):
    m, d = h.shape
    ff = wg.shape[2]
    tf = _tile(ff, 512)
    tm, nb, row_fn = _row_tiling(m, rows_per_batch, 512)
    vec = lambda lyr, chunk: _mod_spec(lyr, chunk, nb, d, lambda i, f: (row_fn(i, row_off), 0, 0))
    rows_spec = pl.BlockSpec((tm, d), lambda i, f: (i, 0))
    in_specs = [rows_spec, rows_spec, vec(layer, 5),
                pl.BlockSpec((None, d, tf), lambda i, f: (layer, 0, f)),
                pl.BlockSpec((None, d, tf), lambda i, f: (layer, 0, f)),
                pl.BlockSpec((None, tf, d), lambda i, f: (layer, f, 0)),
                pl.BlockSpec((1, 1, d), lambda i, f: (0, 0, 0))]
    args = [h, x, mod, wg, wu, wd, g_next.reshape(1, 1, d)]
    if final:
        out_specs = rows_spec
        out_shape = jax.ShapeDtypeStruct((m, d), F32)
    else:
        in_specs += [vec(layer + 1, 0), vec(layer + 1, 1)]
        args += [mod, mod]
        out_specs = [rows_spec, rows_spec]
        out_shape = [jax.ShapeDtypeStruct((m, d), F32), jax.ShapeDtypeStruct((m, d), BF16)]
    return pl.pallas_call(
        functools.partial(_ffn_kernel, final=final, rows=2 * SUBLANES),
        name="ffn",
        grid=(m // tm, ff // tf),
        in_specs=in_specs,
        out_specs=out_specs,
        out_shape=out_shape,
        scratch_shapes=[pltpu.VMEM((tm, d), F32)],
        compiler_params=_params("arbitrary", "arbitrary"),
    )(*args)


def _glu_rows(z_ref, uext_ref, rows):
    step = min(rows, 4 * SUBLANES)
    for r in range(0, rows, step):
        uext_ref[HIST_PAD + r:HIST_PAD + r + step, :] = (
            z_ref[r:r + step, 0:C_CONV] * jax.nn.sigmoid(z_ref[r:r + step, C_CONV:2 * C_CONV]))


def _conv_ln_silu(uext_ref, wdw_ref, bdw_ref, lng_ref, lnb_ref, y_ref, shift_ref, out_ref, rows):
    base = HIST_PAD - (CONV_W - 1)
    for c in range(C_CONV // LANES):
        cs = slice(c * LANES, (c + 1) * LANES)
        for r in range(SUBLANES):
            n = rows + SUBLANES * ((CONV_W - 1 - r) // SUBLANES)
            shift_ref[r, 0:n, :] = uext_ref[base + r:base + r + n, cs]
        acc = jnp.zeros((rows, LANES), F32)
        for j in range(CONV_W):
            r, a = j % SUBLANES, j // SUBLANES
            acc = acc + wdw_ref[j:j + 1, cs] * shift_ref[r, SUBLANES * a:SUBLANES * a + rows, :]
        y_ref[:, cs] = acc + bdw_ref[:, cs]
    step = min(rows, 4 * SUBLANES)
    for r in range(0, rows, step):
        y = y_ref[r:r + step, :]
        yc = y - jnp.mean(y, axis=-1, keepdims=True)
        var = jnp.mean(yc * yc, axis=-1, keepdims=True)
        yn = yc * lax.rsqrt(var + EPS) * lng_ref[...] + lnb_ref[...]
        out_ref[r:r + step, 0:C_CONV] = (yn * jax.nn.sigmoid(yn)).astype(out_ref.dtype)


def _sink_attention(q, k, v, sink_col, mask):
    s = lax.dot_general(q, k, (((1,), (1,)), ((), ())), preferred_element_type=F32) / math.sqrt(HD_B)
    if mask is not None:
        s = jnp.where(mask, s, NEG)
    m = jnp.maximum(jnp.max(s, axis=-1, keepdims=True), sink_col)
    p = jnp.exp(s - m)
    den = jnp.sum(p, axis=-1, keepdims=True) + jnp.exp(sink_col - m)
    return jnp.dot(p.astype(BF16), v, preferred_element_type=F32) / den


def _sink_column(sink_ref, kv_head, rows):
    return jnp.concatenate(
        [jnp.full((rows, 1), sink_ref[kv_head * G_B + g], F32) for g in range(G_B)], axis=0)


def _group_attention(sink_ref, z_ref, kall, vall, r0, rows, mask, cat_ref):
    for kh in range(KV_B):
        hs = slice(kh * HD_B, (kh + 1) * HD_B)
        q = jnp.concatenate(
            [z_ref[r0:r0 + rows, Q_OFF + (kh * G_B + g) * HD_B:Q_OFF + (kh * G_B + g + 1) * HD_B]
             for g in range(G_B)], axis=0).astype(BF16)
        o = _sink_attention(q, kall[:, hs], vall[:, hs], _sink_column(sink_ref, kh, rows), mask)
        for gp in range(G_B // 2):
            pair = jnp.concatenate([o[(2 * gp) * rows:(2 * gp + 1) * rows],
                                    o[(2 * gp + 1) * rows:(2 * gp + 2) * rows]], axis=1)
            c0 = C_CONV + (kh * G_B + 2 * gp) * HD_B
            cat_ref[r0:r0 + rows, c0:c0 + 2 * HD_B] = pair.astype(cat_ref.dtype)


def _even_mix_kernel(sink_ref, zc_ref, zh_ref, zkv_ref, wdw_ref, bdw_ref, lng_ref, lnb_ref,
                     cat_ref, cst_ref, uext_ref, y_ref, shift_ref, *, tt):
    i = pl.program_id(1)
    first = i == 0

    _glu_rows(zc_ref, uext_ref, tt)
    hist = zh_ref[:, 0:C_CONV] * jax.nn.sigmoid(zh_ref[:, C_CONV:2 * C_CONV])
    uext_ref[0:HIST_PAD, :] = jnp.where(first, 0.0, hist)
    _conv_ln_silu(uext_ref, wdw_ref, bdw_ref, lng_ref, lnb_ref, y_ref, shift_ref, cat_ref, tt)

    @pl.when(i == pl.num_programs(1) - 1)
    def _():
        cst_ref[...] = uext_ref[tt:tt + HIST_PAD, :]

    kw = KV_B * HD_B
    sub = 2 * CHUNK
    kext = jnp.concatenate([zkv_ref[:, 0:kw], zc_ref[:, K_OFF:K_OFF + kw]], axis=0).astype(BF16)
    v_t = jnp.concatenate([zkv_ref[:, kw:2 * kw], zc_ref[:, V_OFF:V_OFF + kw]], axis=0).T.astype(BF16)
    cols = G_B * sub
    kc = lax.broadcasted_iota(jnp.int32, (2 * sub, cols), 0) // CHUNK
    qc = (lax.broadcasted_iota(jnp.int32, (2 * sub, cols), 1) % sub) // CHUNK
    band = jnp.logical_and(kc >= qc, kc <= qc + WINDOW // CHUNK)
    band_first = jnp.logical_and(band, jnp.logical_or(jnp.logical_not(first), kc >= WINDOW // CHUNK))
    for s in range(tt // sub):
        r0 = s * sub
        mask = band_first if s == 0 else band
        for kh in range(KV_B):
            hs = slice(kh * HD_B, (kh + 1) * HD_B)
            q = jnp.concatenate(
                [zc_ref[r0:r0 + sub, Q_OFF + (kh * G_B + g) * HD_B:Q_OFF + (kh * G_B + g + 1) * HD_B]
                 for g in range(G_B)], axis=0).astype(BF16)
            logit = lax.dot_general(kext[r0:r0 + 2 * sub, hs], q, (((1,), (1,)), ((), ())),
                                    preferred_element_type=F32) / math.sqrt(HD_B)
            logit = jnp.where(mask, logit, NEG)
            sink = jnp.concatenate(
                [jnp.full((1, sub), sink_ref[kh * G_B + g], F32) for g in range(G_B)], axis=1)
            m = jnp.maximum(jnp.max(logit, axis=0, keepdims=True), sink)
            p = jnp.exp(logit - m)
            den = jnp.sum(p, axis=0, keepdims=True) + jnp.exp(sink - m)
            o_t = jnp.dot(v_t[hs, r0:r0 + 2 * sub], p.astype(BF16), preferred_element_type=F32) / den
            for gp in range(G_B // 2):
                pair_t = jnp.concatenate([o_t[:, (2 * gp) * sub:(2 * gp + 1) * sub],
                                          o_t[:, (2 * gp + 1) * sub:(2 * gp + 2) * sub]], axis=0)
                c0 = C_CONV + (kh * G_B + 2 * gp) * HD_B
                cat_ref[r0:r0 + sub, c0:c0 + 2 * HD_B] = pair_t.T.astype(cat_ref.dtype)


def _even_mix(z, sinks, w_dw, b_dw, ln_g, ln_b):
    b, t, _ = z.shape
    tt = _tile(t, 256)
    hb = tt // HIST_PAD
    kb = tt // WINDOW
    kvw = 2 * KV_B * HD_B
    vec = pl.BlockSpec((1, C_CONV), lambda bi, i: (0, 0))
    return pl.pallas_call(
        functools.partial(_even_mix_kernel, tt=tt),
        name="even_mix",
        grid=(b, t // tt),
        in_specs=[pl.BlockSpec(memory_space=pltpu.SMEM),
                  pl.BlockSpec((None, tt, E_IN), lambda bi, i: (bi, i, 0)),
                  pl.BlockSpec((None, HIST_PAD, 2 * C_CONV),
                               lambda bi, i: (bi, jnp.maximum(i * hb - 1, 0), 0)),
                  pl.BlockSpec((None, WINDOW, kvw),
                               lambda bi, i: (bi, jnp.maximum(i * kb - 1, 0), K_OFF // kvw)),
                  pl.BlockSpec((CONV_W, C_CONV), lambda bi, i: (0, 0)),
                  vec, vec, vec],
        out_specs=[pl.BlockSpec((None, tt, D_MODEL), lambda bi, i: (bi, i, 0)),
                   pl.BlockSpec((None, HIST_PAD, C_CONV), lambda bi, i: (bi, 0, 0))],
        out_shape=[jax.ShapeDtypeStruct((b, t, D_MODEL), BF16),
                   jax.ShapeDtypeStruct((b, HIST_PAD, C_CONV), F32)],
        scratch_shapes=[pltpu.VMEM((HIST_PAD + tt, C_CONV), F32),
                        pltpu.VMEM((tt, C_CONV), F32),
                        pltpu.VMEM((SUBLANES, tt + HIST_PAD, LANES), F32)],
        compiler_params=_params("arbitrary", "arbitrary"),
    )(sinks, z, z, z, w_dw, b_dw.reshape(1, -1), ln_g.reshape(1, -1), ln_b.reshape(1, -1))


def _even_mix_sample_kernel(sink_ref, z_ref, hist_ref, kc_ref, vc_ref, wdw_ref, bdw_ref, lng_ref,
                            lnb_ref, cat_ref, cst_ref, uext_ref, y_ref, shift_ref, *, t):
    uext_ref[0:HIST_PAD, :] = hist_ref[...]
    _glu_rows(z_ref, uext_ref, t)
    _conv_ln_silu(uext_ref, wdw_ref, bdw_ref, lng_ref, lnb_ref, y_ref, shift_ref, cat_ref, t)
    cst_ref[...] = uext_ref[t:t + HIST_PAD, :]

    kw = KV_B * HD_B
    kall = jnp.concatenate([kc_ref[...], z_ref[:, K_OFF:K_OFF + kw]], axis=0).astype(BF16)
    vall = jnp.concatenate([vc_ref[...], z_ref[:, V_OFF:V_OFF + kw]], axis=0).astype(BF16)
    _group_attention(sink_ref, z_ref, kall, vall, 0, t, None, cat_ref)


def _even_mix_sample(z, hist, k_cache, v_cache, sinks, w_dw, b_dw, ln_g, ln_b):
    b, t, _ = z.shape
    win = k_cache.shape[1]
    kw = KV_B * HD_B
    vec = pl.BlockSpec((1, C_CONV), lambda bi: (0, 0))
    return pl.pallas_call(
        functools.partial(_even_mix_sample_kernel, t=t),
        name="even_mix_sample",
        grid=(b,),
        in_specs=[pl.BlockSpec(memory_space=pltpu.SMEM),
                  pl.BlockSpec((None, t, E_IN), lambda bi: (bi, 0, 0)),
                  pl.BlockSpec((None, HIST_PAD, C_CONV), lambda bi: (bi, 0, 0)),
                  pl.BlockSpec((None, win, kw), lambda bi: (bi, 0, 0)),
                  pl.BlockSpec((None, win, kw), lambda bi: (bi, 0, 0)),
                  pl.BlockSpec((CONV_W, C_CONV), lambda bi: (0, 0)),
                  vec, vec, vec],
        out_specs=[pl.BlockSpec((None, t, D_MODEL), lambda bi: (bi, 0, 0)),
                   pl.BlockSpec((None, HIST_PAD, C_CONV), lambda bi: (bi, 0, 0))],
        out_shape=[jax.ShapeDtypeStruct((b, t, D_MODEL), BF16),
                   jax.ShapeDtypeStruct((b, HIST_PAD, C_CONV), F32)],
        scratch_shapes=[pltpu.VMEM((HIST_PAD + t, C_CONV), F32),
                        pltpu.VMEM((t, C_CONV), F32),
                        pltpu.VMEM((SUBLANES, t + HIST_PAD, LANES), F32)],
        compiler_params=_params("arbitrary"),
    )(sinks, z, hist, k_cache.reshape(b, win, kw), v_cache.reshape(b, win, kw),
      w_dw, b_dw.reshape(1, -1), ln_g.reshape(1, -1), ln_b.reshape(1, -1))


def _scan_kernel(x_ref, b_ref, lf_ref, f_ref, ft_ref, carry_ref, *, tt, log_sigmoid):
    i = pl.program_id(1)

    @pl.when(i == 0)
    def _():
        carry_ref[...] = jnp.zeros_like(carry_ref)

    x = x_ref[...]
    if log_sigmoid:
        x = x + b_ref[...]
        x = jnp.minimum(x, 0.0) - jnp.log1p(jnp.exp(-jnp.abs(x)))
    lf_ref[...] = x[:, :H_C]
    row = lax.broadcasted_iota(jnp.int32, x.shape, 0)
    step = 1
    while step < tt:
        x = x + jnp.where(row >= step, pltpu.roll(x, step, 0), 0.0)
        step *= 2
    f = x + carry_ref[...]
    carry_ref[...] = f[tt - 1:tt, :]
    f_ref[...] = f
    if tt < LANES:
        f = jnp.concatenate([f, jnp.zeros((LANES - tt, LANES), F32)], axis=0)
    ft_ref[...] = f.T[:H_C, :tt]


def _forget_scan(x, bias, log_sigmoid):
    b, t, _ = x.shape
    tt = _tile(t, 512)
    return pl.pallas_call(
        functools.partial(_scan_kernel, tt=tt, log_sigmoid=log_sigmoid),
        name="forget_scan",
        grid=(b, t // tt),
        in_specs=[pl.BlockSpec((None, tt, LANES), lambda bi, i: (bi, i, 0)),
                  pl.BlockSpec((1, LANES), lambda bi, i: (0, 0))],
        out_specs=[pl.BlockSpec((None, tt, H_C), lambda bi, i: (bi, i, 0)),
                   pl.BlockSpec((None, tt, LANES), lambda bi, i: (bi, i, 0)),
                   pl.BlockSpec((None, H_C, tt), lambda bi, i: (bi, 0, i))],
        out_shape=[jax.ShapeDtypeStruct((b, t, H_C), F32),
                   jax.ShapeDtypeStruct((b, t, LANES), F32),
                   jax.ShapeDtypeStruct((b, H_C, t), F32)],
        scratch_shapes=[pltpu.VMEM((1, LANES), F32)],
        compiler_params=_params("arbitrary", "arbitrary"),
    )(x, bias)


def _lane_fold(x, op):
    out = x[:, 0:LANES]
    for c in range(1, x.shape[1] // LANES):
        out = op(out, x[:, c * LANES:(c + 1) * LANES])
    return out


def _fox_kernel(q_ref, k_ref, v_ref, fq_ref, fk_ref, o_ref, s_ref, kb_ref, vb_ref, *, tq, t):
    h = pl.program_id(1)
    kb_ref[...] = k_ref[...].astype(BF16)
    vb_ref[...] = v_ref[...].astype(BF16)
    lane = lax.broadcasted_iota(jnp.int32, (tq, LANES), 1)
    causal = (lax.broadcasted_iota(jnp.int32, (tq, tq), 1)
              <= lax.broadcasted_iota(jnp.int32, (tq, tq), 0))
    for qi in reversed(range(t // tq)):
        rows = slice(qi * tq, (qi + 1) * tq)
        base = (qi * (qi + 1)) // 2
        q = q_ref[rows, :]
        fq = jnp.sum(jnp.where(lane == h, fq_ref[rows, :], 0.0), axis=-1, keepdims=True)
        mx = None
        for j in range(qi + 1):
            ks = slice(j * tq, (j + 1) * tq)
            s = lax.dot_general(q, kb_ref[ks, :], (((1,), (1,)), ((), ())), preferred_element_type=F32)
            s = s / math.sqrt(HD_C) + fq - fk_ref[:, ks]
            if j == qi:
                s = jnp.where(causal, s, NEG)
            s_ref[base + j] = s
            fold = _lane_fold(s, jnp.maximum)
            mx = fold if mx is None else jnp.maximum(mx, fold)
        m = jnp.max(mx, axis=-1, keepdims=True)
        den = acc = None
        for j in range(qi + 1):
            p = jnp.exp(s_ref[base + j] - m)
            fold = _lane_fold(p, jnp.add)
            pv = jnp.dot(p.astype(BF16), vb_ref[j * tq:(j + 1) * tq, :], preferred_element_type=F32)
            den = fold if den is None else den + fold
            acc = pv if acc is None else acc + pv
        o_ref[rows, :] = (acc / jnp.sum(den, axis=-1, keepdims=True)).astype(o_ref.dtype)


def _fox_prompt(q, k, v, f_rows, f_heads):
    b, t, _ = q.shape
    tq = _tile(t, 512)
    nq = t // tq
    head_rows = pl.BlockSpec((None, t, HD_C), lambda bi, h: (bi, 0, h))
    return pl.pallas_call(
        functools.partial(_fox_kernel, tq=tq, t=t),
        name="fox_prompt",
        grid=(b, H_C),
        in_specs=[head_rows, head_rows, head_rows,
                  pl.BlockSpec((None, t, LANES), lambda bi, h: (bi, 0, 0)),
                  pl.BlockSpec((None, None, 1, t), lambda bi, h: (bi, h, 0, 0))],
        out_specs=head_rows,
        out_shape=jax.ShapeDtypeStruct((b, t, H_C * HD_C), BF16),
        scratch_shapes=[pltpu.VMEM(((nq * (nq + 1)) // 2, tq, tq), F32),
                        pltpu.VMEM((t, HD_C), BF16), pltpu.VMEM((t, HD_C), BF16)],
        compiler_params=_params("arbitrary", "arbitrary"),
    )(q, k, v, f_rows, f_heads.reshape(b, H_C, 1, t))


def _fox_sample_kernel(q_ref, kc_ref, vc_ref, kn_ref, vn_ref, fct_ref, fnc_ref, fec_ref, fnt_ref,
                       o_ref, s_ref, p_ref, m_ref, l_ref, acc_ref, *, t, tk):
    j = pl.program_id(1)
    contract_last = (((1,), (1,)), ((), ()))

    @pl.when(j == 0)
    def _():
        m_ref[...] = jnp.full_like(m_ref, NEG)
        l_ref[...] = jnp.zeros_like(l_ref)
        acc_ref[...] = jnp.zeros_like(acc_ref)

    def head_rows(h):
        return slice(h * t, (h + 1) * t)

    def update(width):
        s = s_ref[:, 0:width]
        m_prev = m_ref[...]
        m_col = jnp.maximum(m_prev[:, 0:1], jnp.max(s, axis=-1, keepdims=True))
        m_new = jnp.broadcast_to(m_col, m_prev.shape)
        alpha = jnp.exp(m_prev - m_new)
        p = jnp.exp(s - m_col)
        l_ref[...] = alpha * l_ref[...] + jnp.sum(p, axis=-1, keepdims=True)
        acc_ref[...] = alpha * acc_ref[...]
        m_ref[...] = m_new
        p_ref[:, 0:width] = p.astype(BF16)

    fq_cache = fnc_ref[...] + fec_ref[...]
    for h in range(H_C):
        hs = slice(h * HD_C, (h + 1) * HD_C)
        kh = kc_ref[pl.ds(h, tk, stride=H_C), :].astype(BF16)
        s = lax.dot_general(q_ref[:, hs], kh, contract_last, preferred_element_type=F32)
        s_ref[head_rows(h), :] = s / math.sqrt(HD_C) + fq_cache[head_rows(h)] - fct_ref[h:h + 1, :]
    update(tk)
    for h in range(H_C):
        vh = vc_ref[pl.ds(h, tk, stride=H_C), :].astype(BF16)
        acc_ref[head_rows(h), :] += jnp.dot(p_ref[head_rows(h), :], vh, preferred_element_type=F32)

    @pl.when(j == pl.num_programs(1) - 1)
    def _():
        causal = (lax.broadcasted_iota(jnp.int32, (t, t), 1)
                  <= lax.broadcasted_iota(jnp.int32, (t, t), 0))
        fq_new = fnc_ref[...]
        for h in range(H_C):
            hs = slice(h * HD_C, (h + 1) * HD_C)
            s = lax.dot_general(q_ref[:, hs], kn_ref[:, hs].astype(BF16), contract_last,
                                preferred_element_type=F32)
            s = s / math.sqrt(HD_C) + fq_new[head_rows(h)] - fnt_ref[h:h + 1, :]
            s_ref[head_rows(h), 0:t] = jnp.where(causal, s, NEG)
        update(t)
        for h in range(H_C):
            hs = slice(h * HD_C, (h + 1) * HD_C)
            pv = jnp.dot(p_ref[head_rows(h), 0:t], vn_ref[:, hs].astype(BF16), preferred_element_type=F32)
            o_ref[:, hs] = ((acc_ref[head_rows(h), :] + pv) / l_ref[head_rows(h), :]).astype(o_ref.dtype)


def _fox_sample(q, k_new, v_new, k_cache, v_cache, layer, fc_heads, fn_heads):
    b, t, d = q.shape
    n_layers, _, p = k_cache.shape[:3]
    tk = _tile(p, 1024)
    rows = H_C * t
    fn_col = fn_heads.reshape(b, rows, 1)
    fc_end_col = jnp.broadcast_to(fc_heads[:, :, p - 1:], (b, H_C, t)).reshape(b, rows, 1)
    cache_spec = pl.BlockSpec((None, None, tk * H_C, HD_C), lambda bi, j: (layer, bi, j, 0))
    new_spec = pl.BlockSpec((None, t, d), lambda bi, j: (bi, 0, 0))
    col_spec = pl.BlockSpec((None, rows, 1), lambda bi, j: (bi, 0, 0))
    return pl.pallas_call(
        functools.partial(_fox_sample_kernel, t=t, tk=tk),
        name="fox_sample",
        grid=(b, p // tk),
        in_specs=[new_spec, cache_spec, cache_spec, new_spec, new_spec,
                  pl.BlockSpec((None, H_C, tk), lambda bi, j: (bi, 0, j)),
                  col_spec, col_spec,
                  pl.BlockSpec((None, H_C, t), lambda bi, j: (bi, 0, 0))],
        out_specs=new_spec,
        out_shape=jax.ShapeDtypeStruct((b, t, d), BF16),
        scratch_shapes=[pltpu.VMEM((rows, tk), F32), pltpu.VMEM((rows, tk), BF16),
                        pltpu.VMEM((rows, LANES), F32), pltpu.VMEM((rows, LANES), F32),
                        pltpu.VMEM((rows, HD_C), F32)],
        compiler_params=_params("arbitrary", "arbitrary"),
    )(q, k_cache.reshape(n_layers, b, p * H_C, HD_C), v_cache.reshape(n_layers, b, p * H_C, HD_C),
      k_new, v_new, fc_heads, fn_col, fc_end_col, fn_heads)


def kernel(x_prompt, x_sample, c_prompt, c_sample, cache_conv, cache_win_k, cache_win_v, cache_fox_k, cache_fox_v, cache_fox_logf, ada_w, ada_b, norm_mix_g, norm_ffn_g, e_w_in, e_w_dw, e_b_dw, e_ln_g, e_ln_b, e_sinks, e_w_out, o_w_in, o_b_f, o_w_out, ffn_w_gate, ffn_w_up, ffn_w_down, final_g):
    bp, tp, d = x_prompt.shape
    bs, ts, _ = x_sample.shape
    depth = ada_w.shape[0]
    n_odd = o_w_in.shape[0]
    past_len = cache_fox_k.shape[2]
    nqkv = H_C * HD_C

    mod = _adaln(jnp.concatenate([c_prompt, c_sample], axis=0), ada_w, ada_b)
    tab_p = _rope_tables(jnp.arange(tp))
    tab_s = tuple(jnp.tile(x, (bs, 1)) for x in _rope_tables(past_len + jnp.arange(ts)))

    xp = x_prompt.reshape(bp * tp, d)
    xs = x_sample.reshape(bs * ts, d)
    hp = _modulate(x_prompt, norm_mix_g[0], mod, 0, 0, 0).reshape(bp * tp, d)
    hs = _modulate(x_sample, norm_mix_g[0], mod, 0, 0, bp).reshape(bs * ts, d)
    outs = {name: [] for name in ("conv_p", "wk_p", "wv_p", "fl_p",
                                  "conv_s", "wk_s", "wv_s", "fk_s", "fv_s", "fl_s")}
    fk_p = fv_p = None
    e_in, e_out = e_w_in.astype(BF16), e_w_out.astype(BF16)
    o_in, o_out = o_w_in.astype(BF16), o_w_out.astype(BF16)
    wg, wu, wd = ffn_w_gate.astype(BF16), ffn_w_up.astype(BF16), ffn_w_down.astype(BF16)
    for l in range(depth):
        if l % 2 == 0:
            e = l // 2
            w_out, w_out_layer = e_out, e
            conv = (e_w_dw[e], e_b_dw[e], e_ln_g[e], e_ln_b[e])
            zp = _even_proj(hp, e_in, e, tab_p, tp).reshape(bp, tp, E_IN)
            cat, cst = _even_mix(zp, e_sinks[e], *conv)
            win = min(WINDOW, tp)
            outs["conv_p"].append(cst[:, HIST_PAD - (CONV_W - 1):])
            outs["wk_p"].append(zp[:, tp - win:, K_OFF:V_OFF].reshape(bp, win, KV_B, HD_B))
            outs["wv_p"].append(zp[:, tp - win:, V_OFF:].reshape(bp, win, KV_B, HD_B))
            yp = cat.reshape(bp * tp, d)

            zs = _even_proj(hs, e_in, e, tab_s, bs * ts).reshape(bs, ts, E_IN)
            hist = jnp.pad(cache_conv[e], ((0, 0), (HIST_PAD - (CONV_W - 1), 0), (0, 0)))
            cat, cst = _even_mix_sample(zs, hist, cache_win_k[e], cache_win_v[e], e_sinks[e], *conv)
            outs["conv_s"].append(cst[:, HIST_PAD - (CONV_W - 1):])
            outs["wk_s"].append(zs[:, :, K_OFF:V_OFF].reshape(bs, ts, KV_B, HD_B))
            outs["wv_s"].append(zs[:, :, V_OFF:].reshape(bs, ts, KV_B, HD_B))
            ys = cat.reshape(bs * ts, d)
        else:
            o = l // 2
            wf = jnp.pad(o_w_in[o][:, 3 * nqkv:], ((0, 0), (0, LANES - H_C))).astype(BF16)[None]
            bf = jnp.pad(o_b_f[o], (0, LANES - H_C)).reshape(1, LANES)
            w_out, w_out_layer = o_out, o

            q = _matmul(hp, o_in, o, 0, nqkv, BF16).reshape(bp, tp, nqkv)
            k, fk_p = _kv_proj(hp, o_in, o, nqkv, fk_p, o, n_odd)
            v, fv_p = _kv_proj(hp, o_in, o, 2 * nqkv, fv_p, o, n_odd)
            logit = _matmul(hp, wf, 0, 0, LANES, F32).reshape(bp, tp, LANES)
            lf, f_rows, f_heads = _forget_scan(logit, bf, True)
            outs["fl_p"].append(lf)
            yp = _fox_prompt(q, k.reshape(bp, tp, nqkv), v.reshape(bp, tp, nqkv),
                             f_rows, f_heads).reshape(bp * tp, d)

            q = _matmul(hs, o_in, o, 0, nqkv, BF16).reshape(bs, ts, nqkv)
            k = _matmul(hs, o_in, o, nqkv, nqkv, F32).reshape(bs, ts, nqkv)
            v = _matmul(hs, o_in, o, 2 * nqkv, nqkv, F32).reshape(bs, ts, nqkv)
            logit = _matmul(hs, wf, 0, 0, LANES, F32).reshape(bs, ts, LANES)
            lf, _, fn_heads = _forget_scan(logit, bf, True)
            cache_lf = jnp.pad(cache_fox_logf[o], ((0, 0), (0, 0), (0, LANES - H_C)))
            _, _, fc_heads = _forget_scan(cache_lf, bf, False)
            outs["fk_s"].append(k.reshape(bs, ts, H_C, HD_C))
            outs["fv_s"].append(v.reshape(bs, ts, H_C, HD_C))
            outs["fl_s"].append(lf)
            ys = _fox_sample(q, k, v, cache_fox_k, cache_fox_v, o, fc_heads,
                             fn_heads).reshape(bs * ts, d)

        xp, hp = _residual_matmul(yp, w_out, w_out_layer, xp, norm_ffn_g[l], mod, l, tp, 0)
        xs, hs = _residual_matmul(ys, w_out, w_out_layer, xs, norm_ffn_g[l], mod, l, ts, bp)

        if l + 1 < depth:
            xp, hp = _ffn(hp, xp, wg, wu, wd, norm_mix_g[l + 1], mod, l, tp, 0, False)
            xs, hs = _ffn(hs, xs, wg, wu, wd, norm_mix_g[l + 1], mod, l, ts, bp, False)
        else:
            y_prompt = _ffn(hp, xp, wg, wu, wd, final_g, mod, l, tp, 0, True).reshape(bp, tp, d)
            y_sample = _ffn(hs, xs, wg, wu, wd, final_g, mod, l, ts, bp, True).reshape(bs, ts, d)

    stack = lambda name: jnp.stack(outs[name])
    return (y_prompt, y_sample,
            stack("conv_p"), stack("wk_p"), stack("wv_p"),
            fk_p.reshape(n_odd, bp, tp, H_C, HD_C), fv_p.reshape(n_odd, bp, tp, H_C, HD_C), stack("fl_p"),
            stack("conv_s"), stack("wk_s"), stack("wv_s"), stack("fk_s"), stack("fv_s"), stack("fl_s"))
```

```python
import functools
import math

import jax
import jax.numpy as jnp
from jax import lax
from jax.experimental import pallas as pl
from jax.experimental.pallas import tpu as pltpu

F32 = jnp.float32
BF16 = jnp.bfloat16

D_MODEL = 2048
CHUNK = 64
C_CONV = D_MODEL // 2
CONV_W = 31
HIST_PAD = 32
HD_B = 64
H_B = (D_MODEL // 2) // HD_B
KV_B = H_B // 4
G_B = H_B // KV_B
WINDOW = 128
E_IN = 2 * C_CONV + H_B * HD_B + 2 * KV_B * HD_B
Q_OFF = 2 * C_CONV
K_OFF = Q_OFF + H_B * HD_B
V_OFF = K_OFF + KV_B * HD_B
HD_C = 128
H_C = D_MODEL // HD_C
ROPE_THETA = 10000.0
EPS = 1e-6
NEG = -1e30
LANES = 128
SUBLANES = 8
VMEM_LIMIT = 56 * 1024 * 1024


def _params(*sem):
    return pltpu.CompilerParams(dimension_semantics=sem, vmem_limit_bytes=VMEM_LIMIT)


def _tile(n, pref):
    t = min(n, pref)
    while n % t:
        t //= 2
    return t


def _bcast_rows(vec, rows):
    nb, _, w = vec.shape
    if nb == 1:
        return vec[0]
    return jnp.broadcast_to(vec, (nb, rows // nb, w)).reshape(rows, w)


def _mod_spec(layer, chunk, nb, width, index_fn):
    return pl.BlockSpec((None, None, nb, 1, width),
                        lambda *g: (layer, chunk) + tuple(index_fn(*g)))


def _row_tiling(m, rows_per_batch, pref):
    if m > rows_per_batch and m > pref:
        tm = _tile(rows_per_batch, pref)
        return tm, 1, lambda i, row_off: (i * tm) // rows_per_batch + row_off
    nb = max(m // rows_per_batch, 1)
    return m, nb, lambda i, row_off: row_off // nb + i


def _rms_scale(x):
    return lax.rsqrt(jnp.mean(x * x, axis=-1, keepdims=True) + EPS)


def _adaln_kernel(c_ref, w_ref, b_ref, o_ref):
    c = c_ref[...]
    a = (c * jax.nn.sigmoid(c)).astype(BF16)
    o_ref[...] = jnp.dot(a, w_ref[...].astype(BF16), preferred_element_type=F32) + b_ref[...]


def _adaln(c_all, ada_w, ada_b):
    n_layers, d, n = ada_w.shape
    nb = c_all.shape[0]
    tn = 1024
    per = d // tn
    out = pl.pallas_call(
        _adaln_kernel,
        name="adaln",
        grid=(n_layers, n // tn),
        in_specs=[pl.BlockSpec((nb, d), lambda l, j: (0, 0)),
                  pl.BlockSpec((None, d, tn), lambda l, j: (l, 0, j)),
                  pl.BlockSpec((None, 1, tn), lambda l, j: (l, 0, j))],
        out_specs=pl.BlockSpec((None, None, nb, tn), lambda l, j: (l, j // per, 0, j % per)),
        out_shape=jax.ShapeDtypeStruct((n_layers, n // d, nb, d), F32),
        compiler_params=_params("arbitrary", "arbitrary"),
    )(c_all, ada_w, ada_b.reshape(n_layers, 1, n))
    return out.reshape(n_layers, n // d, nb, 1, d)


def _modulate_kernel(x_ref, g_ref, sh_ref, sc_ref, o_ref):
    x = x_ref[...]
    y = x * _rms_scale(x) * (g_ref[...] * (1.0 + sc_ref[0]))
    o_ref[...] = (y + sh_ref[0]).astype(o_ref.dtype)


def _modulate(x, g, mod, layer, chunk, row_off):
    b, t, d = x.shape
    tt = _tile(t, 512)
    return pl.pallas_call(
        _modulate_kernel,
        name="modulate",
        grid=(b, t // tt),
        in_specs=[pl.BlockSpec((None, tt, d), lambda bi, i: (bi, i, 0)),
                  pl.BlockSpec((1, d), lambda bi, i: (0, 0)),
                  _mod_spec(layer, chunk, 1, d, lambda bi, i: (bi + row_off, 0, 0)),
                  _mod_spec(layer, chunk + 1, 1, d, lambda bi, i: (bi + row_off, 0, 0))],
        out_specs=pl.BlockSpec((None, tt, d), lambda bi, i: (bi, i, 0)),
        out_shape=jax.ShapeDtypeStruct((b, t, d), BF16),
        compiler_params=_params("arbitrary", "arbitrary"),
    )(x, g.reshape(1, d), mod, mod)


def _mm_kernel(a_ref, w_ref, o_ref):
    o_ref[...] = jnp.dot(a_ref[...], w_ref[...], preferred_element_type=F32).astype(o_ref.dtype)


def _matmul(a, w, layer, col0, n, out_dtype):
    m, k = a.shape
    tm, tn = _tile(m, 1024), _tile(n, 1024)
    assert col0 % tn == 0
    return pl.pallas_call(
        _mm_kernel,
        name="matmul",
        grid=(m // tm, n // tn),
        in_specs=[pl.BlockSpec((tm, k), lambda i, j: (i, 0)),
                  pl.BlockSpec((None, k, tn), lambda i, j: (layer, 0, col0 // tn + j))],
        out_specs=pl.BlockSpec((tm, tn), lambda i, j: (i, j)),
        out_shape=jax.ShapeDtypeStruct((m, n), out_dtype),
        compiler_params=_params("arbitrary", "arbitrary"),
    )(a, w)


def _kv_proj_kernel(a_ref, w_ref, *refs, tm, tn):
    rows_ref, heads_ref = refs[-2:]
    a = a_ref[...]
    per = tn // HD_C
    for j in range(w_ref.shape[1] // tn):
        y = jnp.dot(a, w_ref[:, j * tn:(j + 1) * tn], preferred_element_type=F32)
        rows_ref[:, j * tn:(j + 1) * tn] = y.astype(rows_ref.dtype)
        for hh in range(per):
            heads_ref[pl.ds(j * per + hh, tm, stride=H_C), :] = y[:, hh * HD_C:(hh + 1) * HD_C]


def _kv_proj(a, w, w_layer, col0, stacked, layer, n_layers):
    m, k = a.shape
    n = H_C * HD_C
    assert col0 % n == 0
    tm = _tile(m, 512)
    in_specs = [pl.BlockSpec((tm, k), lambda i: (i, 0)),
                pl.BlockSpec((None, k, n), lambda i: (w_layer, 0, col0 // n))]
    args = [a, w]
    aliases = {}
    if stacked is not None:
        in_specs.append(pl.BlockSpec(memory_space=pl.ANY))
        args.append(stacked)
        aliases = {2: 1}
    return pl.pallas_call(
        functools.partial(_kv_proj_kernel, tm=tm, tn=512),
        name="kv_proj",
        grid=(m // tm,),
        in_specs=in_specs,
        out_specs=[pl.BlockSpec((tm, n), lambda i: (i, 0)),
                   pl.BlockSpec((None, tm * H_C, HD_C), lambda i: (layer, i, 0))],
        out_shape=[jax.ShapeDtypeStruct((m, n), BF16),
                   jax.ShapeDtypeStruct((n_layers, m * H_C, HD_C), F32)],
        input_output_aliases=aliases,
        compiler_params=_params("arbitrary"),
    )(*args)


def _rope_tables(pos):
    half = HD_B // 2
    freqs = ROPE_THETA ** (-jnp.arange(half, dtype=F32) / half)
    ang = pos.astype(F32)[:, None] * freqs[None, :]
    cos, sin = jnp.cos(ang), jnp.sin(ang)
    zero = jnp.zeros_like(sin)
    rep = LANES // HD_B
    cos_t = jnp.tile(jnp.concatenate([cos, cos], axis=1), (1, rep))
    sa = jnp.tile(jnp.concatenate([-sin, zero], axis=1), (1, rep))
    sb = jnp.tile(jnp.concatenate([zero, sin], axis=1), (1, rep))
    return cos_t, sa, sb


def _rope(x, cos, sa, sb):
    half = HD_B // 2
    outs = []
    for g in range(x.shape[1] // LANES):
        xg = x[:, g * LANES:(g + 1) * LANES]
        outs.append(xg * cos + pltpu.roll(xg, LANES - half, 1) * sa + pltpu.roll(xg, half, 1) * sb)
    return outs[0] if len(outs) == 1 else jnp.concatenate(outs, axis=1)


def _even_proj_kernel(a_ref, w_ref, cos_ref, sa_ref, sb_ref, o_ref, *, tn):
    a = a_ref[...]
    kw = KV_B * HD_B
    for j in range(E_IN // tn):
        c0 = j * tn
        z = jnp.dot(a, w_ref[:, c0:c0 + tn], preferred_element_type=F32)
        if c0 + tn <= Q_OFF:
            o_ref[:, c0:c0 + tn] = z
        elif c0 + tn <= K_OFF:
            o_ref[:, c0:c0 + tn] = _rope(z, cos_ref[...], sa_ref[...], sb_ref[...])
        else:
            o_ref[:, c0:c0 + kw] = _rope(z[:, :kw], cos_ref[...], sa_ref[...], sb_ref[...])
            o_ref[:, c0 + kw:c0 + tn] = z[:, kw:]


def _even_proj(a, w, layer, tables, period):
    m, k = a.shape
    n = w.shape[2]
    tn = 2 * KV_B * HD_B
    assert Q_OFF % tn == 0 and K_OFF % tn == 0 and n == K_OFF + tn
    tm = _tile(period, 512) if m > period else m
    per_blocks = max(period // tm, 1)
    tab_spec = pl.BlockSpec((tm, LANES), lambda i: (i % per_blocks, 0))
    return pl.pallas_call(
        functools.partial(_even_proj_kernel, tn=tn),
        name="even_proj",
        grid=(m // tm,),
        in_specs=[pl.BlockSpec((tm, k), lambda i: (i, 0)),
                  pl.BlockSpec((None, k, n), lambda i: (layer, 0, 0), pipeline_mode=pl.Buffered(1)),
                  tab_spec, tab_spec, tab_spec],
        out_specs=pl.BlockSpec((tm, n), lambda i: (i, 0)),
        out_shape=jax.ShapeDtypeStruct((m, n), F32),
        compiler_params=_params("arbitrary"),
    )(a, w, *tables)


def _residual_kernel(a_ref, w_ref, x_ref, gate_ref, g_ref, sh_ref, sc_ref, xo_ref, ho_ref, *, tn):
    tm, d = xo_ref.shape
    a = a_ref[...]
    gate = _bcast_rows(gate_ref[...], tm)
    ss = jnp.zeros((tm, 1), F32)
    for j in range(d // tn):
        cs = slice(j * tn, (j + 1) * tn)
        xn = x_ref[:, cs] + gate[:, cs] * jnp.dot(a, w_ref[:, cs], preferred_element_type=F32)
        xo_ref[:, cs] = xn
        ss = ss + jnp.sum(xn * xn, axis=-1, keepdims=True)
    gs = _bcast_rows(g_ref[...] * (1.0 + sc_ref[...]), tm)
    h = xo_ref[...] * lax.rsqrt(ss / d + EPS) * gs + _bcast_rows(sh_ref[...], tm)
    ho_ref[...] = h.astype(ho_ref.dtype)


def _residual_matmul(a, w, w_layer, x, g, mod, layer, rows_per_batch, row_off):
    m, k = a.shape
    d = w.shape[2]
    tm, nb, row_fn = _row_tiling(m, rows_per_batch, 512)
    vec = lambda chunk: _mod_spec(layer, chunk, nb, d, lambda i: (row_fn(i, row_off), 0, 0))
    return pl.pallas_call(
        functools.partial(_residual_kernel, tn=512),
        name="residual",
        grid=(m // tm,),
        in_specs=[pl.BlockSpec((tm, k), lambda i: (i, 0)),
                  pl.BlockSpec((None, k, d), lambda i: (w_layer, 0, 0)),
                  pl.BlockSpec((tm, d), lambda i: (i, 0)),
                  vec(2),
                  pl.BlockSpec((1, 1, d), lambda i: (0, 0, 0)),
                  vec(3), vec(4)],
        out_specs=[pl.BlockSpec((tm, d), lambda i: (i, 0)),
                   pl.BlockSpec((tm, d), lambda i: (i, 0))],
        out_shape=[jax.ShapeDtypeStruct((m, d), F32), jax.ShapeDtypeStruct((m, d), BF16)],
        compiler_params=_params("arbitrary"),
    )(a, w, x, mod, g.reshape(1, 1, d), mod, mod)


def _ffn_kernel(h_ref, x_ref, gate_ref, wg_ref, wu_ref, wd_ref, g_ref, *refs, final, rows):
    if final:
        y_ref, acc_ref = refs
    else:
        sh_ref, sc_ref, xo_ref, ho_ref, acc_ref = refs
    f = pl.program_id(1)
    tm, d = acc_ref.shape

    @pl.when(f == 0)
    def _():
        acc_ref[...] = jnp.zeros_like(acc_ref)

    h = h_ref[...]
    g = jnp.dot(h, wg_ref[...], preferred_element_type=F32)
    u = jnp.dot(h, wu_ref[...], preferred_element_type=F32)
    a = (g * jax.nn.sigmoid(g) * u).astype(BF16)
    acc_ref[...] += jnp.dot(a, wd_ref[...], preferred_element_type=F32)

    @pl.when(f == pl.num_programs(1) - 1)
    def _():
        nb = gate_ref.shape[0]
        per = tm // nb
        for r0 in range(0, tm, rows):
            rs = slice(r0, r0 + rows)
            b = r0 // per
            xn = x_ref[rs, :] + gate_ref[b] * acc_ref[rs, :]
            if final:
                y_ref[rs, :] = xn * _rms_scale(xn) * g_ref[0]
            else:
                xo_ref[rs, :] = xn
                hn = xn * _rms_scale(xn) * (g_ref[0] * (1.0 + sc_ref[b])) + sh_ref[b]
                ho_ref[rs, :] = hn.astype(ho_ref.dtype)


def _ffn(h, x, wg, wu, wd, g_next, mod, layer, rows_per_batch, row_off, final):
    m, d = h.shape
    ff = wg.shape[2]
    tf = _tile(ff, 512)
    tm, nb, row_fn = _row_tiling(m, rows_per_batch, 512)
    vec = lambda lyr, chunk: _mod_spec(lyr, chunk, nb, d, lambda i, f: (row_fn(i, row_off), 0, 0))
    rows_spec = pl.BlockSpec((tm, d), lambda i, f: (i, 0))
    in_specs = [rows_spec, rows_spec, vec(layer, 5),
                pl.BlockSpec((None, d, tf), lambda i, f: (layer, 0, f)),
                pl.BlockSpec((None, d, tf), lambda i, f: (layer, 0, f)),
                pl.BlockSpec((None, tf, d), lambda i, f: (layer, f, 0)),
                pl.BlockSpec((1, 1, d), lambda i, f: (0, 0, 0))]
    args = [h, x, mod, wg, wu, wd, g_next.reshape(1, 1, d)]
    if final:
        out_specs = rows_spec
        out_shape = jax.ShapeDtypeStruct((m, d), F32)
    else:
        in_specs += [vec(layer + 1, 0), vec(layer + 1, 1)]
        args += [mod, mod]
        out_specs = [rows_spec, rows_spec]
        out_shape = [jax.ShapeDtypeStruct((m, d), F32), jax.ShapeDtypeStruct((m, d), BF16)]
    return pl.pallas_call(
        functools.partial(_ffn_kernel, final=final, rows=2 * SUBLANES),
        name="ffn",
        grid=(m // tm, ff // tf),
        in_specs=in_specs,
        out_specs=out_specs,
        out_shape=out_shape,
        scratch_shapes=[pltpu.VMEM((tm, d), F32)],
        compiler_params=_params("arbitrary", "arbitrary"),
    )(*args)


def _glu_rows(z_ref, uext_ref, rows):
    step = min(rows, 4 * SUBLANES)
    for r in range(0, rows, step):
        uext_ref[HIST_PAD + r:HIST_PAD + r + step, :] = (
            z_ref[r:r + step, 0:C_CONV] * jax.nn.sigmoid(z_ref[r:r + step, C_CONV:2 * C_CONV]))


def _conv_ln_silu(uext_ref, wdw_ref, bdw_ref, lng_ref, lnb_ref, y_ref, shift_ref, out_ref, rows):
    base = HIST_PAD - (CONV_W - 1)
    for c in range(C_CONV // LANES):
        cs = slice(c * LANES, (c + 1) * LANES)
        for r in range(SUBLANES):
            n = rows + SUBLANES * ((CONV_W - 1 - r) // SUBLANES)
            shift_ref[r, 0:n, :] = uext_ref[base + r:base + r + n, cs]
        rb = min(rows, 8 * SUBLANES)

        def row_block(i, carry):
            t0 = pl.multiple_of(i * rb, rb)
            acc = jnp.zeros((rb, LANES), F32)
            for j in range(CONV_W):
                r, a = j % SUBLANES, j // SUBLANES
                acc = acc + wdw_ref[j:j + 1, cs] * shift_ref[r, pl.ds(t0 + SUBLANES * a, rb), :]
            y_ref[pl.ds(t0, rb), cs] = acc + bdw_ref[:, cs]
            return carry

        lax.fori_loop(0, rows // rb, row_block, 0)
    step = min(rows, 4 * SUBLANES)
    for r in range(0, rows, step):
        y = y_ref[r:r + step, :]
        yc = y - jnp.mean(y, axis=-1, keepdims=True)
        var = jnp.mean(yc * yc, axis=-1, keepdims=True)
        yn = yc * lax.rsqrt(var + EPS) * lng_ref[...] + lnb_ref[...]
        out_ref[r:r + step, 0:C_CONV] = (yn * jax.nn.sigmoid(yn)).astype(out_ref.dtype)


def _sink_attention(q, k, v, sink_col, mask):
    s = lax.dot_general(q, k, (((1,), (1,)), ((), ())), preferred_element_type=F32) / math.sqrt(HD_B)
    if mask is not None:
        s = jnp.where(mask, s, NEG)
    m = jnp.maximum(jnp.max(s, axis=-1, keepdims=True), sink_col)
    p = jnp.exp(s - m)
    den = jnp.sum(p, axis=-1, keepdims=True) + jnp.exp(sink_col - m)
    return jnp.dot(p.astype(BF16), v, preferred_element_type=F32) / den


def _sink_column(sink_ref, kv_head, rows):
    return jnp.concatenate(
        [jnp.full((rows, 1), sink_ref[kv_head * G_B + g], F32) for g in range(G_B)], axis=0)


def _group_attention(sink_ref, z_ref, kall, vall, r0, rows, mask, cat_ref):
    for kh in range(KV_B):
        hs = slice(kh * HD_B, (kh + 1) * HD_B)
        q = jnp.concatenate(
            [z_ref[r0:r0 + rows, Q_OFF + (kh * G_B + g) * HD_B:Q_OFF + (kh * G_B + g + 1) * HD_B]
             for g in range(G_B)], axis=0).astype(BF16)
        o = _sink_attention(q, kall[:, hs], vall[:, hs], _sink_column(sink_ref, kh, rows), mask)
        for gp in range(G_B // 2):
            pair = jnp.concatenate([o[(2 * gp) * rows:(2 * gp + 1) * rows],
                                    o[(2 * gp + 1) * rows:(2 * gp + 2) * rows]], axis=1)
            c0 = C_CONV + (kh * G_B + 2 * gp) * HD_B
            cat_ref[r0:r0 + rows, c0:c0 + 2 * HD_B] = pair.astype(cat_ref.dtype)


def _even_mix_kernel(sink_ref, zc_ref, zh_ref, zkv_ref, wdw_ref, bdw_ref, lng_ref, lnb_ref,
                     cat_ref, cst_ref, uext_ref, y_ref, shift_ref, *, tt):
    i = pl.program_id(1)
    first = i == 0

    _glu_rows(zc_ref, uext_ref, tt)
    hist = zh_ref[:, 0:C_CONV] * jax.nn.sigmoid(zh_ref[:, C_CONV:2 * C_CONV])
    uext_ref[0:HIST_PAD, :] = jnp.where(first, 0.0, hist)
    _conv_ln_silu(uext_ref, wdw_ref, bdw_ref, lng_ref, lnb_ref, y_ref, shift_ref, cat_ref, tt)

    @pl.when(i == pl.num_programs(1) - 1)
    def _():
        cst_ref[...] = uext_ref[tt:tt + HIST_PAD, :]

    kw = KV_B * HD_B
    sub = 2 * CHUNK
    kext = jnp.concatenate([zkv_ref[:, 0:kw], zc_ref[:, K_OFF:K_OFF + kw]], axis=0).astype(BF16)
    v_t = jnp.concatenate([zkv_ref[:, kw:2 * kw], zc_ref[:, V_OFF:V_OFF + kw]], axis=0).T.astype(BF16)
    cols = G_B * sub
    kc = lax.broadcasted_iota(jnp.int32, (2 * sub, cols), 0) // CHUNK
    qc = (lax.broadcasted_iota(jnp.int32, (2 * sub, cols), 1) % sub) // CHUNK
    band = jnp.logical_and(kc >= qc, kc <= qc + WINDOW // CHUNK)
    band_first = jnp.logical_and(band, jnp.logical_or(jnp.logical_not(first), kc >= WINDOW // CHUNK))
    for s in range(tt // sub):
        r0 = s * sub
        mask = band_first if s == 0 else band
        for kh in range(KV_B):
            hs = slice(kh * HD_B, (kh + 1) * HD_B)
            q = jnp.concatenate(
                [zc_ref[r0:r0 + sub, Q_OFF + (kh * G_B + g) * HD_B:Q_OFF + (kh * G_B + g + 1) * HD_B]
                 for g in range(G_B)], axis=0).astype(BF16)
            logit = lax.dot_general(kext[r0:r0 + 2 * sub, hs], q, (((1,), (1,)), ((), ())),
                                    preferred_element_type=F32) / math.sqrt(HD_B)
            logit = jnp.where(mask, logit, NEG)
            sink = jnp.concatenate(
                [jnp.full((1, sub), sink_ref[kh * G_B + g], F32) for g in range(G_B)], axis=1)
            m = jnp.maximum(jnp.max(logit, axis=0, keepdims=True), sink)
            p = jnp.exp(logit - m)
            den = jnp.sum(p, axis=0, keepdims=True) + jnp.exp(sink - m)
            o_t = jnp.dot(v_t[hs, r0:r0 + 2 * sub], p.astype(BF16), preferred_element_type=F32) / den
            for gp in range(G_B // 2):
                pair_t = jnp.concatenate([o_t[:, (2 * gp) * sub:(2 * gp + 1) * sub],
                                          o_t[:, (2 * gp + 1) * sub:(2 * gp + 2) * sub]], axis=0)
                c0 = C_CONV + (kh * G_B + 2 * gp) * HD_B
                cat_ref[r0:r0 + sub, c0:c0 + 2 * HD_B] = pair_t.T.astype(cat_ref.dtype)


def _even_mix(z, sinks, w_dw, b_dw, ln_g, ln_b):
    b, t, _ = z.shape
    tt = _tile(t, 256)
    hb = tt // HIST_PAD
    kb = tt // WINDOW
    kvw = 2 * KV_B * HD_B
    vec = pl.BlockSpec((1, C_CONV), lambda bi, i: (0, 0))
    return pl.pallas_call(
        functools.partial(_even_mix_kernel, tt=tt),
        name="even_mix",
        grid=(b, t // tt),
        in_specs=[pl.BlockSpec(memory_space=pltpu.SMEM),
                  pl.BlockSpec((None, tt, E_IN), lambda bi, i: (bi, i, 0)),
                  pl.BlockSpec((None, HIST_PAD, 2 * C_CONV),
                               lambda bi, i: (bi, jnp.maximum(i * hb - 1, 0), 0)),
                  pl.BlockSpec((None, WINDOW, kvw),
                               lambda bi, i: (bi, jnp.maximum(i * kb - 1, 0), K_OFF // kvw)),
                  pl.BlockSpec((CONV_W, C_CONV), lambda bi, i: (0, 0)),
                  vec, vec, vec],
        out_specs=[pl.BlockSpec((None, tt, D_MODEL), lambda bi, i: (bi, i, 0)),
                   pl.BlockSpec((None, HIST_PAD, C_CONV), lambda bi, i: (bi, 0, 0))],
        out_shape=[jax.ShapeDtypeStruct((b, t, D_MODEL), BF16),
                   jax.ShapeDtypeStruct((b, HIST_PAD, C_CONV), F32)],
        scratch_shapes=[pltpu.VMEM((HIST_PAD + tt, C_CONV), F32),
                        pltpu.VMEM((tt, C_CONV), F32),
                        pltpu.VMEM((SUBLANES, tt + HIST_PAD, LANES), F32)],
        compiler_params=_params("arbitrary", "arbitrary"),
    )(sinks, z, z, z, w_dw, b_dw.reshape(1, -1), ln_g.reshape(1, -1), ln_b.reshape(1, -1))


def _even_mix_sample_kernel(sink_ref, z_ref, hist_ref, kc_ref, vc_ref, wdw_ref, bdw_ref, lng_ref,
                            lnb_ref, cat_ref, cst_ref, uext_ref, y_ref, shift_ref, *, t):
    uext_ref[0:HIST_PAD, :] = hist_ref[...]
    _glu_rows(z_ref, uext_ref, t)
    _conv_ln_silu(uext_ref, wdw_ref, bdw_ref, lng_ref, lnb_ref, y_ref, shift_ref, cat_ref, t)
    cst_ref[...] = uext_ref[t:t + HIST_PAD, :]

    kw = KV_B * HD_B
    kall = jnp.concatenate([kc_ref[...], z_ref[:, K_OFF:K_OFF + kw]], axis=0).astype(BF16)
    vall = jnp.concatenate([vc_ref[...], z_ref[:, V_OFF:V_OFF + kw]], axis=0).astype(BF16)
    _group_attention(sink_ref, z_ref, kall, vall, 0, t, None, cat_ref)


def _even_mix_sample(z, hist, k_cache, v_cache, sinks, w_dw, b_dw, ln_g, ln_b):
    b, t, _ = z.shape
    win = k_cache.shape[1]
    kw = KV_B * HD_B
    vec = pl.BlockSpec((1, C_CONV), lambda bi: (0, 0))
    return pl.pallas_call(
        functools.partial(_even_mix_sample_kernel, t=t),
        name="even_mix_sample",
        grid=(b,),
        in_specs=[pl.BlockSpec(memory_space=pltpu.SMEM),
                  pl.BlockSpec((None, t, E_IN), lambda bi: (bi, 0, 0)),
                  pl.BlockSpec((None, HIST_PAD, C_CONV), lambda bi: (bi, 0, 0)),
                  pl.BlockSpec((None, win, kw), lambda bi: (bi, 0, 0)),
                  pl.BlockSpec((None, win, kw), lambda bi: (bi, 0, 0)),
                  pl.BlockSpec((CONV_W, C_CONV), lambda bi: (0, 0)),
                  vec, vec, vec],
        out_specs=[pl.BlockSpec((None, t, D_MODEL), lambda bi: (bi, 0, 0)),
                   pl.BlockSpec((None, HIST_PAD, C_CONV), lambda bi: (bi, 0, 0))],
        out_shape=[jax.ShapeDtypeStruct((b, t, D_MODEL), BF16),
                   jax.ShapeDtypeStruct((b, HIST_PAD, C_CONV), F32)],
        scratch_shapes=[pltpu.VMEM((HIST_PAD + t, C_CONV), F32),
                        pltpu.VMEM((t, C_CONV), F32),
                        pltpu.VMEM((SUBLANES, t + HIST_PAD, LANES), F32)],
        compiler_params=_params("arbitrary"),
    )(sinks, z, hist, k_cache.reshape(b, win, kw), v_cache.reshape(b, win, kw),
      w_dw, b_dw.reshape(1, -1), ln_g.reshape(1, -1), ln_b.reshape(1, -1))


def _scan_kernel(x_ref, b_ref, lf_ref, f_ref, ft_ref, carry_ref, *, tt, log_sigmoid):
    i = pl.program_id(1)

    @pl.when(i == 0)
    def _():
        carry_ref[...] = jnp.zeros_like(carry_ref)

    x = x_ref[...]
    if log_sigmoid:
        x = x + b_ref[...]
        x = jnp.minimum(x, 0.0) - jnp.log1p(jnp.exp(-jnp.abs(x)))
    lf_ref[...] = x[:, :H_C]
    row = lax.broadcasted_iota(jnp.int32, x.shape, 0)
    step = 1
    while step < tt:
        x = x + jnp.where(row >= step, pltpu.roll(x, step, 0), 0.0)
        step *= 2
    f = x + carry_ref[...]
    carry_ref[...] = f[tt - 1:tt, :]
    f_ref[...] = f
    if tt < LANES:
        f = jnp.concatenate([f, jnp.zeros((LANES - tt, LANES), F32)], axis=0)
    ft_ref[...] = f.T[:H_C, :tt]


def _forget_scan(x, bias, log_sigmoid):
    b, t, _ = x.shape
    tt = _tile(t, 512)
    return pl.pallas_call(
        functools.partial(_scan_kernel, tt=tt, log_sigmoid=log_sigmoid),
        name="forget_scan",
        grid=(b, t // tt),
        in_specs=[pl.BlockSpec((None, tt, LANES), lambda bi, i: (bi, i, 0)),
                  pl.BlockSpec((1, LANES), lambda bi, i: (0, 0))],
        out_specs=[pl.BlockSpec((None, tt, H_C), lambda bi, i: (bi, i, 0)),
                   pl.BlockSpec((None, tt, LANES), lambda bi, i: (bi, i, 0)),
                   pl.BlockSpec((None, H_C, tt), lambda bi, i: (bi, 0, i))],
        out_shape=[jax.ShapeDtypeStruct((b, t, H_C), F32),
                   jax.ShapeDtypeStruct((b, t, LANES), F32),
                   jax.ShapeDtypeStruct((b, H_C, t), F32)],
        scratch_shapes=[pltpu.VMEM((1, LANES), F32)],
        compiler_params=_params("arbitrary", "arbitrary"),
    )(x, bias)


def _lane_fold(x, op):
    out = x[:, 0:LANES]
    for c in range(1, x.shape[1] // LANES):
        out = op(out, x[:, c * LANES:(c + 1) * LANES])
    return out


def _fox_kernel(q_ref, k_ref, v_ref, fq_ref, fk_ref, o_ref, s_ref, vb_ref, *, tq, t):
    h = pl.program_id(1)
    vb_ref[:, 0:HD_C] = v_ref[...]
    vb_ref[:, HD_C:] = jnp.ones((t, HD_C), BF16)
    lane = lax.broadcasted_iota(jnp.int32, (tq, LANES), 1)
    causal = (lax.broadcasted_iota(jnp.int32, (tq, tq), 1)
              <= lax.broadcasted_iota(jnp.int32, (tq, tq), 0))
    for qi in reversed(range(t // tq)):
        rows = slice(qi * tq, (qi + 1) * tq)
        base = (qi * (qi + 1)) // 2
        q = q_ref[rows, :]
        fq = jnp.sum(jnp.where(lane == h, fq_ref[rows, :], 0.0), axis=-1, keepdims=True)
        mx = None
        for j in range(qi + 1):
            ks = slice(j * tq, (j + 1) * tq)
            s = lax.dot_general(q, k_ref[ks, :], (((1,), (1,)), ((), ())), preferred_element_type=F32)
            s = s / math.sqrt(HD_C) + fq - fk_ref[:, ks]
            if j == qi:
                s = jnp.where(causal, s, NEG)
            s_ref[base + j] = s
            fold = _lane_fold(s, jnp.maximum)
            mx = fold if mx is None else jnp.maximum(mx, fold)
        m = jnp.max(mx, axis=-1, keepdims=True)
        acc = None
        for j in range(qi + 1):
            p = jnp.exp(s_ref[base + j] - m).astype(BF16)
            pv = jnp.dot(p, vb_ref[j * tq:(j + 1) * tq, :], preferred_element_type=F32)
            acc = pv if acc is None else acc + pv
        o_ref[rows, :] = (acc[:, 0:HD_C] / acc[:, HD_C:]).astype(o_ref.dtype)


def _fox_prompt(q, k, v, f_rows, f_heads):
    b, t, _ = q.shape
    tq = _tile(t, 512)
    nq = t // tq
    head_rows = pl.BlockSpec((None, t, HD_C), lambda bi, h: (bi, 0, h))
    return pl.pallas_call(
        functools.partial(_fox_kernel, tq=tq, t=t),
        name="fox_prompt",
        grid=(b, H_C),
        in_specs=[head_rows, head_rows, head_rows,
                  pl.BlockSpec((None, t, LANES), lambda bi, h: (bi, 0, 0)),
                  pl.BlockSpec((None, None, 1, t), lambda bi, h: (bi, h, 0, 0))],
        out_specs=head_rows,
        out_shape=jax.ShapeDtypeStruct((b, t, H_C * HD_C), BF16),
        scratch_shapes=[pltpu.VMEM(((nq * (nq + 1)) // 2, tq, tq), F32),
                        pltpu.VMEM((t, 2 * HD_C), BF16)],
        compiler_params=_params("arbitrary", "arbitrary"),
    )(q, k, v, f_rows, f_heads.reshape(b, H_C, 1, t))


def _fox_sample_kernel(q_ref, kc_ref, vc_ref, kn_ref, vn_ref, fct_ref, fnc_ref, fec_ref, fnt_ref,
                       o_ref, s_ref, p_ref, m_ref, l_ref, acc_ref, *, t, tk):
    j = pl.program_id(1)
    contract_last = (((1,), (1,)), ((), ()))

    @pl.when(j == 0)
    def _():
        m_ref[...] = jnp.full_like(m_ref, NEG)
        l_ref[...] = jnp.zeros_like(l_ref)
        acc_ref[...] = jnp.zeros_like(acc_ref)

    def head_rows(h):
        return slice(h * t, (h + 1) * t)

    def update(width):
        s = s_ref[:, 0:width]
        m_prev = m_ref[...]
        m_col = jnp.maximum(m_prev[:, 0:1], jnp.max(s, axis=-1, keepdims=True))
        m_new = jnp.broadcast_to(m_col, m_prev.shape)
        alpha = jnp.exp(m_prev - m_new)
        p = jnp.exp(s - m_col)
        l_ref[...] = alpha * l_ref[...] + jnp.sum(p, axis=-1, keepdims=True)
        acc_ref[...] = alpha * acc_ref[...]
        m_ref[...] = m_new
        p_ref[:, 0:width] = p.astype(BF16)

    fq_cache = fnc_ref[...] + fec_ref[...]
    for h in range(H_C):
        hs = slice(h * HD_C, (h + 1) * HD_C)
        kh = kc_ref[pl.ds(h, tk, stride=H_C), :].astype(BF16)
        s = lax.dot_general(q_ref[:, hs], kh, contract_last, preferred_element_type=F32)
        s_ref[head_rows(h), :] = s / math.sqrt(HD_C) + fq_cache[head_rows(h)] - fct_ref[h:h + 1, :]
    update(tk)
    for h in range(H_C):
        vh = vc_ref[pl.ds(h, tk, stride=H_C), :].astype(BF16)
        acc_ref[head_rows(h), :] += jnp.dot(p_ref[head_rows(h), :], vh, preferred_element_type=F32)

    @pl.when(j == pl.num_programs(1) - 1)
    def _():
        causal = (lax.broadcasted_iota(jnp.int32, (t, t), 1)
                  <= lax.broadcasted_iota(jnp.int32, (t, t), 0))
        fq_new = fnc_ref[...]
        for h in range(H_C):
            hs = slice(h * HD_C, (h + 1) * HD_C)
            s = lax.dot_general(q_ref[:, hs], kn_ref[:, hs].astype(BF16), contract_last,
                                preferred_element_type=F32)
            s = s / math.sqrt(HD_C) + fq_new[head_rows(h)] - fnt_ref[h:h + 1, :]
            s_ref[head_rows(h), 0:t] = jnp.where(causal, s, NEG)
        update(t)
        for h in range(H_C):
            hs = slice(h * HD_C, (h + 1) * HD_C)
            pv = jnp.dot(p_ref[head_rows(h), 0:t], vn_ref[:, hs].astype(BF16), preferred_element_type=F32)
            o_ref[:, hs] = ((acc_ref[head_rows(h), :] + pv) / l_ref[head_rows(h), :]).astype(o_ref.dtype)


def _fox_sample(q, k_new, v_new, k_cache, v_cache, layer, fc_heads, fn_heads):
    b, t, d = q.shape
    n_layers, _, p = k_cache.shape[:3]
    tk = _tile(p, 1024)
    rows = H_C * t
    fn_col = fn_heads.reshape(b, rows, 1)
    fc_end_col = jnp.broadcast_to(fc_heads[:, :, p - 1:], (b, H_C, t)).reshape(b, rows, 1)
    cache_spec = pl.BlockSpec((None, None, tk * H_C, HD_C), lambda bi, j: (layer, bi, j, 0))
    new_spec = pl.BlockSpec((None, t, d), lambda bi, j: (bi, 0, 0))
    col_spec = pl.BlockSpec((None, rows, 1), lambda bi, j: (bi, 0, 0))
    return pl.pallas_call(
        functools.partial(_fox_sample_kernel, t=t, tk=tk),
        name="fox_sample",
        grid=(b, p // tk),
        in_specs=[new_spec, cache_spec, cache_spec, new_spec, new_spec,
                  pl.BlockSpec((None, H_C, tk), lambda bi, j: (bi, 0, j)),
                  col_spec, col_spec,
                  pl.BlockSpec((None, H_C, t), lambda bi, j: (bi, 0, 0))],
        out_specs=new_spec,
        out_shape=jax.ShapeDtypeStruct((b, t, d), BF16),
        scratch_shapes=[pltpu.VMEM((rows, tk), F32), pltpu.VMEM((rows, tk), BF16),
                        pltpu.VMEM((rows, LANES), F32), pltpu.VMEM((rows, LANES), F32),
                        pltpu.VMEM((rows, HD_C), F32)],
        compiler_params=_params("arbitrary", "arbitrary"),
    )(q, k_cache.reshape(n_layers, b, p * H_C, HD_C), v_cache.reshape(n_layers, b, p * H_C, HD_C),
      k_new, v_new, fc_heads, fn_col, fc_end_col, fn_heads)


def kernel(x_prompt, x_sample, c_prompt, c_sample, cache_conv, cache_win_k, cache_win_v, cache_fox_k, cache_fox_v, cache_fox_logf, ada_w, ada_b, norm_mix_g, norm_ffn_g, e_w_in, e_w_dw, e_b_dw, e_ln_g, e_ln_b, e_sinks, e_w_out, o_w_in, o_b_f, o_w_out, ffn_w_gate, ffn_w_up, ffn_w_down, final_g):
    bp, tp, d = x_prompt.shape
    bs, ts, _ = x_sample.shape
    depth = ada_w.shape[0]
    n_odd = o_w_in.shape[0]
    past_len = cache_fox_k.shape[2]
    nqkv = H_C * HD_C

    mod = _adaln(jnp.concatenate([c_prompt, c_sample], axis=0), ada_w, ada_b)
    tab_p = _rope_tables(jnp.arange(tp))
    tab_s = tuple(jnp.tile(x, (bs, 1)) for x in _rope_tables(past_len + jnp.arange(ts)))

    xp = x_prompt.reshape(bp * tp, d)
    xs = x_sample.reshape(bs * ts, d)
    hp = _modulate(x_prompt, norm_mix_g[0], mod, 0, 0, 0).reshape(bp * tp, d)
    hs = _modulate(x_sample, norm_mix_g[0], mod, 0, 0, bp).reshape(bs * ts, d)
    outs = {name: [] for name in ("conv_p", "wk_p", "wv_p", "fl_p",
                                  "conv_s", "wk_s", "wv_s", "fk_s", "fv_s", "fl_s")}
    fk_p = fv_p = None
    e_in, e_out = e_w_in.astype(BF16), e_w_out.astype(BF16)
    o_in, o_out = o_w_in.astype(BF16), o_w_out.astype(BF16)
    wg, wu, wd = ffn_w_gate.astype(BF16), ffn_w_up.astype(BF16), ffn_w_down.astype(BF16)
    for l in range(depth):
        if l % 2 == 0:
            e = l // 2
            w_out, w_out_layer = e_out, e
            conv = (e_w_dw[e], e_b_dw[e], e_ln_g[e], e_ln_b[e])
            zp = _even_proj(hp, e_in, e, tab_p, tp).reshape(bp, tp, E_IN)
            cat, cst = _even_mix(zp, e_sinks[e], *conv)
            win = min(WINDOW, tp)
            outs["conv_p"].append(cst[:, HIST_PAD - (CONV_W - 1):])
            outs["wk_p"].append(zp[:, tp - win:, K_OFF:V_OFF].reshape(bp, win, KV_B, HD_B))
            outs["wv_p"].append(zp[:, tp - win:, V_OFF:].reshape(bp, win, KV_B, HD_B))
            yp = cat.reshape(bp * tp, d)

            zs = _even_proj(hs, e_in, e, tab_s, bs * ts).reshape(bs, ts, E_IN)
            hist = jnp.pad(cache_conv[e], ((0, 0), (HIST_PAD - (CONV_W - 1), 0), (0, 0)))
            cat, cst = _even_mix_sample(zs, hist, cache_win_k[e], cache_win_v[e], e_sinks[e], *conv)
            outs["conv_s"].append(cst[:, HIST_PAD - (CONV_W - 1):])
            outs["wk_s"].append(zs[:, :, K_OFF:V_OFF].reshape(bs, ts, KV_B, HD_B))
            outs["wv_s"].append(zs[:, :, V_OFF:].reshape(bs, ts, KV_B, HD_B))
            ys = cat.reshape(bs * ts, d)
        else:
            o = l // 2
            wf = jnp.pad(o_w_in[o][:, 3 * nqkv:], ((0, 0), (0, LANES - H_C))).astype(BF16)[None]
            bf = jnp.pad(o_b_f[o], (0, LANES - H_C)).reshape(1, LANES)
            w_out, w_out_layer = o_out, o

            q = _matmul(hp, o_in, o, 0, nqkv, BF16).reshape(bp, tp, nqkv)
            k, fk_p = _kv_proj(hp, o_in, o, nqkv, fk_p, o, n_odd)
            v, fv_p = _kv_proj(hp, o_in, o, 2 * nqkv, fv_p, o, n_odd)
            logit = _matmul(hp, wf, 0, 0, LANES, F32).reshape(bp, tp, LANES)
            lf, f_rows, f_heads = _forget_scan(logit, bf, True)
            outs["fl_p"].append(lf)
            yp = _fox_prompt(q, k.reshape(bp, tp, nqkv), v.reshape(bp, tp, nqkv),
                             f_rows, f_heads).reshape(bp * tp, d)

            q = _matmul(hs, o_in, o, 0, nqkv, BF16).reshape(bs, ts, nqkv)
            k = _matmul(hs, o_in, o, nqkv, nqkv, F32).reshape(bs, ts, nqkv)
            v = _matmul(hs, o_in, o, 2 * nqkv, nqkv, F32).reshape(bs, ts, nqkv)
            logit = _matmul(hs, wf, 0, 0, LANES, F32).reshape(bs, ts, LANES)
            lf, _, fn_heads = _forget_scan(logit, bf, True)
            cache_lf = jnp.pad(cache_fox_logf[o], ((0, 0), (0, 0), (0, LANES - H_C)))
            _, _, fc_heads = _forget_scan(cache_lf, bf, False)
            outs["fk_s"].append(k.reshape(bs, ts, H_C, HD_C))
            outs["fv_s"].append(v.reshape(bs, ts, H_C, HD_C))
            outs["fl_s"].append(lf)
            ys = _fox_sample(q, k, v, cache_fox_k, cache_fox_v, o, fc_heads,
                             fn_heads).reshape(bs * ts, d)

        xp, hp = _residual_matmul(yp, w_out, w_out_layer, xp, norm_ffn_g[l], mod, l, tp, 0)
        xs, hs = _residual_matmul(ys, w_out, w_out_layer, xs, norm_ffn_g[l], mod, l, ts, bp)

        if l + 1 < depth:
            xp, hp = _ffn(hp, xp, wg, wu, wd, norm_mix_g[l + 1], mod, l, tp, 0, False)
            xs, hs = _ffn(hs, xs, wg, wu, wd, norm_mix_g[l + 1], mod, l, ts, bp, False)
        else:
            y_prompt = _ffn(hp, xp, wg, wu, wd, final_g, mod, l, tp, 0, True).reshape(bp, tp, d)
            y_sample = _ffn(hs, xs, wg, wu, wd, final_g, mod, l, ts, bp, True).reshape(bs, ts, d)

    stack = lambda name: jnp.stack(outs[name])
    return (y_prompt, y_sample,
            stack("conv_p"), stack("wk_p"), stack("wv_p"),
            fk_p.reshape(n_odd, bp, tp, H_C, HD_C), fv_p.reshape(n_odd, bp, tp, H_C, HD_C), stack("fl_p"),
            stack("conv_s"), stack("wk_s"), stack("wv_s"), stack("fk_s"), stack("fv_s"), stack("fl_s"))
```

```python
import functools
import math

import jax
import jax.numpy as jnp
from jax import lax
from jax.experimental import pallas as pl
from jax.experimental.pallas import tpu as pltpu

F32 = jnp.float32
BF16 = jnp.bfloat16

D_MODEL = 2048
CHUNK = 64
C_CONV = D_MODEL // 2
CONV_W = 31
HIST_PAD = 32
HD_B = 64
H_B = (D_MODEL // 2) // HD_B
KV_B = H_B // 4
G_B = H_B // KV_B
WINDOW = 128
E_IN = 2 * C_CONV + H_B * HD_B + 2 * KV_B * HD_B
Q_OFF = 2 * C_CONV
K_OFF = Q_OFF + H_B * HD_B
V_OFF = K_OFF + KV_B * HD_B
HD_C = 128
H_C = D_MODEL // HD_C
ROPE_THETA = 10000.0
EPS = 1e-6
NEG = -1e30
LANES = 128
SUBLANES = 8
VMEM_LIMIT = 56 * 1024 * 1024


def _params(*sem):
    return pltpu.CompilerParams(dimension_semantics=sem, vmem_limit_bytes=VMEM_LIMIT)


def _tile(n, pref):
    t = min(n, pref)
    while n % t:
        t //= 2
    return t


def _bcast_rows(vec, rows):
    nb, _, w = vec.shape
    if nb == 1:
        return vec[0]
    return jnp.broadcast_to(vec, (nb, rows // nb, w)).reshape(rows, w)


def _mod_spec(layer, chunk, nb, width, index_fn):
    return pl.BlockSpec((None, None, nb, 1, width),
                        lambda *g: (layer, chunk) + tuple(index_fn(*g)))


def _row_tiling(m, rows_per_batch, pref):
    if m > rows_per_batch and m > pref:
        tm = _tile(rows_per_batch, pref)
        return tm, 1, lambda i, row_off: (i * tm) // rows_per_batch + row_off
    nb = max(m // rows_per_batch, 1)
    return m, nb, lambda i, row_off: row_off // nb + i


def _rms_scale(x):
    return lax.rsqrt(jnp.mean(x * x, axis=-1, keepdims=True) + EPS)


def _adaln_kernel(c_ref, w_ref, b_ref, o_ref):
    c = c_ref[...]
    a = (c * jax.nn.sigmoid(c)).astype(BF16)
    o_ref[...] = jnp.dot(a, w_ref[...].astype(BF16), preferred_element_type=F32) + b_ref[...]


def _adaln(c_all, ada_w, ada_b):
    n_layers, d, n = ada_w.shape
    nb = c_all.shape[0]
    tn = 1024
    per = d // tn
    out = pl.pallas_call(
        _adaln_kernel,
        name="adaln",
        grid=(n_layers, n // tn),
        in_specs=[pl.BlockSpec((nb, d), lambda l, j: (0, 0)),
                  pl.BlockSpec((None, d, tn), lambda l, j: (l, 0, j)),
                  pl.BlockSpec((None, 1, tn), lambda l, j: (l, 0, j))],
        out_specs=pl.BlockSpec((None, None, nb, tn), lambda l, j: (l, j // per, 0, j % per)),
        out_shape=jax.ShapeDtypeStruct((n_layers, n // d, nb, d), F32),
        compiler_params=_params("arbitrary", "arbitrary"),
    )(c_all, ada_w, ada_b.reshape(n_layers, 1, n))
    return out.reshape(n_layers, n // d, nb, 1, d)


def _modulate_kernel(x_ref, g_ref, sh_ref, sc_ref, o_ref):
    x = x_ref[...]
    y = x * _rms_scale(x) * (g_ref[...] * (1.0 + sc_ref[0]))
    o_ref[...] = (y + sh_ref[0]).astype(o_ref.dtype)


def _modulate(x, g, mod, layer, chunk, row_off):
    b, t, d = x.shape
    tt = _tile(t, 512)
    return pl.pallas_call(
        _modulate_kernel,
        name="modulate",
        grid=(b, t // tt),
        in_specs=[pl.BlockSpec((None, tt, d), lambda bi, i: (bi, i, 0)),
                  pl.BlockSpec((1, d), lambda bi, i: (0, 0)),
                  _mod_spec(layer, chunk, 1, d, lambda bi, i: (bi + row_off, 0, 0)),
                  _mod_spec(layer, chunk + 1, 1, d, lambda bi, i: (bi + row_off, 0, 0))],
        out_specs=pl.BlockSpec((None, tt, d), lambda bi, i: (bi, i, 0)),
        out_shape=jax.ShapeDtypeStruct((b, t, d), BF16),
        compiler_params=_params("arbitrary", "arbitrary"),
    )(x, g.reshape(1, d), mod, mod)


def _mm_kernel(a_ref, w_ref, o_ref):
    o_ref[...] = jnp.dot(a_ref[...], w_ref[...], preferred_element_type=F32).astype(o_ref.dtype)


def _matmul(a, w, layer, col0, n, out_dtype):
    m, k = a.shape
    tm, tn = _tile(m, 1024), _tile(n, 1024)
    assert col0 % tn == 0
    return pl.pallas_call(
        _mm_kernel,
        name="matmul",
        grid=(m // tm, n // tn),
        in_specs=[pl.BlockSpec((tm, k), lambda i, j: (i, 0)),
                  pl.BlockSpec((None, k, tn), lambda i, j: (layer, 0, col0 // tn + j))],
        out_specs=pl.BlockSpec((tm, tn), lambda i, j: (i, j)),
        out_shape=jax.ShapeDtypeStruct((m, n), out_dtype),
        compiler_params=_params("arbitrary", "arbitrary"),
    )(a, w)


def _kv_proj_kernel(a_ref, w_ref, *refs, tm, tn):
    rows_ref, heads_ref, y_ref = refs[-3:]
    a = a_ref[...]
    for j in range(w_ref.shape[1] // tn):
        y = jnp.dot(a, w_ref[:, j * tn:(j + 1) * tn], preferred_element_type=F32)
        rows_ref[:, j * tn:(j + 1) * tn] = y.astype(rows_ref.dtype)
        y_ref[:, j * tn:(j + 1) * tn] = y
    step = 8 * SUBLANES
    for r in range(0, tm, step):
        heads_ref[r * H_C:(r + step) * H_C, :] = pltpu.einshape(
            "t(hd)->(th)d", y_ref[r:r + step, :], h=H_C)


def _kv_proj(a, w, w_layer, col0, stacked, layer, n_layers):
    m, k = a.shape
    n = H_C * HD_C
    assert col0 % n == 0
    tm = _tile(m, 512)
    in_specs = [pl.BlockSpec((tm, k), lambda i: (i, 0)),
                pl.BlockSpec((None, k, n), lambda i: (w_layer, 0, col0 // n))]
    args = [a, w]
    aliases = {}
    if stacked is not None:
        in_specs.append(pl.BlockSpec(memory_space=pl.ANY))
        args.append(stacked)
        aliases = {2: 1}
    return pl.pallas_call(
        functools.partial(_kv_proj_kernel, tm=tm, tn=512),
        name="kv_proj",
        grid=(m // tm,),
        in_specs=in_specs,
        out_specs=[pl.BlockSpec((tm, n), lambda i: (i, 0)),
                   pl.BlockSpec((None, tm * H_C, HD_C), lambda i: (layer, i, 0))],
        out_shape=[jax.ShapeDtypeStruct((m, n), BF16),
                   jax.ShapeDtypeStruct((n_layers, m * H_C, HD_C), F32)],
        input_output_aliases=aliases,
        scratch_shapes=[pltpu.VMEM((tm, n), F32)],
        compiler_params=_params("arbitrary"),
    )(*args)


def _rope_tables(pos):
    half = HD_B // 2
    freqs = ROPE_THETA ** (-jnp.arange(half, dtype=F32) / half)
    ang = pos.astype(F32)[:, None] * freqs[None, :]
    cos, sin = jnp.cos(ang), jnp.sin(ang)
    zero = jnp.zeros_like(sin)
    rep = LANES // HD_B
    cos_t = jnp.tile(jnp.concatenate([cos, cos], axis=1), (1, rep))
    sa = jnp.tile(jnp.concatenate([-sin, zero], axis=1), (1, rep))
    sb = jnp.tile(jnp.concatenate([zero, sin], axis=1), (1, rep))
    return cos_t, sa, sb


def _rope(x, cos, sa, sb):
    half = HD_B // 2
    outs = []
    for g in range(x.shape[1] // LANES):
        xg = x[:, g * LANES:(g + 1) * LANES]
        outs.append(xg * cos + pltpu.roll(xg, LANES - half, 1) * sa + pltpu.roll(xg, half, 1) * sb)
    return outs[0] if len(outs) == 1 else jnp.concatenate(outs, axis=1)


def _even_proj_kernel(a_ref, w_ref, cos_ref, sa_ref, sb_ref, o_ref, *, tn):
    a = a_ref[...]
    kw = KV_B * HD_B
    for j in range(E_IN // tn):
        c0 = j * tn
        z = jnp.dot(a, w_ref[:, c0:c0 + tn], preferred_element_type=F32)
        if c0 + tn <= Q_OFF:
            o_ref[:, c0:c0 + tn] = z
        elif c0 + tn <= K_OFF:
            o_ref[:, c0:c0 + tn] = _rope(z, cos_ref[...], sa_ref[...], sb_ref[...])
        else:
            o_ref[:, c0:c0 + kw] = _rope(z[:, :kw], cos_ref[...], sa_ref[...], sb_ref[...])
            o_ref[:, c0 + kw:c0 + tn] = z[:, kw:]


def _even_proj(a, w, layer, tables, period):
    m, k = a.shape
    n = w.shape[2]
    tn = 2 * KV_B * HD_B
    assert Q_OFF % tn == 0 and K_OFF % tn == 0 and n == K_OFF + tn
    tm = _tile(period, 512) if m > period else m
    per_blocks = max(period // tm, 1)
    tab_spec = pl.BlockSpec((tm, LANES), lambda i: (i % per_blocks, 0))
    return pl.pallas_call(
        functools.partial(_even_proj_kernel, tn=tn),
        name="even_proj",
        grid=(m // tm,),
        in_specs=[pl.BlockSpec((tm, k), lambda i: (i, 0)),
                  pl.BlockSpec((None, k, n), lambda i: (layer, 0, 0), pipeline_mode=pl.Buffered(1)),
                  tab_spec, tab_spec, tab_spec],
        out_specs=pl.BlockSpec((tm, n), lambda i: (i, 0)),
        out_shape=jax.ShapeDtypeStruct((m, n), F32),
        compiler_params=_params("arbitrary"),
    )(a, w, *tables)


def _residual_kernel(a_ref, w_ref, x_ref, gate_ref, g_ref, sh_ref, sc_ref, xo_ref, ho_ref, *, tn):
    tm, d = xo_ref.shape
    a = a_ref[...]
    gate = _bcast_rows(gate_ref[...], tm)
    ss = jnp.zeros((tm, 1), F32)
    for j in range(d // tn):
        cs = slice(j * tn, (j + 1) * tn)
        xn = x_ref[:, cs] + gate[:, cs] * jnp.dot(a, w_ref[:, cs], preferred_element_type=F32)
        xo_ref[:, cs] = xn
        ss = ss + jnp.sum(xn * xn, axis=-1, keepdims=True)
    gs = _bcast_rows(g_ref[...] * (1.0 + sc_ref[...]), tm)
    h = xo_ref[...] * lax.rsqrt(ss / d + EPS) * gs + _bcast_rows(sh_ref[...], tm)
    ho_ref[...] = h.astype(ho_ref.dtype)


def _residual_matmul(a, w, w_layer, x, g, mod, layer, rows_per_batch, row_off):
    m, k = a.shape
    d = w.shape[2]
    tm, nb, row_fn = _row_tiling(m, rows_per_batch, 512)
    vec = lambda chunk: _mod_spec(layer, chunk, nb, d, lambda i: (row_fn(i, row_off), 0, 0))
    return pl.pallas_call(
        functools.partial(_residual_kernel, tn=512),
        name="residual",
        grid=(m // tm,),
        in_specs=[pl.BlockSpec((tm, k), lambda i: (i, 0)),
                  pl.BlockSpec((None, k, d), lambda i: (w_layer, 0, 0)),
                  pl.BlockSpec((tm, d), lambda i: (i, 0)),
                  vec(2),
                  pl.BlockSpec((1, 1, d), lambda i: (0, 0, 0)),
                  vec(3), vec(4)],
        out_specs=[pl.BlockSpec((tm, d), lambda i: (i, 0)),
                   pl.BlockSpec((tm, d), lambda i: (i, 0))],
        out_shape=[jax.ShapeDtypeStruct((m, d), F32), jax.ShapeDtypeStruct((m, d), BF16)],
        compiler_params=_params("arbitrary"),
    )(a, w, x, mod, g.reshape(1, 1, d), mod, mod)


def _ffn_kernel(h_ref, x_ref, gate_ref, wg_ref, wu_ref, wd_ref, g_ref, *refs, final, rows):
    if final:
        y_ref, acc_ref = refs
    else:
        sh_ref, sc_ref, xo_ref, ho_ref, acc_ref = refs
    f = pl.program_id(1)
    tm, d = acc_ref.shape

    @pl.when(f == 0)
    def _():
        acc_ref[...] = jnp.zeros_like(acc_ref)

    h = h_ref[...]
    g = jnp.dot(h, wg_ref[...], preferred_element_type=F32)
    u = jnp.dot(h, wu_ref[...], preferred_element_type=F32)
    a = (g * jax.nn.sigmoid(g) * u).astype(BF16)
    acc_ref[...] += jnp.dot(a, wd_ref[...], preferred_element_type=F32)

    @pl.when(f == pl.num_programs(1) - 1)
    def _():
        nb = gate_ref.shape[0]
        per = tm // nb
        for r0 in range(0, tm, rows):
            rs = slice(r0, r0 + rows)
            b = r0 // per
            xn = x_ref[rs, :] + gate_ref[b] * acc_ref[rs, :]
            if final:
                y_ref[rs, :] = xn * _rms_scale(xn) * g_ref[0]
            else:
                xo_ref[rs, :] = xn
                hn = xn * _rms_scale(xn) * (g_ref[0] * (1.0 + sc_ref[b])) + sh_ref[b]
                ho_ref[rs, :] = hn.astype(ho_ref.dtype)


def _ffn(h, x, wg, wu, wd, g_next, mod, layer, rows_per_batch, row_off, final):
    m, d = h.shape
    ff = wg.shape[2]
    tf = _tile(ff, 512)
    tm, nb, row_fn = _row_tiling(m, rows_per_batch, 512)
    vec = lambda lyr, chunk: _mod_spec(lyr, chunk, nb, d, lambda i, f: (row_fn(i, row_off), 0, 0))
    rows_spec = pl.BlockSpec((tm, d), lambda i, f: (i, 0))
    in_specs = [rows_spec, rows_spec, vec(layer, 5),
                pl.BlockSpec((None, d, tf), lambda i, f: (layer, 0, f)),
                pl.BlockSpec((None, d, tf), lambda i, f: (layer, 0, f)),
                pl.BlockSpec((None, tf, d), lambda i, f: (layer, f, 0)),
                pl.BlockSpec((1, 1, d), lambda i, f: (0, 0, 0))]
    args = [h, x, mod, wg, wu, wd, g_next.reshape(1, 1, d)]
    if final:
        out_specs = rows_spec
        out_shape = jax.ShapeDtypeStruct((m, d), F32)
    else:
        in_specs += [vec(layer + 1, 0), vec(layer + 1, 1)]
        args += [mod, mod]
        out_specs = [rows_spec, rows_spec]
        out_shape = [jax.ShapeDtypeStruct((m, d), F32), jax.ShapeDtypeStruct((m, d), BF16)]
    return pl.pallas_call(
        functools.partial(_ffn_kernel, final=final, rows=2 * SUBLANES),
        name="ffn",
        grid=(m // tm, ff // tf),
        in_specs=in_specs,
        out_specs=out_specs,
        out_shape=out_shape,
        scratch_shapes=[pltpu.VMEM((tm, d), F32)],
        compiler_params=_params("arbitrary", "arbitrary"),
    )(*args)


def _glu_rows(z_ref, uext_ref, rows):
    step = min(rows, 4 * SUBLANES)
    for r in range(0, rows, step):
        uext_ref[HIST_PAD + r:HIST_PAD + r + step, :] = (
            z_ref[r:r + step, 0:C_CONV] * jax.nn.sigmoid(z_ref[r:r + step, C_CONV:2 * C_CONV]))


def _conv_ln_silu(uext_ref, wdw_ref, bdw_ref, lng_ref, lnb_ref, y_ref, shift_ref, out_ref, rows):
    base = HIST_PAD - (CONV_W - 1)
    for c in range(C_CONV // LANES):
        cs = slice(c * LANES, (c + 1) * LANES)
        for r in range(SUBLANES):
            n = rows + SUBLANES * ((CONV_W - 1 - r) // SUBLANES)
            shift_ref[r, 0:n, :] = uext_ref[base + r:base + r + n, cs]
        rb = min(rows, 8 * SUBLANES)

        def row_block(i, carry):
            t0 = pl.multiple_of(i * rb, rb)
            acc = jnp.zeros((rb, LANES), F32)
            for j in range(CONV_W):
                r, a = j % SUBLANES, j // SUBLANES
                acc = acc + wdw_ref[j:j + 1, cs] * shift_ref[r, pl.ds(t0 + SUBLANES * a, rb), :]
            y_ref[pl.ds(t0, rb), cs] = acc + bdw_ref[:, cs]
            return carry

        lax.fori_loop(0, rows // rb, row_block, 0)
    step = min(rows, 4 * SUBLANES)
    for r in range(0, rows, step):
        y = y_ref[r:r + step, :]
        yc = y - jnp.mean(y, axis=-1, keepdims=True)
        var = jnp.mean(yc * yc, axis=-1, keepdims=True)
        yn = yc * lax.rsqrt(var + EPS) * lng_ref[...] + lnb_ref[...]
        out_ref[r:r + step, 0:C_CONV] = (yn * jax.nn.sigmoid(yn)).astype(out_ref.dtype)


def _sink_attention(q, k, v, sink_col, mask):
    s = lax.dot_general(q, k, (((1,), (1,)), ((), ())), preferred_element_type=F32) / math.sqrt(HD_B)
    if mask is not None:
        s = jnp.where(mask, s, NEG)
    m = jnp.maximum(jnp.max(s, axis=-1, keepdims=True), sink_col)
    p = jnp.exp(s - m)
    den = jnp.sum(p, axis=-1, keepdims=True) + jnp.exp(sink_col - m)
    return jnp.dot(p.astype(BF16), v, preferred_element_type=F32) / den


def _sink_column(sink_ref, kv_head, rows):
    return jnp.concatenate(
        [jnp.full((rows, 1), sink_ref[kv_head * G_B + g], F32) for g in range(G_B)], axis=0)


def _group_attention(sink_ref, z_ref, kall, vall, r0, rows, mask, cat_ref):
    for kh in range(KV_B):
        hs = slice(kh * HD_B, (kh + 1) * HD_B)
        q = jnp.concatenate(
            [z_ref[r0:r0 + rows, Q_OFF + (kh * G_B + g) * HD_B:Q_OFF + (kh * G_B + g + 1) * HD_B]
             for g in range(G_B)], axis=0).astype(BF16)
        o = _sink_attention(q, kall[:, hs], vall[:, hs], _sink_column(sink_ref, kh, rows), mask)
        for gp in range(G_B // 2):
            pair = jnp.concatenate([o[(2 * gp) * rows:(2 * gp + 1) * rows],
                                    o[(2 * gp + 1) * rows:(2 * gp + 2) * rows]], axis=1)
            c0 = C_CONV + (kh * G_B + 2 * gp) * HD_B
            cat_ref[r0:r0 + rows, c0:c0 + 2 * HD_B] = pair.astype(cat_ref.dtype)


def _even_mix_kernel(sink_ref, zc_ref, zh_ref, zkv_ref, wdw_ref, bdw_ref, lng_ref, lnb_ref,
                     cat_ref, cst_ref, uext_ref, y_ref, shift_ref, *, tt):
    i = pl.program_id(1)
    first = i == 0

    _glu_rows(zc_ref, uext_ref, tt)
    hist = zh_ref[:, 0:C_CONV] * jax.nn.sigmoid(zh_ref[:, C_CONV:2 * C_CONV])
    uext_ref[0:HIST_PAD, :] = jnp.where(first, 0.0, hist)
    _conv_ln_silu(uext_ref, wdw_ref, bdw_ref, lng_ref, lnb_ref, y_ref, shift_ref, cat_ref, tt)

    @pl.when(i == pl.num_programs(1) - 1)
    def _():
        cst_ref[...] = uext_ref[tt:tt + HIST_PAD, :]

    kw = KV_B * HD_B
    sub = 2 * CHUNK
    kext = jnp.concatenate([zkv_ref[:, 0:kw], zc_ref[:, K_OFF:K_OFF + kw]], axis=0).astype(BF16)
    v_t = jnp.concatenate([zkv_ref[:, kw:2 * kw], zc_ref[:, V_OFF:V_OFF + kw]], axis=0).T.astype(BF16)
    cols = G_B * sub
    kc = lax.broadcasted_iota(jnp.int32, (2 * sub, cols), 0) // CHUNK
    qc = (lax.broadcasted_iota(jnp.int32, (2 * sub, cols), 1) % sub) // CHUNK
    band = jnp.logical_and(kc >= qc, kc <= qc + WINDOW // CHUNK)
    band_first = jnp.logical_and(band, jnp.logical_or(jnp.logical_not(first), kc >= WINDOW // CHUNK))
    for s in range(tt // sub):
        r0 = s * sub
        mask = band_first if s == 0 else band
        for kh in range(KV_B):
            hs = slice(kh * HD_B, (kh + 1) * HD_B)
            q = jnp.concatenate(
                [zc_ref[r0:r0 + sub, Q_OFF + (kh * G_B + g) * HD_B:Q_OFF + (kh * G_B + g + 1) * HD_B]
                 for g in range(G_B)], axis=0).astype(BF16)
            logit = lax.dot_general(kext[r0:r0 + 2 * sub, hs], q, (((1,), (1,)), ((), ())),
                                    preferred_element_type=F32) / math.sqrt(HD_B)
            logit = jnp.where(mask, logit, NEG)
            sink = jnp.concatenate(
                [jnp.full((1, sub), sink_ref[kh * G_B + g], F32) for g in range(G_B)], axis=1)
            m = jnp.maximum(jnp.max(logit, axis=0, keepdims=True), sink)
            p = jnp.exp(logit - m)
            den = jnp.sum(p, axis=0, keepdims=True) + jnp.exp(sink - m)
            o_t = jnp.dot(v_t[hs, r0:r0 + 2 * sub], p.astype(BF16), preferred_element_type=F32) / den
            for gp in range(G_B // 2):
                pair_t = jnp.concatenate([o_t[:, (2 * gp) * sub:(2 * gp + 1) * sub],
                                          o_t[:, (2 * gp + 1) * sub:(2 * gp + 2) * sub]], axis=0)
                c0 = C_CONV + (kh * G_B + 2 * gp) * HD_B
                cat_ref[r0:r0 + sub, c0:c0 + 2 * HD_B] = pair_t.T.astype(cat_ref.dtype)


def _even_mix(z, sinks, w_dw, b_dw, ln_g, ln_b):
    b, t, _ = z.shape
    tt = _tile(t, 256)
    hb = tt // HIST_PAD
    kb = tt // WINDOW
    kvw = 2 * KV_B * HD_B
    vec = pl.BlockSpec((1, C_CONV), lambda bi, i: (0, 0))
    return pl.pallas_call(
        functools.partial(_even_mix_kernel, tt=tt),
        name="even_mix",
        grid=(b, t // tt),
        in_specs=[pl.BlockSpec(memory_space=pltpu.SMEM),
                  pl.BlockSpec((None, tt, E_IN), lambda bi, i: (bi, i, 0)),
                  pl.BlockSpec((None, HIST_PAD, 2 * C_CONV),
                               lambda bi, i: (bi, jnp.maximum(i * hb - 1, 0), 0)),
                  pl.BlockSpec((None, WINDOW, kvw),
                               lambda bi, i: (bi, jnp.maximum(i * kb - 1, 0), K_OFF // kvw)),
                  pl.BlockSpec((CONV_W, C_CONV), lambda bi, i: (0, 0)),
                  vec, vec, vec],
        out_specs=[pl.BlockSpec((None, tt, D_MODEL), lambda bi, i: (bi, i, 0)),
                   pl.BlockSpec((None, HIST_PAD, C_CONV), lambda bi, i: (bi, 0, 0))],
        out_shape=[jax.ShapeDtypeStruct((b, t, D_MODEL), BF16),
                   jax.ShapeDtypeStruct((b, HIST_PAD, C_CONV), F32)],
        scratch_shapes=[pltpu.VMEM((HIST_PAD + tt, C_CONV), F32),
                        pltpu.VMEM((tt, C_CONV), F32),
                        pltpu.VMEM((SUBLANES, tt + HIST_PAD, LANES), F32)],
        compiler_params=_params("arbitrary", "arbitrary"),
    )(sinks, z, z, z, w_dw, b_dw.reshape(1, -1), ln_g.reshape(1, -1), ln_b.reshape(1, -1))


def _even_mix_sample_kernel(sink_ref, z_ref, hist_ref, kc_ref, vc_ref, wdw_ref, bdw_ref, lng_ref,
                            lnb_ref, cat_ref, cst_ref, uext_ref, y_ref, shift_ref, *, t):
    uext_ref[0:HIST_PAD, :] = hist_ref[...]
    _glu_rows(z_ref, uext_ref, t)
    _conv_ln_silu(uext_ref, wdw_ref, bdw_ref, lng_ref, lnb_ref, y_ref, shift_ref, cat_ref, t)
    cst_ref[...] = uext_ref[t:t + HIST_PAD, :]

    kw = KV_B * HD_B
    kall = jnp.concatenate([kc_ref[...], z_ref[:, K_OFF:K_OFF + kw]], axis=0).astype(BF16)
    vall = jnp.concatenate([vc_ref[...], z_ref[:, V_OFF:V_OFF + kw]], axis=0).astype(BF16)
    _group_attention(sink_ref, z_ref, kall, vall, 0, t, None, cat_ref)


def _even_mix_sample(z, hist, k_cache, v_cache, sinks, w_dw, b_dw, ln_g, ln_b):
    b, t, _ = z.shape
    win = k_cache.shape[1]
    kw = KV_B * HD_B
    vec = pl.BlockSpec((1, C_CONV), lambda bi: (0, 0))
    return pl.pallas_call(
        functools.partial(_even_mix_sample_kernel, t=t),
        name="even_mix_sample",
        grid=(b,),
        in_specs=[pl.BlockSpec(memory_space=pltpu.SMEM),
                  pl.BlockSpec((None, t, E_IN), lambda bi: (bi, 0, 0)),
                  pl.BlockSpec((None, HIST_PAD, C_CONV), lambda bi: (bi, 0, 0)),
                  pl.BlockSpec((None, win, kw), lambda bi: (bi, 0, 0)),
                  pl.BlockSpec((None, win, kw), lambda bi: (bi, 0, 0)),
                  pl.BlockSpec((CONV_W, C_CONV), lambda bi: (0, 0)),
                  vec, vec, vec],
        out_specs=[pl.BlockSpec((None, t, D_MODEL), lambda bi: (bi, 0, 0)),
                   pl.BlockSpec((None, HIST_PAD, C_CONV), lambda bi: (bi, 0, 0))],
        out_shape=[jax.ShapeDtypeStruct((b, t, D_MODEL), BF16),
                   jax.ShapeDtypeStruct((b, HIST_PAD, C_CONV), F32)],
        scratch_shapes=[pltpu.VMEM((HIST_PAD + t, C_CONV), F32),
                        pltpu.VMEM((t, C_CONV), F32),
                        pltpu.VMEM((SUBLANES, t + HIST_PAD, LANES), F32)],
        compiler_params=_params("arbitrary"),
    )(sinks, z, hist, k_cache.reshape(b, win, kw), v_cache.reshape(b, win, kw),
      w_dw, b_dw.reshape(1, -1), ln_g.reshape(1, -1), ln_b.reshape(1, -1))


def _scan_kernel(x_ref, b_ref, lf_ref, f_ref, ft_ref, carry_ref, *, tt, log_sigmoid):
    i = pl.program_id(1)

    @pl.when(i == 0)
    def _():
        carry_ref[...] = jnp.zeros_like(carry_ref)

    x = x_ref[...]
    if log_sigmoid:
        x = x + b_ref[...]
        x = jnp.minimum(x, 0.0) - jnp.log1p(jnp.exp(-jnp.abs(x)))
    lf_ref[...] = x[:, :H_C]
    row = lax.broadcasted_iota(jnp.int32, x.shape, 0)
    step = 1
    while step < tt:
        x = x + jnp.where(row >= step, pltpu.roll(x, step, 0), 0.0)
        step *= 2
    f = x + carry_ref[...]
    carry_ref[...] = f[tt - 1:tt, :]
    f_ref[...] = f
    if tt < LANES:
        f = jnp.concatenate([f, jnp.zeros((LANES - tt, LANES), F32)], axis=0)
    ft_ref[...] = f.T[:H_C, :tt]


def _forget_scan(x, bias, log_sigmoid):
    b, t, _ = x.shape
    tt = _tile(t, 512)
    return pl.pallas_call(
        functools.partial(_scan_kernel, tt=tt, log_sigmoid=log_sigmoid),
        name="forget_scan",
        grid=(b, t // tt),
        in_specs=[pl.BlockSpec((None, tt, LANES), lambda bi, i: (bi, i, 0)),
                  pl.BlockSpec((1, LANES), lambda bi, i: (0, 0))],
        out_specs=[pl.BlockSpec((None, tt, H_C), lambda bi, i: (bi, i, 0)),
                   pl.BlockSpec((None, tt, LANES), lambda bi, i: (bi, i, 0)),
                   pl.BlockSpec((None, H_C, tt), lambda bi, i: (bi, 0, i))],
        out_shape=[jax.ShapeDtypeStruct((b, t, H_C), F32),
                   jax.ShapeDtypeStruct((b, t, LANES), F32),
                   jax.ShapeDtypeStruct((b, H_C, t), F32)],
        scratch_shapes=[pltpu.VMEM((1, LANES), F32)],
        compiler_params=_params("arbitrary", "arbitrary"),
    )(x, bias)


def _lane_fold(x, op):
    out = x[:, 0:LANES]
    for c in range(1, x.shape[1] // LANES):
        out = op(out, x[:, c * LANES:(c + 1) * LANES])
    return out


def _fox_kernel(q_ref, k_ref, v_ref, fq_ref, fk_ref, o_ref, s_ref, vb_ref, *, tq, t):
    h = pl.program_id(1)
    vb_ref[:, 0:HD_C] = v_ref[...]
    vb_ref[:, HD_C:] = jnp.ones((t, HD_C), BF16)
    lane = lax.broadcasted_iota(jnp.int32, (tq, LANES), 1)
    causal = (lax.broadcasted_iota(jnp.int32, (tq, tq), 1)
              <= lax.broadcasted_iota(jnp.int32, (tq, tq), 0))
    for qi in reversed(range(t // tq)):
        rows = slice(qi * tq, (qi + 1) * tq)
        base = (qi * (qi + 1)) // 2
        q = q_ref[rows, :]
        fq = jnp.sum(jnp.where(lane == h, fq_ref[rows, :], 0.0), axis=-1, keepdims=True)
        mx = None
        for j in range(qi + 1):
            ks = slice(j * tq, (j + 1) * tq)
            s = lax.dot_general(q, k_ref[ks, :], (((1,), (1,)), ((), ())), preferred_element_type=F32)
            s = s / math.sqrt(HD_C) + fq - fk_ref[:, ks]
            if j == qi:
                s = jnp.where(causal, s, NEG)
            s_ref[base + j] = s
            fold = _lane_fold(s, jnp.maximum)
            mx = fold if mx is None else jnp.maximum(mx, fold)
        m = jnp.max(mx, axis=-1, keepdims=True)
        acc = None
        for j in range(qi + 1):
            p = jnp.exp(s_ref[base + j] - m).astype(BF16)
            pv = jnp.dot(p, vb_ref[j * tq:(j + 1) * tq, :], preferred_element_type=F32)
            acc = pv if acc is None else acc + pv
        o_ref[rows, :] = (acc[:, 0:HD_C] / acc[:, HD_C:]).astype(o_ref.dtype)


def _fox_prompt(q, k, v, f_rows, f_heads):
    b, t, _ = q.shape
    tq = _tile(t, 512)
    nq = t // tq
    head_rows = pl.BlockSpec((None, t, HD_C), lambda bi, h: (bi, 0, h))
    return pl.pallas_call(
        functools.partial(_fox_kernel, tq=tq, t=t),
        name="fox_prompt",
        grid=(b, H_C),
        in_specs=[head_rows, head_rows, head_rows,
                  pl.BlockSpec((None, t, LANES), lambda bi, h: (bi, 0, 0)),
                  pl.BlockSpec((None, None, 1, t), lambda bi, h: (bi, h, 0, 0))],
        out_specs=head_rows,
        out_shape=jax.ShapeDtypeStruct((b, t, H_C * HD_C), BF16),
        scratch_shapes=[pltpu.VMEM(((nq * (nq + 1)) // 2, tq, tq), F32),
                        pltpu.VMEM((t, 2 * HD_C), BF16)],
        compiler_params=_params("arbitrary", "arbitrary"),
    )(q, k, v, f_rows, f_heads.reshape(b, H_C, 1, t))


def _fox_sample_kernel(q_ref, kc_ref, vc_ref, kn_ref, vn_ref, fct_ref, fnc_ref, fec_ref, fnt_ref,
                       o_ref, s_ref, p_ref, m_ref, l_ref, acc_ref, kd_ref, vd_ref, *, t, tk):
    j = pl.program_id(1)
    contract_last = (((1,), (1,)), ((), ()))

    @pl.when(j == 0)
    def _():
        m_ref[...] = jnp.full_like(m_ref, NEG)
        l_ref[...] = jnp.zeros_like(l_ref)
        acc_ref[...] = jnp.zeros_like(acc_ref)

    def head_rows(h):
        return slice(h * t, (h + 1) * t)

    def update(width):
        s = s_ref[:, 0:width]
        m_prev = m_ref[...]
        m_col = jnp.maximum(m_prev[:, 0:1], jnp.max(s, axis=-1, keepdims=True))
        m_new = jnp.broadcast_to(m_col, m_prev.shape)
        alpha = jnp.exp(m_prev - m_new)
        p = jnp.exp(s - m_col)
        l_ref[...] = alpha * l_ref[...] + jnp.sum(p, axis=-1, keepdims=True)
        acc_ref[...] = alpha * acc_ref[...]
        m_ref[...] = m_new
        p_ref[:, 0:width] = p.astype(BF16)

    step = LANES
    for c in range(tk // step):
        src = slice(c * step * H_C, (c + 1) * step * H_C)
        dst = slice(c * step, (c + 1) * step)
        kd_ref[:, dst, :] = pltpu.einshape("(ph)d->hpd", kc_ref[src, :], h=H_C).astype(BF16)
        vd_ref[:, dst, :] = pltpu.einshape("(ph)d->hpd", vc_ref[src, :], h=H_C).astype(BF16)
    fq_cache = fnc_ref[...] + fec_ref[...]
    for h in range(H_C):
        hs = slice(h * HD_C, (h + 1) * HD_C)
        s = lax.dot_general(q_ref[:, hs], kd_ref[h], contract_last, preferred_element_type=F32)
        s_ref[head_rows(h), :] = s / math.sqrt(HD_C) + fq_cache[head_rows(h)] - fct_ref[h:h + 1, :]
    update(tk)
    for h in range(H_C):
        acc_ref[head_rows(h), :] += jnp.dot(p_ref[head_rows(h), :], vd_ref[h], preferred_element_type=F32)

    @pl.when(j == pl.num_programs(1) - 1)
    def _():
        causal = (lax.broadcasted_iota(jnp.int32, (t, t), 1)
                  <= lax.broadcasted_iota(jnp.int32, (t, t), 0))
        fq_new = fnc_ref[...]
        for h in range(H_C):
            hs = slice(h * HD_C, (h + 1) * HD_C)
            s = lax.dot_general(q_ref[:, hs], kn_ref[:, hs].astype(BF16), contract_last,
                                preferred_element_type=F32)
            s = s / math.sqrt(HD_C) + fq_new[head_rows(h)] - fnt_ref[h:h + 1, :]
            s_ref[head_rows(h), 0:t] = jnp.where(causal, s, NEG)
        update(t)
        for h in range(H_C):
            hs = slice(h * HD_C, (h + 1) * HD_C)
            pv = jnp.dot(p_ref[head_rows(h), 0:t], vn_ref[:, hs].astype(BF16), preferred_element_type=F32)
            o_ref[:, hs] = ((acc_ref[head_rows(h), :] + pv) / l_ref[head_rows(h), :]).astype(o_ref.dtype)


def _fox_sample(q, k_new, v_new, k_cache, v_cache, layer, fc_heads, fn_heads):
    b, t, d = q.shape
    n_layers, _, p = k_cache.shape[:3]
    tk = _tile(p, 1024)
    rows = H_C * t
    fn_col = fn_heads.reshape(b, rows, 1)
    fc_end_col = jnp.broadcast_to(fc_heads[:, :, p - 1:], (b, H_C, t)).reshape(b, rows, 1)
    cache_spec = pl.BlockSpec((None, None, tk * H_C, HD_C), lambda bi, j: (layer, bi, j, 0))
    new_spec = pl.BlockSpec((None, t, d), lambda bi, j: (bi, 0, 0))
    col_spec = pl.BlockSpec((None, rows, 1), lambda bi, j: (bi, 0, 0))
    return pl.pallas_call(
        functools.partial(_fox_sample_kernel, t=t, tk=tk),
        name="fox_sample",
        grid=(b, p // tk),
        in_specs=[new_spec, cache_spec, cache_spec, new_spec, new_spec,
                  pl.BlockSpec((None, H_C, tk), lambda bi, j: (bi, 0, j)),
                  col_spec, col_spec,
                  pl.BlockSpec((None, H_C, t), lambda bi, j: (bi, 0, 0))],
        out_specs=new_spec,
        out_shape=jax.ShapeDtypeStruct((b, t, d), BF16),
        scratch_shapes=[pltpu.VMEM((rows, tk), F32), pltpu.VMEM((rows, tk), BF16),
                        pltpu.VMEM((rows, LANES), F32), pltpu.VMEM((rows, LANES), F32),
                        pltpu.VMEM((rows, HD_C), F32),
                        pltpu.VMEM((H_C, tk, HD_C), BF16), pltpu.VMEM((H_C, tk, HD_C), BF16)],
        compiler_params=_params("arbitrary", "arbitrary"),
    )(q, k_cache.reshape(n_layers, b, p * H_C, HD_C), v_cache.reshape(n_layers, b, p * H_C, HD_C),
      k_new, v_new, fc_heads, fn_col, fc_end_col, fn_heads)


def kernel(x_prompt, x_sample, c_prompt, c_sample, cache_conv, cache_win_k, cache_win_v, cache_fox_k, cache_fox_v, cache_fox_logf, ada_w, ada_b, norm_mix_g, norm_ffn_g, e_w_in, e_w_dw, e_b_dw, e_ln_g, e_ln_b, e_sinks, e_w_out, o_w_in, o_b_f, o_w_out, ffn_w_gate, ffn_w_up, ffn_w_down, final_g):
    bp, tp, d = x_prompt.shape
    bs, ts, _ = x_sample.shape
    depth = ada_w.shape[0]
    n_odd = o_w_in.shape[0]
    past_len = cache_fox_k.shape[2]
    nqkv = H_C * HD_C

    mod = _adaln(jnp.concatenate([c_prompt, c_sample], axis=0), ada_w, ada_b)
    tab_p = _rope_tables(jnp.arange(tp))
    tab_s = tuple(jnp.tile(x, (bs, 1)) for x in _rope_tables(past_len + jnp.arange(ts)))

    xp = x_prompt.reshape(bp * tp, d)
    xs = x_sample.reshape(bs * ts, d)
    hp = _modulate(x_prompt, norm_mix_g[0], mod, 0, 0, 0).reshape(bp * tp, d)
    hs = _modulate(x_sample, norm_mix_g[0], mod, 0, 0, bp).reshape(bs * ts, d)
    outs = {name: [] for name in ("conv_p", "wk_p", "wv_p", "fl_p",
                                  "conv_s", "wk_s", "wv_s", "fk_s", "fv_s", "fl_s")}
    fk_p = fv_p = None
    e_in, e_out = e_w_in.astype(BF16), e_w_out.astype(BF16)
    o_in, o_out = o_w_in.astype(BF16), o_w_out.astype(BF16)
    wg, wu, wd = ffn_w_gate.astype(BF16), ffn_w_up.astype(BF16), ffn_w_down.astype(BF16)
    for l in range(depth):
        if l % 2 == 0:
            e = l // 2
            w_out, w_out_layer = e_out, e
            conv = (e_w_dw[e], e_b_dw[e], e_ln_g[e], e_ln_b[e])
            zp = _even_proj(hp, e_in, e, tab_p, tp).reshape(bp, tp, E_IN)
            cat, cst = _even_mix(zp, e_sinks[e], *conv)
            win = min(WINDOW, tp)
            outs["conv_p"].append(cst[:, HIST_PAD - (CONV_W - 1):])
            outs["wk_p"].append(zp[:, tp - win:, K_OFF:V_OFF].reshape(bp, win, KV_B, HD_B))
            outs["wv_p"].append(zp[:, tp - win:, V_OFF:].reshape(bp, win, KV_B, HD_B))
            yp = cat.reshape(bp * tp, d)

            zs = _even_proj(hs, e_in, e, tab_s, bs * ts).reshape(bs, ts, E_IN)
            hist = jnp.pad(cache_conv[e], ((0, 0), (HIST_PAD - (CONV_W - 1), 0), (0, 0)))
            cat, cst = _even_mix_sample(zs, hist, cache_win_k[e], cache_win_v[e], e_sinks[e], *conv)
            outs["conv_s"].append(cst[:, HIST_PAD - (CONV_W - 1):])
            outs["wk_s"].append(zs[:, :, K_OFF:V_OFF].reshape(bs, ts, KV_B, HD_B))
            outs["wv_s"].append(zs[:, :, V_OFF:].reshape(bs, ts, KV_B, HD_B))
            ys = cat.reshape(bs * ts, d)
        else:
            o = l // 2
            wf = jnp.pad(o_w_in[o][:, 3 * nqkv:], ((0, 0), (0, LANES - H_C))).astype(BF16)[None]
            bf = jnp.pad(o_b_f[o], (0, LANES - H_C)).reshape(1, LANES)
            w_out, w_out_layer = o_out, o

            q = _matmul(hp, o_in, o, 0, nqkv, BF16).reshape(bp, tp, nqkv)
            k, fk_p = _kv_proj(hp, o_in, o, nqkv, fk_p, o, n_odd)
            v, fv_p = _kv_proj(hp, o_in, o, 2 * nqkv, fv_p, o, n_odd)
            logit = _matmul(hp, wf, 0, 0, LANES, F32).reshape(bp, tp, LANES)
            lf, f_rows, f_heads = _forget_scan(logit, bf, True)
            outs["fl_p"].append(lf)
            yp = _fox_prompt(q, k.reshape(bp, tp, nqkv), v.reshape(bp, tp, nqkv),
                             f_rows, f_heads).reshape(bp * tp, d)

            q = _matmul(hs, o_in, o, 0, nqkv, BF16).reshape(bs, ts, nqkv)
            k = _matmul(hs, o_in, o, nqkv, nqkv, F32).reshape(bs, ts, nqkv)
            v = _matmul(hs, o_in, o, 2 * nqkv, nqkv, F32).reshape(bs, ts, nqkv)
            logit = _matmul(hs, wf, 0, 0, LANES, F32).reshape(bs, ts, LANES)
            lf, _, fn_heads = _forget_scan(logit, bf, True)
            cache_lf = jnp.pad(cache_fox_logf[o], ((0, 0), (0, 0), (0, LANES - H_C)))
            _, _, fc_heads = _forget_scan(cache_lf, bf, False)
            outs["fk_s"].append(k.reshape(bs, ts, H_C, HD_C))
            outs["fv_s"].append(v.reshape(bs, ts, H_C, HD_C))
            outs["fl_s"].append(lf)
            ys = _fox_sample(q, k, v, cache_fox_k, cache_fox_v, o, fc_heads,
                             fn_heads).reshape(bs * ts, d)

        xp, hp = _residual_matmul(yp, w_out, w_out_layer, xp, norm_ffn_g[l], mod, l, tp, 0)
        xs, hs = _residual_matmul(ys, w_out, w_out_layer, xs, norm_ffn_g[l], mod, l, ts, bp)

        if l + 1 < depth:
            xp, hp = _ffn(hp, xp, wg, wu, wd, norm_mix_g[l + 1], mod, l, tp, 0, False)
            xs, hs = _ffn(hs, xs, wg, wu, wd, norm_mix_g[l + 1], mod, l, ts, bp, False)
        else:
            y_prompt = _ffn(hp, xp, wg, wu, wd, final_g, mod, l, tp, 0, True).reshape(bp, tp, d)
            y_sample = _ffn(hs, xs, wg, wu, wd, final_g, mod, l, ts, bp, True).reshape(bs, ts, d)

    stack = lambda name: jnp.stack(outs[name])
    return (y_prompt, y_sample,
            stack("conv_p"), stack("wk_p"), stack("wv_p"),
            fk_p.reshape(n_odd, bp, tp, H_C, HD_C), fv_p.reshape(n_odd, bp, tp, H_C, HD_C), stack("fl_p"),
            stack("conv_s"), stack("wk_s"), stack("wv_s"), stack("fk_s"), stack("fv_s"), stack("fl_s"))
```

```python
import functools
import math

import jax
import jax.numpy as jnp
from jax import lax
from jax.experimental import pallas as pl
from jax.experimental.pallas import tpu as pltpu

F32 = jnp.float32
BF16 = jnp.bfloat16

D_MODEL = 2048
CHUNK = 64
C_CONV = D_MODEL // 2
CONV_W = 31
HIST_PAD = 32
HD_B = 64
H_B = (D_MODEL // 2) // HD_B
KV_B = H_B // 4
G_B = H_B // KV_B
WINDOW = 128
E_IN = 2 * C_CONV + H_B * HD_B + 2 * KV_B * HD_B
Q_OFF = 2 * C_CONV
K_OFF = Q_OFF + H_B * HD_B
V_OFF = K_OFF + KV_B * HD_B
HD_C = 128
H_C = D_MODEL // HD_C
ROPE_THETA = 10000.0
EPS = 1e-6
NEG = -1e30
LANES = 128
SUBLANES = 8
VMEM_LIMIT = 56 * 1024 * 1024


def _params(*sem):
    return pltpu.CompilerParams(dimension_semantics=sem, vmem_limit_bytes=VMEM_LIMIT)


def _tile(n, pref):
    t = min(n, pref)
    while n % t:
        t //= 2
    return t


def _bcast_rows(vec, rows):
    nb, _, w = vec.shape
    if nb == 1:
        return vec[0]
    return jnp.broadcast_to(vec, (nb, rows // nb, w)).reshape(rows, w)


def _mod_spec(layer, chunk, nb, width, index_fn):
    return pl.BlockSpec((None, None, nb, 1, width),
                        lambda *g: (layer, chunk) + tuple(index_fn(*g)))


def _row_tiling(m, rows_per_batch, pref):
    if m > rows_per_batch and m > pref:
        tm = _tile(rows_per_batch, pref)
        return tm, 1, lambda i, row_off: (i * tm) // rows_per_batch + row_off
    nb = max(m // rows_per_batch, 1)
    return m, nb, lambda i, row_off: row_off // nb + i


def _rms_scale(x):
    return lax.rsqrt(jnp.mean(x * x, axis=-1, keepdims=True) + EPS)


def _adaln_kernel(c_ref, w_ref, b_ref, o_ref):
    c = c_ref[...]
    a = (c * jax.nn.sigmoid(c)).astype(BF16)
    o_ref[...] = jnp.dot(a, w_ref[...].astype(BF16), preferred_element_type=F32) + b_ref[...]


def _adaln(c_all, ada_w, ada_b):
    n_layers, d, n = ada_w.shape
    nb = c_all.shape[0]
    tn = 1024
    per = d // tn
    out = pl.pallas_call(
        _adaln_kernel,
        name="adaln",
        grid=(n_layers, n // tn),
        in_specs=[pl.BlockSpec((nb, d), lambda l, j: (0, 0)),
                  pl.BlockSpec((None, d, tn), lambda l, j: (l, 0, j)),
                  pl.BlockSpec((None, 1, tn), lambda l, j: (l, 0, j))],
        out_specs=pl.BlockSpec((None, None, nb, tn), lambda l, j: (l, j // per, 0, j % per)),
        out_shape=jax.ShapeDtypeStruct((n_layers, n // d, nb, d), F32),
        compiler_params=_params("arbitrary", "arbitrary"),
    )(c_all, ada_w, ada_b.reshape(n_layers, 1, n))
    return out.reshape(n_layers, n // d, nb, 1, d)


def _modulate_kernel(x_ref, g_ref, sh_ref, sc_ref, o_ref):
    x = x_ref[...]
    y = x * _rms_scale(x) * (g_ref[...] * (1.0 + sc_ref[0]))
    o_ref[...] = (y + sh_ref[0]).astype(o_ref.dtype)


def _modulate(x, g, mod, layer, chunk, row_off):
    b, t, d = x.shape
    tt = _tile(t, 512)
    return pl.pallas_call(
        _modulate_kernel,
        name="modulate",
        grid=(b, t // tt),
        in_specs=[pl.BlockSpec((None, tt, d), lambda bi, i: (bi, i, 0)),
                  pl.BlockSpec((1, d), lambda bi, i: (0, 0)),
                  _mod_spec(layer, chunk, 1, d, lambda bi, i: (bi + row_off, 0, 0)),
                  _mod_spec(layer, chunk + 1, 1, d, lambda bi, i: (bi + row_off, 0, 0))],
        out_specs=pl.BlockSpec((None, tt, d), lambda bi, i: (bi, i, 0)),
        out_shape=jax.ShapeDtypeStruct((b, t, d), BF16),
        compiler_params=_params("arbitrary", "arbitrary"),
    )(x, g.reshape(1, d), mod, mod)


def _mm_kernel(a_ref, w_ref, o_ref):
    o_ref[...] = jnp.dot(a_ref[...], w_ref[...], preferred_element_type=F32).astype(o_ref.dtype)


def _matmul(a, w, layer, col0, n, out_dtype):
    m, k = a.shape
    tm, tn = _tile(m, 1024), _tile(n, 1024)
    assert col0 % tn == 0
    return pl.pallas_call(
        _mm_kernel,
        name="matmul",
        grid=(m // tm, n // tn),
        in_specs=[pl.BlockSpec((tm, k), lambda i, j: (i, 0)),
                  pl.BlockSpec((None, k, tn), lambda i, j: (layer, 0, col0 // tn + j))],
        out_specs=pl.BlockSpec((tm, tn), lambda i, j: (i, j)),
        out_shape=jax.ShapeDtypeStruct((m, n), out_dtype),
        compiler_params=_params("arbitrary", "arbitrary"),
    )(a, w)


def _kv_proj_kernel(a_ref, w_ref, *refs, tm, tn):
    rows_ref, heads_ref, y_ref = refs[-3:]
    a = a_ref[...]
    for j in range(w_ref.shape[1] // tn):
        y = jnp.dot(a, w_ref[:, j * tn:(j + 1) * tn], preferred_element_type=F32)
        rows_ref[:, j * tn:(j + 1) * tn] = y.astype(rows_ref.dtype)
        y_ref[:, j * tn:(j + 1) * tn] = y
    step = 8 * SUBLANES
    for r in range(0, tm, step):
        heads_ref[r * H_C:(r + step) * H_C, :] = pltpu.einshape(
            "t(hd)->(th)d", y_ref[r:r + step, :], h=H_C)


def _kv_proj(a, w, w_layer, col0, stacked, layer, n_layers):
    m, k = a.shape
    n = H_C * HD_C
    assert col0 % n == 0
    tm = _tile(m, 512)
    in_specs = [pl.BlockSpec((tm, k), lambda i: (i, 0)),
                pl.BlockSpec((None, k, n), lambda i: (w_layer, 0, col0 // n))]
    args = [a, w]
    aliases = {}
    if stacked is not None:
        in_specs.append(pl.BlockSpec(memory_space=pl.ANY))
        args.append(stacked)
        aliases = {2: 1}
    return pl.pallas_call(
        functools.partial(_kv_proj_kernel, tm=tm, tn=512),
        name="kv_proj",
        grid=(m // tm,),
        in_specs=in_specs,
        out_specs=[pl.BlockSpec((tm, n), lambda i: (i, 0)),
                   pl.BlockSpec((None, tm * H_C, HD_C), lambda i: (layer, i, 0))],
        out_shape=[jax.ShapeDtypeStruct((m, n), BF16),
                   jax.ShapeDtypeStruct((n_layers, m * H_C, HD_C), F32)],
        input_output_aliases=aliases,
        scratch_shapes=[pltpu.VMEM((tm, n), F32)],
        compiler_params=_params("arbitrary"),
    )(*args)


def _rope_tables(pos):
    half = HD_B // 2
    freqs = ROPE_THETA ** (-jnp.arange(half, dtype=F32) / half)
    ang = pos.astype(F32)[:, None] * freqs[None, :]
    cos, sin = jnp.cos(ang), jnp.sin(ang)
    zero = jnp.zeros_like(sin)
    rep = LANES // HD_B
    cos_t = jnp.tile(jnp.concatenate([cos, cos], axis=1), (1, rep))
    sa = jnp.tile(jnp.concatenate([-sin, zero], axis=1), (1, rep))
    sb = jnp.tile(jnp.concatenate([zero, sin], axis=1), (1, rep))
    return cos_t, sa, sb


def _rope(x, cos, sa, sb):
    half = HD_B // 2
    outs = []
    for g in range(x.shape[1] // LANES):
        xg = x[:, g * LANES:(g + 1) * LANES]
        outs.append(xg * cos + pltpu.roll(xg, LANES - half, 1) * sa + pltpu.roll(xg, half, 1) * sb)
    return outs[0] if len(outs) == 1 else jnp.concatenate(outs, axis=1)


def _even_proj_kernel(a_ref, w_ref, cos_ref, sa_ref, sb_ref, o_ref, *, tn):
    a = a_ref[...]
    kw = KV_B * HD_B
    for j in range(E_IN // tn):
        c0 = j * tn
        z = jnp.dot(a, w_ref[:, c0:c0 + tn], preferred_element_type=F32)
        if c0 + tn <= Q_OFF:
            o_ref[:, c0:c0 + tn] = z
        elif c0 + tn <= K_OFF:
            o_ref[:, c0:c0 + tn] = _rope(z, cos_ref[...], sa_ref[...], sb_ref[...])
        else:
            o_ref[:, c0:c0 + kw] = _rope(z[:, :kw], cos_ref[...], sa_ref[...], sb_ref[...])
            o_ref[:, c0 + kw:c0 + tn] = z[:, kw:]


def _even_proj(a, w, layer, tables, period):
    m, k = a.shape
    n = w.shape[2]
    tn = 2 * KV_B * HD_B
    assert Q_OFF % tn == 0 and K_OFF % tn == 0 and n == K_OFF + tn
    tm = _tile(period, 512) if m > period else m
    per_blocks = max(period // tm, 1)
    tab_spec = pl.BlockSpec((tm, LANES), lambda i: (i % per_blocks, 0))
    return pl.pallas_call(
        functools.partial(_even_proj_kernel, tn=tn),
        name="even_proj",
        grid=(m // tm,),
        in_specs=[pl.BlockSpec((tm, k), lambda i: (i, 0)),
                  pl.BlockSpec((None, k, n), lambda i: (layer, 0, 0), pipeline_mode=pl.Buffered(1)),
                  tab_spec, tab_spec, tab_spec],
        out_specs=pl.BlockSpec((tm, n), lambda i: (i, 0)),
        out_shape=jax.ShapeDtypeStruct((m, n), F32),
        compiler_params=_params("arbitrary"),
    )(a, w, *tables)


def _residual_kernel(a_ref, w_ref, x_ref, gate_ref, g_ref, sh_ref, sc_ref, xo_ref, ho_ref, *, tn):
    tm, d = xo_ref.shape
    a = a_ref[...]
    gate = _bcast_rows(gate_ref[...], tm)
    ss = jnp.zeros((tm, 1), F32)
    for j in range(d // tn):
        cs = slice(j * tn, (j + 1) * tn)
        xn = x_ref[:, cs] + gate[:, cs] * jnp.dot(a, w_ref[:, cs], preferred_element_type=F32)
        xo_ref[:, cs] = xn
        ss = ss + jnp.sum(xn * xn, axis=-1, keepdims=True)
    gs = _bcast_rows(g_ref[...] * (1.0 + sc_ref[...]), tm)
    h = xo_ref[...] * lax.rsqrt(ss / d + EPS) * gs + _bcast_rows(sh_ref[...], tm)
    ho_ref[...] = h.astype(ho_ref.dtype)


def _residual_matmul(a, w, w_layer, x, g, mod, layer, rows_per_batch, row_off):
    m, k = a.shape
    d = w.shape[2]
    tm, nb, row_fn = _row_tiling(m, rows_per_batch, 512)
    vec = lambda chunk: _mod_spec(layer, chunk, nb, d, lambda i: (row_fn(i, row_off), 0, 0))
    return pl.pallas_call(
        functools.partial(_residual_kernel, tn=512),
        name="residual",
        grid=(m // tm,),
        in_specs=[pl.BlockSpec((tm, k), lambda i: (i, 0)),
                  pl.BlockSpec((None, k, d), lambda i: (w_layer, 0, 0)),
                  pl.BlockSpec((tm, d), lambda i: (i, 0)),
                  vec(2),
                  pl.BlockSpec((1, 1, d), lambda i: (0, 0, 0)),
                  vec(3), vec(4)],
        out_specs=[pl.BlockSpec((tm, d), lambda i: (i, 0)),
                   pl.BlockSpec((tm, d), lambda i: (i, 0))],
        out_shape=[jax.ShapeDtypeStruct((m, d), F32), jax.ShapeDtypeStruct((m, d), BF16)],
        compiler_params=_params("arbitrary"),
    )(a, w, x, mod, g.reshape(1, 1, d), mod, mod)


def _ffn_kernel(h_ref, x_ref, gate_ref, wg_ref, wu_ref, wd_ref, g_ref, *refs, final, rows):
    if final:
        y_ref, acc_ref = refs
    else:
        sh_ref, sc_ref, xo_ref, ho_ref, acc_ref = refs
    f = pl.program_id(1)
    tm, d = acc_ref.shape

    @pl.when(f == 0)
    def _():
        acc_ref[...] = jnp.zeros_like(acc_ref)

    h = h_ref[...]
    g = jnp.dot(h, wg_ref[...], preferred_element_type=F32)
    u = jnp.dot(h, wu_ref[...], preferred_element_type=F32)
    a = (g * jax.nn.sigmoid(g) * u).astype(BF16)
    acc_ref[...] += jnp.dot(a, wd_ref[...], preferred_element_type=F32)

    @pl.when(f == pl.num_programs(1) - 1)
    def _():
        nb = gate_ref.shape[0]
        per = tm // nb
        for r0 in range(0, tm, rows):
            rs = slice(r0, r0 + rows)
            b = r0 // per
            xn = x_ref[rs, :] + gate_ref[b] * acc_ref[rs, :]
            if final:
                y_ref[rs, :] = xn * _rms_scale(xn) * g_ref[0]
            else:
                xo_ref[rs, :] = xn
                hn = xn * _rms_scale(xn) * (g_ref[0] * (1.0 + sc_ref[b])) + sh_ref[b]
                ho_ref[rs, :] = hn.astype(ho_ref.dtype)


def _ffn(h, x, wg, wu, wd, g_next, mod, layer, rows_per_batch, row_off, final):
    m, d = h.shape
    ff = wg.shape[2]
    tf = _tile(ff, 512)
    tm, nb, row_fn = _row_tiling(m, rows_per_batch, 512)
    vec = lambda lyr, chunk: _mod_spec(lyr, chunk, nb, d, lambda i, f: (row_fn(i, row_off), 0, 0))
    rows_spec = pl.BlockSpec((tm, d), lambda i, f: (i, 0))
    in_specs = [rows_spec, rows_spec, vec(layer, 5),
                pl.BlockSpec((None, d, tf), lambda i, f: (layer, 0, f)),
                pl.BlockSpec((None, d, tf), lambda i, f: (layer, 0, f)),
                pl.BlockSpec((None, tf, d), lambda i, f: (layer, f, 0)),
                pl.BlockSpec((1, 1, d), lambda i, f: (0, 0, 0))]
    args = [h, x, mod, wg, wu, wd, g_next.reshape(1, 1, d)]
    if final:
        out_specs = rows_spec
        out_shape = jax.ShapeDtypeStruct((m, d), F32)
    else:
        in_specs += [vec(layer + 1, 0), vec(layer + 1, 1)]
        args += [mod, mod]
        out_specs = [rows_spec, rows_spec]
        out_shape = [jax.ShapeDtypeStruct((m, d), F32), jax.ShapeDtypeStruct((m, d), BF16)]
    return pl.pallas_call(
        functools.partial(_ffn_kernel, final=final, rows=2 * SUBLANES),
        name="ffn",
        grid=(m // tm, ff // tf),
        in_specs=in_specs,
        out_specs=out_specs,
        out_shape=out_shape,
        scratch_shapes=[pltpu.VMEM((tm, d), F32)],
        compiler_params=_params("arbitrary", "arbitrary"),
    )(*args)


def _glu_rows(z_ref, uext_ref, rows):
    step = min(rows, 4 * SUBLANES)
    for r in range(0, rows, step):
        uext_ref[HIST_PAD + r:HIST_PAD + r + step, :] = (
            z_ref[r:r + step, 0:C_CONV] * jax.nn.sigmoid(z_ref[r:r + step, C_CONV:2 * C_CONV]))


def _conv_ln_silu(uext_ref, wdw_ref, bdw_ref, lng_ref, lnb_ref, y_ref, shift_ref, out_ref, rows):
    base = HIST_PAD - (CONV_W - 1)
    for c in range(C_CONV // LANES):
        cs = slice(c * LANES, (c + 1) * LANES)
        for r in range(SUBLANES):
            n = rows + SUBLANES * ((CONV_W - 1 - r) // SUBLANES)
            shift_ref[r, 0:n, :] = uext_ref[base + r:base + r + n, cs]
        rb = min(rows, 8 * SUBLANES)

        def row_block(i, carry):
            t0 = pl.multiple_of(i * rb, rb)
            acc = jnp.zeros((rb, LANES), F32)
            for j in range(CONV_W):
                r, a = j % SUBLANES, j // SUBLANES
                acc = acc + wdw_ref[j:j + 1, cs] * shift_ref[r, pl.ds(t0 + SUBLANES * a, rb), :]
            y_ref[pl.ds(t0, rb), cs] = acc + bdw_ref[:, cs]
            return carry

        lax.fori_loop(0, rows // rb, row_block, 0)
    step = min(rows, 4 * SUBLANES)
    for r in range(0, rows, step):
        y = y_ref[r:r + step, :]
        yc = y - jnp.mean(y, axis=-1, keepdims=True)
        var = jnp.mean(yc * yc, axis=-1, keepdims=True)
        yn = yc * lax.rsqrt(var + EPS) * lng_ref[...] + lnb_ref[...]
        out_ref[r:r + step, 0:C_CONV] = (yn * jax.nn.sigmoid(yn)).astype(out_ref.dtype)


def _sink_attention(q, k, v, sink_col, mask):
    s = lax.dot_general(q, k, (((1,), (1,)), ((), ())), preferred_element_type=F32) / math.sqrt(HD_B)
    if mask is not None:
        s = jnp.where(mask, s, NEG)
    m = jnp.maximum(jnp.max(s, axis=-1, keepdims=True), sink_col)
    p = jnp.exp(s - m)
    den = jnp.sum(p, axis=-1, keepdims=True) + jnp.exp(sink_col - m)
    return jnp.dot(p.astype(BF16), v, preferred_element_type=F32) / den


def _sink_column(sink_ref, kv_head, rows):
    return jnp.concatenate(
        [jnp.full((rows, 1), sink_ref[kv_head * G_B + g], F32) for g in range(G_B)], axis=0)


def _group_attention(sink_ref, z_ref, kall, vall, r0, rows, mask, cat_ref):
    for kh in range(KV_B):
        hs = slice(kh * HD_B, (kh + 1) * HD_B)
        q = jnp.concatenate(
            [z_ref[r0:r0 + rows, Q_OFF + (kh * G_B + g) * HD_B:Q_OFF + (kh * G_B + g + 1) * HD_B]
             for g in range(G_B)], axis=0).astype(BF16)
        o = _sink_attention(q, kall[:, hs], vall[:, hs], _sink_column(sink_ref, kh, rows), mask)
        for gp in range(G_B // 2):
            pair = jnp.concatenate([o[(2 * gp) * rows:(2 * gp + 1) * rows],
                                    o[(2 * gp + 1) * rows:(2 * gp + 2) * rows]], axis=1)
            c0 = C_CONV + (kh * G_B + 2 * gp) * HD_B
            cat_ref[r0:r0 + rows, c0:c0 + 2 * HD_B] = pair.astype(cat_ref.dtype)


def _even_mix_kernel(sink_ref, zc_ref, zh_ref, zkv_ref, wdw_ref, bdw_ref, lng_ref, lnb_ref,
                     cat_ref, cst_ref, uext_ref, y_ref, shift_ref, *, tt):
    i = pl.program_id(1)
    first = i == 0

    _glu_rows(zc_ref, uext_ref, tt)
    hist = zh_ref[:, 0:C_CONV] * jax.nn.sigmoid(zh_ref[:, C_CONV:2 * C_CONV])
    uext_ref[0:HIST_PAD, :] = jnp.where(first, 0.0, hist)
    _conv_ln_silu(uext_ref, wdw_ref, bdw_ref, lng_ref, lnb_ref, y_ref, shift_ref, cat_ref, tt)

    @pl.when(i == pl.num_programs(1) - 1)
    def _():
        cst_ref[...] = uext_ref[tt:tt + HIST_PAD, :]

    kw = KV_B * HD_B
    sub = 2 * CHUNK
    kext = jnp.concatenate([zkv_ref[:, 0:kw], zc_ref[:, K_OFF:K_OFF + kw]], axis=0).astype(BF16)
    v_t = jnp.concatenate([zkv_ref[:, kw:2 * kw], zc_ref[:, V_OFF:V_OFF + kw]], axis=0).T.astype(BF16)
    cols = G_B * sub
    kc = lax.broadcasted_iota(jnp.int32, (2 * sub, cols), 0) // CHUNK
    qc = (lax.broadcasted_iota(jnp.int32, (2 * sub, cols), 1) % sub) // CHUNK
    band = jnp.logical_and(kc >= qc, kc <= qc + WINDOW // CHUNK)
    band_first = jnp.logical_and(band, jnp.logical_or(jnp.logical_not(first), kc >= WINDOW // CHUNK))
    for s in range(tt // sub):
        r0 = s * sub
        mask = band_first if s == 0 else band
        for kh in range(KV_B):
            hs = slice(kh * HD_B, (kh + 1) * HD_B)
            q = jnp.concatenate(
                [zc_ref[r0:r0 + sub, Q_OFF + (kh * G_B + g) * HD_B:Q_OFF + (kh * G_B + g + 1) * HD_B]
                 for g in range(G_B)], axis=0).astype(BF16)
            logit = lax.dot_general(kext[r0:r0 + 2 * sub, hs], q, (((1,), (1,)), ((), ())),
                                    preferred_element_type=F32) / math.sqrt(HD_B)
            logit = jnp.where(mask, logit, NEG)
            sink = jnp.concatenate(
                [jnp.full((1, sub), sink_ref[kh * G_B + g], F32) for g in range(G_B)], axis=1)
            m = jnp.maximum(jnp.max(logit, axis=0, keepdims=True), sink)
            p = jnp.exp(logit - m)
            den = jnp.sum(p, axis=0, keepdims=True) + jnp.exp(sink - m)
            o_t = jnp.dot(v_t[hs, r0:r0 + 2 * sub], p.astype(BF16), preferred_element_type=F32) / den
            for gp in range(G_B // 2):
                pair_t = jnp.concatenate([o_t[:, (2 * gp) * sub:(2 * gp + 1) * sub],
                                          o_t[:, (2 * gp + 1) * sub:(2 * gp + 2) * sub]], axis=0)
                c0 = C_CONV + (kh * G_B + 2 * gp) * HD_B
                cat_ref[r0:r0 + sub, c0:c0 + 2 * HD_B] = pair_t.T.astype(cat_ref.dtype)


def _even_mix(z, sinks, w_dw, b_dw, ln_g, ln_b):
    b, t, _ = z.shape
    tt = _tile(t, 512)
    hb = tt // HIST_PAD
    kb = tt // WINDOW
    kvw = 2 * KV_B * HD_B
    vec = pl.BlockSpec((1, C_CONV), lambda bi, i: (0, 0))
    return pl.pallas_call(
        functools.partial(_even_mix_kernel, tt=tt),
        name="even_mix",
        grid=(b, t // tt),
        in_specs=[pl.BlockSpec(memory_space=pltpu.SMEM),
                  pl.BlockSpec((None, tt, E_IN), lambda bi, i: (bi, i, 0)),
                  pl.BlockSpec((None, HIST_PAD, 2 * C_CONV),
                               lambda bi, i: (bi, jnp.maximum(i * hb - 1, 0), 0)),
                  pl.BlockSpec((None, WINDOW, kvw),
                               lambda bi, i: (bi, jnp.maximum(i * kb - 1, 0), K_OFF // kvw)),
                  pl.BlockSpec((CONV_W, C_CONV), lambda bi, i: (0, 0)),
                  vec, vec, vec],
        out_specs=[pl.BlockSpec((None, tt, D_MODEL), lambda bi, i: (bi, i, 0)),
                   pl.BlockSpec((None, HIST_PAD, C_CONV), lambda bi, i: (bi, 0, 0))],
        out_shape=[jax.ShapeDtypeStruct((b, t, D_MODEL), BF16),
                   jax.ShapeDtypeStruct((b, HIST_PAD, C_CONV), F32)],
        scratch_shapes=[pltpu.VMEM((HIST_PAD + tt, C_CONV), F32),
                        pltpu.VMEM((tt, C_CONV), F32),
                        pltpu.VMEM((SUBLANES, tt + HIST_PAD, LANES), F32)],
        compiler_params=_params("arbitrary", "arbitrary"),
    )(sinks, z, z, z, w_dw, b_dw.reshape(1, -1), ln_g.reshape(1, -1), ln_b.reshape(1, -1))


def _even_mix_sample_kernel(sink_ref, z_ref, hist_ref, kc_ref, vc_ref, wdw_ref, bdw_ref, lng_ref,
                            lnb_ref, cat_ref, cst_ref, uext_ref, y_ref, shift_ref, *, t):
    uext_ref[0:HIST_PAD, :] = hist_ref[...]
    _glu_rows(z_ref, uext_ref, t)
    _conv_ln_silu(uext_ref, wdw_ref, bdw_ref, lng_ref, lnb_ref, y_ref, shift_ref, cat_ref, t)
    cst_ref[...] = uext_ref[t:t + HIST_PAD, :]

    kw = KV_B * HD_B
    kall = jnp.concatenate([kc_ref[...], z_ref[:, K_OFF:K_OFF + kw]], axis=0).astype(BF16)
    vall = jnp.concatenate([vc_ref[...], z_ref[:, V_OFF:V_OFF + kw]], axis=0).astype(BF16)
    _group_attention(sink_ref, z_ref, kall, vall, 0, t, None, cat_ref)


def _even_mix_sample(z, hist, k_cache, v_cache, sinks, w_dw, b_dw, ln_g, ln_b):
    b, t, _ = z.shape
    win = k_cache.shape[1]
    kw = KV_B * HD_B
    vec = pl.BlockSpec((1, C_CONV), lambda bi: (0, 0))
    return pl.pallas_call(
        functools.partial(_even_mix_sample_kernel, t=t),
        name="even_mix_sample",
        grid=(b,),
        in_specs=[pl.BlockSpec(memory_space=pltpu.SMEM),
                  pl.BlockSpec((None, t, E_IN), lambda bi: (bi, 0, 0)),
                  pl.BlockSpec((None, HIST_PAD, C_CONV), lambda bi: (bi, 0, 0)),
                  pl.BlockSpec((None, win, kw), lambda bi: (bi, 0, 0)),
                  pl.BlockSpec((None, win, kw), lambda bi: (bi, 0, 0)),
                  pl.BlockSpec((CONV_W, C_CONV), lambda bi: (0, 0)),
                  vec, vec, vec],
        out_specs=[pl.BlockSpec((None, t, D_MODEL), lambda bi: (bi, 0, 0)),
                   pl.BlockSpec((None, HIST_PAD, C_CONV), lambda bi: (bi, 0, 0))],
        out_shape=[jax.ShapeDtypeStruct((b, t, D_MODEL), BF16),
                   jax.ShapeDtypeStruct((b, HIST_PAD, C_CONV), F32)],
        scratch_shapes=[pltpu.VMEM((HIST_PAD + t, C_CONV), F32),
                        pltpu.VMEM((t, C_CONV), F32),
                        pltpu.VMEM((SUBLANES, t + HIST_PAD, LANES), F32)],
        compiler_params=_params("arbitrary"),
    )(sinks, z, hist, k_cache.reshape(b, win, kw), v_cache.reshape(b, win, kw),
      w_dw, b_dw.reshape(1, -1), ln_g.reshape(1, -1), ln_b.reshape(1, -1))


def _scan_kernel(x_ref, b_ref, lf_ref, f_ref, ft_ref, carry_ref, *, tt, log_sigmoid):
    i = pl.program_id(1)

    @pl.when(i == 0)
    def _():
        carry_ref[...] = jnp.zeros_like(carry_ref)

    x = x_ref[...]
    if log_sigmoid:
        x = x + b_ref[...]
        x = jnp.minimum(x, 0.0) - jnp.log1p(jnp.exp(-jnp.abs(x)))
    lf_ref[...] = x[:, :H_C]
    row = lax.broadcasted_iota(jnp.int32, x.shape, 0)
    step = 1
    while step < tt:
        x = x + jnp.where(row >= step, pltpu.roll(x, step, 0), 0.0)
        step *= 2
    f = x + carry_ref[...]
    carry_ref[...] = f[tt - 1:tt, :]
    f_ref[...] = f
    if tt < LANES:
        f = jnp.concatenate([f, jnp.zeros((LANES - tt, LANES), F32)], axis=0)
    ft_ref[...] = f.T[:H_C, :tt]


def _forget_scan(x, bias, log_sigmoid):
    b, t, _ = x.shape
    tt = _tile(t, 512)
    return pl.pallas_call(
        functools.partial(_scan_kernel, tt=tt, log_sigmoid=log_sigmoid),
        name="forget_scan",
        grid=(b, t // tt),
        in_specs=[pl.BlockSpec((None, tt, LANES), lambda bi, i: (bi, i, 0)),
                  pl.BlockSpec((1, LANES), lambda bi, i: (0, 0))],
        out_specs=[pl.BlockSpec((None, tt, H_C), lambda bi, i: (bi, i, 0)),
                   pl.BlockSpec((None, tt, LANES), lambda bi, i: (bi, i, 0)),
                   pl.BlockSpec((None, H_C, tt), lambda bi, i: (bi, 0, i))],
        out_shape=[jax.ShapeDtypeStruct((b, t, H_C), F32),
                   jax.ShapeDtypeStruct((b, t, LANES), F32),
                   jax.ShapeDtypeStruct((b, H_C, t), F32)],
        scratch_shapes=[pltpu.VMEM((1, LANES), F32)],
        compiler_params=_params("arbitrary", "arbitrary"),
    )(x, bias)


def _lane_fold(x, op):
    out = x[:, 0:LANES]
    for c in range(1, x.shape[1] // LANES):
        out = op(out, x[:, c * LANES:(c + 1) * LANES])
    return out


def _fox_kernel(q_ref, k_ref, v_ref, fq_ref, fk_ref, o_ref, s_ref, vb_ref, *, tq, t):
    h = pl.program_id(1)
    vb_ref[:, 0:HD_C] = v_ref[...]
    vb_ref[:, HD_C:] = jnp.ones((t, HD_C), BF16)
    lane = lax.broadcasted_iota(jnp.int32, (tq, LANES), 1)
    causal = (lax.broadcasted_iota(jnp.int32, (tq, tq), 1)
              <= lax.broadcasted_iota(jnp.int32, (tq, tq), 0))
    for qi in reversed(range(t // tq)):
        rows = slice(qi * tq, (qi + 1) * tq)
        base = (qi * (qi + 1)) // 2
        q = q_ref[rows, :]
        fq = jnp.sum(jnp.where(lane == h, fq_ref[rows, :], 0.0), axis=-1, keepdims=True)
        mx = None
        for j in range(qi + 1):
            ks = slice(j * tq, (j + 1) * tq)
            s = lax.dot_general(q, k_ref[ks, :], (((1,), (1,)), ((), ())), preferred_element_type=F32)
            s = s / math.sqrt(HD_C) + fq - fk_ref[:, ks]
            if j == qi:
                s = jnp.where(causal, s, NEG)
            s_ref[base + j] = s
            fold = _lane_fold(s, jnp.maximum)
            mx = fold if mx is None else jnp.maximum(mx, fold)
        m = jnp.max(mx, axis=-1, keepdims=True)
        acc = None
        for j in range(qi + 1):
            p = jnp.exp(s_ref[base + j] - m).astype(BF16)
            pv = jnp.dot(p, vb_ref[j * tq:(j + 1) * tq, :], preferred_element_type=F32)
            acc = pv if acc is None else acc + pv
        o_ref[rows, :] = (acc[:, 0:HD_C] / acc[:, HD_C:]).astype(o_ref.dtype)


def _fox_prompt(q, k, v, f_rows, f_heads):
    b, t, _ = q.shape
    tq = _tile(t, 512)
    nq = t // tq
    head_rows = pl.BlockSpec((None, t, HD_C), lambda bi, h: (bi, 0, h))
    return pl.pallas_call(
        functools.partial(_fox_kernel, tq=tq, t=t),
        name="fox_prompt",
        grid=(b, H_C),
        in_specs=[head_rows, head_rows, head_rows,
                  pl.BlockSpec((None, t, LANES), lambda bi, h: (bi, 0, 0)),
                  pl.BlockSpec((None, None, 1, t), lambda bi, h: (bi, h, 0, 0))],
        out_specs=head_rows,
        out_shape=jax.ShapeDtypeStruct((b, t, H_C * HD_C), BF16),
        scratch_shapes=[pltpu.VMEM(((nq * (nq + 1)) // 2, tq, tq), F32),
                        pltpu.VMEM((t, 2 * HD_C), BF16)],
        compiler_params=_params("arbitrary", "arbitrary"),
    )(q, k, v, f_rows, f_heads.reshape(b, H_C, 1, t))


def _fox_sample_kernel(q_ref, kc_ref, vc_ref, kn_ref, vn_ref, fct_ref, fnc_ref, fec_ref, fnt_ref,
                       o_ref, s_ref, p_ref, m_ref, l_ref, acc_ref, kd_ref, vd_ref, *, t, tk):
    j = pl.program_id(1)
    contract_last = (((1,), (1,)), ((), ()))

    @pl.when(j == 0)
    def _():
        m_ref[...] = jnp.full_like(m_ref, NEG)
        l_ref[...] = jnp.zeros_like(l_ref)
        acc_ref[...] = jnp.zeros_like(acc_ref)

    def head_rows(h):
        return slice(h * t, (h + 1) * t)

    def update(width):
        s = s_ref[:, 0:width]
        m_prev = m_ref[...]
        m_col = jnp.maximum(m_prev[:, 0:1], jnp.max(s, axis=-1, keepdims=True))
        m_new = jnp.broadcast_to(m_col, m_prev.shape)
        alpha = jnp.exp(m_prev - m_new)
        p = jnp.exp(s - m_col)
        l_ref[...] = alpha * l_ref[...] + jnp.sum(p, axis=-1, keepdims=True)
        acc_ref[...] = alpha * acc_ref[...]
        m_ref[...] = m_new
        p_ref[:, 0:width] = p.astype(BF16)

    step = LANES
    for c in range(tk // step):
        src = slice(c * step * H_C, (c + 1) * step * H_C)
        dst = slice(c * step, (c + 1) * step)
        kd_ref[:, dst, :] = pltpu.einshape("(ph)d->hpd", kc_ref[src, :], h=H_C).astype(BF16)
        vd_ref[:, dst, :] = pltpu.einshape("(ph)d->hpd", vc_ref[src, :], h=H_C).astype(BF16)
    fq_cache = fnc_ref[...] + fec_ref[...]
    for h in range(H_C):
        hs = slice(h * HD_C, (h + 1) * HD_C)
        s = lax.dot_general(q_ref[:, hs], kd_ref[h], contract_last, preferred_element_type=F32)
        s_ref[head_rows(h), :] = s / math.sqrt(HD_C) + fq_cache[head_rows(h)] - fct_ref[h:h + 1, :]
    update(tk)
    for h in range(H_C):
        acc_ref[head_rows(h), :] += jnp.dot(p_ref[head_rows(h), :], vd_ref[h], preferred_element_type=F32)

    @pl.when(j == pl.num_programs(1) - 1)
    def _():
        causal = (lax.broadcasted_iota(jnp.int32, (t, t), 1)
                  <= lax.broadcasted_iota(jnp.int32, (t, t), 0))
        fq_new = fnc_ref[...]
        for h in range(H_C):
            hs = slice(h * HD_C, (h + 1) * HD_C)
            s = lax.dot_general(q_ref[:, hs], kn_ref[:, hs].astype(BF16), contract_last,
                                preferred_element_type=F32)
            s = s / math.sqrt(HD_C) + fq_new[head_rows(h)] - fnt_ref[h:h + 1, :]
            s_ref[head_rows(h), 0:t] = jnp.where(causal, s, NEG)
        update(t)
        for h in range(H_C):
            hs = slice(h * HD_C, (h + 1) * HD_C)
            pv = jnp.dot(p_ref[head_rows(h), 0:t], vn_ref[:, hs].astype(BF16), preferred_element_type=F32)
            o_ref[:, hs] = ((acc_ref[head_rows(h), :] + pv) / l_ref[head_rows(h), :]).astype(o_ref.dtype)


def _fox_sample(q, k_new, v_new, k_cache, v_cache, layer, fc_heads, fn_heads):
    b, t, d = q.shape
    n_layers, _, p = k_cache.shape[:3]
    tk = _tile(p, 1024)
    rows = H_C * t
    fn_col = fn_heads.reshape(b, rows, 1)
    fc_end_col = jnp.broadcast_to(fc_heads[:, :, p - 1:], (b, H_C, t)).reshape(b, rows, 1)
    cache_spec = pl.BlockSpec((None, None, tk * H_C, HD_C), lambda bi, j: (layer, bi, j, 0))
    new_spec = pl.BlockSpec((None, t, d), lambda bi, j: (bi, 0, 0))
    col_spec = pl.BlockSpec((None, rows, 1), lambda bi, j: (bi, 0, 0))
    return pl.pallas_call(
        functools.partial(_fox_sample_kernel, t=t, tk=tk),
        name="fox_sample",
        grid=(b, p // tk),
        in_specs=[new_spec, cache_spec, cache_spec, new_spec, new_spec,
                  pl.BlockSpec((None, H_C, tk), lambda bi, j: (bi, 0, j)),
                  col_spec, col_spec,
                  pl.BlockSpec((None, H_C, t), lambda bi, j: (bi, 0, 0))],
        out_specs=new_spec,
        out_shape=jax.ShapeDtypeStruct((b, t, d), BF16),
        scratch_shapes=[pltpu.VMEM((rows, tk), F32), pltpu.VMEM((rows, tk), BF16),
                        pltpu.VMEM((rows, LANES), F32), pltpu.VMEM((rows, LANES), F32),
                        pltpu.VMEM((rows, HD_C), F32),
                        pltpu.VMEM((H_C, tk, HD_C), BF16), pltpu.VMEM((H_C, tk, HD_C), BF16)],
        compiler_params=_params("arbitrary", "arbitrary"),
    )(q, k_cache.reshape(n_layers, b, p * H_C, HD_C), v_cache.reshape(n_layers, b, p * H_C, HD_C),
      k_new, v_new, fc_heads, fn_col, fc_end_col, fn_heads)


def kernel(x_prompt, x_sample, c_prompt, c_sample, cache_conv, cache_win_k, cache_win_v, cache_fox_k, cache_fox_v, cache_fox_logf, ada_w, ada_b, norm_mix_g, norm_ffn_g, e_w_in, e_w_dw, e_b_dw, e_ln_g, e_ln_b, e_sinks, e_w_out, o_w_in, o_b_f, o_w_out, ffn_w_gate, ffn_w_up, ffn_w_down, final_g):
    bp, tp, d = x_prompt.shape
    bs, ts, _ = x_sample.shape
    depth = ada_w.shape[0]
    n_odd = o_w_in.shape[0]
    past_len = cache_fox_k.shape[2]
    nqkv = H_C * HD_C

    mod = _adaln(jnp.concatenate([c_prompt, c_sample], axis=0), ada_w, ada_b)
    tab_p = _rope_tables(jnp.arange(tp))
    tab_s = tuple(jnp.tile(x, (bs, 1)) for x in _rope_tables(past_len + jnp.arange(ts)))

    xp = x_prompt.reshape(bp * tp, d)
    xs = x_sample.reshape(bs * ts, d)
    hp = _modulate(x_prompt, norm_mix_g[0], mod, 0, 0, 0).reshape(bp * tp, d)
    hs = _modulate(x_sample, norm_mix_g[0], mod, 0, 0, bp).reshape(bs * ts, d)
    outs = {name: [] for name in ("conv_p", "wk_p", "wv_p", "fl_p",
                                  "conv_s", "wk_s", "wv_s", "fk_s", "fv_s", "fl_s")}
    fk_p = fv_p = None
    e_in, e_out = e_w_in.astype(BF16), e_w_out.astype(BF16)
    o_in, o_out = o_w_in[:, :, :3 * nqkv].astype(BF16), o_w_out.astype(BF16)
    wg, wu, wd = ffn_w_gate.astype(BF16), ffn_w_up.astype(BF16), ffn_w_down.astype(BF16)
    for l in range(depth):
        if l % 2 == 0:
            e = l // 2
            w_out, w_out_layer = e_out, e
            conv = (e_w_dw[e], e_b_dw[e], e_ln_g[e], e_ln_b[e])
            zp = _even_proj(hp, e_in, e, tab_p, tp).reshape(bp, tp, E_IN)
            cat, cst = _even_mix(zp, e_sinks[e], *conv)
            win = min(WINDOW, tp)
            outs["conv_p"].append(cst[:, HIST_PAD - (CONV_W - 1):])
            outs["wk_p"].append(zp[:, tp - win:, K_OFF:V_OFF].reshape(bp, win, KV_B, HD_B))
            outs["wv_p"].append(zp[:, tp - win:, V_OFF:].reshape(bp, win, KV_B, HD_B))
            yp = cat.reshape(bp * tp, d)

            zs = _even_proj(hs, e_in, e, tab_s, bs * ts).reshape(bs, ts, E_IN)
            hist = jnp.pad(cache_conv[e], ((0, 0), (HIST_PAD - (CONV_W - 1), 0), (0, 0)))
            cat, cst = _even_mix_sample(zs, hist, cache_win_k[e], cache_win_v[e], e_sinks[e], *conv)
            outs["conv_s"].append(cst[:, HIST_PAD - (CONV_W - 1):])
            outs["wk_s"].append(zs[:, :, K_OFF:V_OFF].reshape(bs, ts, KV_B, HD_B))
            outs["wv_s"].append(zs[:, :, V_OFF:].reshape(bs, ts, KV_B, HD_B))
            ys = cat.reshape(bs * ts, d)
        else:
            o = l // 2
            wf = jnp.pad(o_w_in[o][:, 3 * nqkv:], ((0, 0), (0, LANES - H_C))).astype(BF16)[None]
            bf = jnp.pad(o_b_f[o], (0, LANES - H_C)).reshape(1, LANES)
            w_out, w_out_layer = o_out, o

            q = _matmul(hp, o_in, o, 0, nqkv, BF16).reshape(bp, tp, nqkv)
            k, fk_p = _kv_proj(hp, o_in, o, nqkv, fk_p, o, n_odd)
            v, fv_p = _kv_proj(hp, o_in, o, 2 * nqkv, fv_p, o, n_odd)
            logit = _matmul(hp, wf, 0, 0, LANES, F32).reshape(bp, tp, LANES)
            lf, f_rows, f_heads = _forget_scan(logit, bf, True)
            outs["fl_p"].append(lf)
            yp = _fox_prompt(q, k.reshape(bp, tp, nqkv), v.reshape(bp, tp, nqkv),
                             f_rows, f_heads).reshape(bp * tp, d)

            q = _matmul(hs, o_in, o, 0, nqkv, BF16).reshape(bs, ts, nqkv)
            k = _matmul(hs, o_in, o, nqkv, nqkv, F32).reshape(bs, ts, nqkv)
            v = _matmul(hs, o_in, o, 2 * nqkv, nqkv, F32).reshape(bs, ts, nqkv)
            logit = _matmul(hs, wf, 0, 0, LANES, F32).reshape(bs, ts, LANES)
            lf, _, fn_heads = _forget_scan(logit, bf, True)
            cache_lf = jnp.pad(cache_fox_logf[o], ((0, 0), (0, 0), (0, LANES - H_C)))
            _, _, fc_heads = _forget_scan(cache_lf, bf, False)
            outs["fk_s"].append(k.reshape(bs, ts, H_C, HD_C))
            outs["fv_s"].append(v.reshape(bs, ts, H_C, HD_C))
            outs["fl_s"].append(lf)
            ys = _fox_sample(q, k, v, cache_fox_k, cache_fox_v, o, fc_heads,
                             fn_heads).reshape(bs * ts, d)

        xp, hp = _residual_matmul(yp, w_out, w_out_layer, xp, norm_ffn_g[l], mod, l, tp, 0)
        xs, hs = _residual_matmul(ys, w_out, w_out_layer, xs, norm_ffn_g[l], mod, l, ts, bp)

        if l + 1 < depth:
            xp, hp = _ffn(hp, xp, wg, wu, wd, norm_mix_g[l + 1], mod, l, tp, 0, False)
            xs, hs = _ffn(hs, xs, wg, wu, wd, norm_mix_g[l + 1], mod, l, ts, bp, False)
        else:
            y_prompt = _ffn(hp, xp, wg, wu, wd, final_g, mod, l, tp, 0, True).reshape(bp, tp, d)
            y_sample = _ffn(hs, xs, wg, wu, wd, final_g, mod, l, ts, bp, True).reshape(bs, ts, d)

    stack = lambda name: jnp.stack(outs[name])
    return (y_prompt, y_sample,
            stack("conv_p"), stack("wk_p"), stack("wv_p"),
            fk_p.reshape(n_odd, bp, tp, H_C, HD_C), fv_p.reshape(n_odd, bp, tp, H_C, HD_C), stack("fl_p"),
            stack("conv_s"), stack("wk_s"), stack("wv_s"), stack("fk_s"), stack("fv_s"), stack("fl_s"))
```

```python
import functools
import math

import jax
import jax.numpy as jnp
from jax import lax
from jax.experimental import pallas as pl
from jax.experimental.pallas import tpu as pltpu

F32 = jnp.float32
BF16 = jnp.bfloat16

D_MODEL = 2048
CHUNK = 64
C_CONV = D_MODEL // 2
CONV_W = 31
HIST_PAD = 32
HD_B = 64
H_B = (D_MODEL // 2) // HD_B
KV_B = H_B // 4
G_B = H_B // KV_B
WINDOW = 128
E_IN = 2 * C_CONV + H_B * HD_B + 2 * KV_B * HD_B
Q_OFF = 2 * C_CONV
K_OFF = Q_OFF + H_B * HD_B
V_OFF = K_OFF + KV_B * HD_B
HD_C = 128
H_C = D_MODEL // HD_C
ROPE_THETA = 10000.0
EPS = 1e-6
NEG = -1e30
LANES = 128
SUBLANES = 8
VMEM_LIMIT = 56 * 1024 * 1024


def _params(*sem):
    return pltpu.CompilerParams(dimension_semantics=sem, vmem_limit_bytes=VMEM_LIMIT)


def _tile(n, pref):
    t = min(n, pref)
    while n % t:
        t //= 2
    return t


def _bcast_rows(vec, rows):
    nb, _, w = vec.shape
    if nb == 1:
        return vec[0]
    return jnp.broadcast_to(vec, (nb, rows // nb, w)).reshape(rows, w)


def _mod_spec(layer, chunk, nb, width, index_fn):
    return pl.BlockSpec((None, None, nb, 1, width),
                        lambda *g: (layer, chunk) + tuple(index_fn(*g)))


def _row_tiling(m, rows_per_batch, pref):
    if m > rows_per_batch and m > pref:
        tm = _tile(rows_per_batch, pref)
        return tm, 1, lambda i, row_off: (i * tm) // rows_per_batch + row_off
    nb = max(m // rows_per_batch, 1)
    return m, nb, lambda i, row_off: row_off // nb + i


def _rms_scale(x):
    return lax.rsqrt(jnp.mean(x * x, axis=-1, keepdims=True) + EPS)


def _adaln_kernel(c_ref, w_ref, b_ref, o_ref):
    c = c_ref[...]
    a = (c * jax.nn.sigmoid(c)).astype(BF16)
    o_ref[...] = jnp.dot(a, w_ref[...].astype(BF16), preferred_element_type=F32) + b_ref[...]


def _adaln(c_all, ada_w, ada_b):
    n_layers, d, n = ada_w.shape
    nb = c_all.shape[0]
    tn = 1024
    per = d // tn
    out = pl.pallas_call(
        _adaln_kernel,
        name="adaln",
        grid=(n_layers, n // tn),
        in_specs=[pl.BlockSpec((nb, d), lambda l, j: (0, 0)),
                  pl.BlockSpec((None, d, tn), lambda l, j: (l, 0, j)),
                  pl.BlockSpec((None, 1, tn), lambda l, j: (l, 0, j))],
        out_specs=pl.BlockSpec((None, None, nb, tn), lambda l, j: (l, j // per, 0, j % per)),
        out_shape=jax.ShapeDtypeStruct((n_layers, n // d, nb, d), F32),
        compiler_params=_params("arbitrary", "arbitrary"),
    )(c_all, ada_w, ada_b.reshape(n_layers, 1, n))
    return out.reshape(n_layers, n // d, nb, 1, d)


def _modulate_kernel(x_ref, g_ref, sh_ref, sc_ref, o_ref):
    x = x_ref[...]
    y = x * _rms_scale(x) * (g_ref[...] * (1.0 + sc_ref[0]))
    o_ref[...] = (y + sh_ref[0]).astype(o_ref.dtype)


def _modulate(x, g, mod, layer, chunk, row_off):
    b, t, d = x.shape
    tt = _tile(t, 512)
    return pl.pallas_call(
        _modulate_kernel,
        name="modulate",
        grid=(b, t // tt),
        in_specs=[pl.BlockSpec((None, tt, d), lambda bi, i: (bi, i, 0)),
                  pl.BlockSpec((1, d), lambda bi, i: (0, 0)),
                  _mod_spec(layer, chunk, 1, d, lambda bi, i: (bi + row_off, 0, 0)),
                  _mod_spec(layer, chunk + 1, 1, d, lambda bi, i: (bi + row_off, 0, 0))],
        out_specs=pl.BlockSpec((None, tt, d), lambda bi, i: (bi, i, 0)),
        out_shape=jax.ShapeDtypeStruct((b, t, d), BF16),
        compiler_params=_params("arbitrary", "arbitrary"),
    )(x, g.reshape(1, d), mod, mod)


def _mm_kernel(a_ref, w_ref, o_ref):
    o_ref[...] = jnp.dot(a_ref[...], w_ref[...], preferred_element_type=F32).astype(o_ref.dtype)


def _matmul(a, w, layer, col0, n, out_dtype):
    m, k = a.shape
    tm, tn = _tile(m, 1024), _tile(n, 1024)
    assert col0 % tn == 0
    return pl.pallas_call(
        _mm_kernel,
        name="matmul",
        grid=(m // tm, n // tn),
        in_specs=[pl.BlockSpec((tm, k), lambda i, j: (i, 0)),
                  pl.BlockSpec((None, k, tn), lambda i, j: (layer, 0, col0 // tn + j))],
        out_specs=pl.BlockSpec((tm, tn), lambda i, j: (i, j)),
        out_shape=jax.ShapeDtypeStruct((m, n), out_dtype),
        compiler_params=_params("arbitrary", "arbitrary"),
    )(a, w)


def _kv_proj_kernel(a_ref, w_ref, *refs, tm, tn):
    rows_ref, heads_ref, y_ref = refs[-3:]
    a = a_ref[...]
    for j in range(w_ref.shape[1] // tn):
        y = jnp.dot(a, w_ref[:, j * tn:(j + 1) * tn], preferred_element_type=F32)
        rows_ref[:, j * tn:(j + 1) * tn] = y.astype(rows_ref.dtype)
        y_ref[:, j * tn:(j + 1) * tn] = y
    step = 8 * SUBLANES
    for r in range(0, tm, step):
        heads_ref[r * H_C:(r + step) * H_C, :] = pltpu.einshape(
            "t(hd)->(th)d", y_ref[r:r + step, :], h=H_C)


def _kv_proj(a, w, w_layer, col0, stacked, layer, n_layers):
    m, k = a.shape
    n = H_C * HD_C
    assert col0 % n == 0
    tm = _tile(m, 512)
    in_specs = [pl.BlockSpec((tm, k), lambda i: (i, 0)),
                pl.BlockSpec((None, k, n), lambda i: (w_layer, 0, col0 // n))]
    args = [a, w]
    aliases = {}
    if stacked is not None:
        in_specs.append(pl.BlockSpec(memory_space=pl.ANY))
        args.append(stacked)
        aliases = {2: 1}
    return pl.pallas_call(
        functools.partial(_kv_proj_kernel, tm=tm, tn=512),
        name="kv_proj",
        grid=(m // tm,),
        in_specs=in_specs,
        out_specs=[pl.BlockSpec((tm, n), lambda i: (i, 0)),
                   pl.BlockSpec((None, tm * H_C, HD_C), lambda i: (layer, i, 0))],
        out_shape=[jax.ShapeDtypeStruct((m, n), BF16),
                   jax.ShapeDtypeStruct((n_layers, m * H_C, HD_C), F32)],
        input_output_aliases=aliases,
        scratch_shapes=[pltpu.VMEM((tm, n), F32)],
        compiler_params=_params("arbitrary"),
    )(*args)


def _rope_tables(pos):
    half = HD_B // 2
    freqs = ROPE_THETA ** (-jnp.arange(half, dtype=F32) / half)
    ang = pos.astype(F32)[:, None] * freqs[None, :]
    cos, sin = jnp.cos(ang), jnp.sin(ang)
    zero = jnp.zeros_like(sin)
    rep = LANES // HD_B
    cos_t = jnp.tile(jnp.concatenate([cos, cos], axis=1), (1, rep))
    sa = jnp.tile(jnp.concatenate([-sin, zero], axis=1), (1, rep))
    sb = jnp.tile(jnp.concatenate([zero, sin], axis=1), (1, rep))
    return cos_t, sa, sb


def _rope(x, cos, sa, sb):
    half = HD_B // 2
    outs = []
    for g in range(x.shape[1] // LANES):
        xg = x[:, g * LANES:(g + 1) * LANES]
        outs.append(xg * cos + pltpu.roll(xg, LANES - half, 1) * sa + pltpu.roll(xg, half, 1) * sb)
    return outs[0] if len(outs) == 1 else jnp.concatenate(outs, axis=1)


def _even_proj_kernel(a_ref, w_ref, cos_ref, sa_ref, sb_ref, o_ref, *, tn):
    a = a_ref[...]
    kw = KV_B * HD_B
    for j in range(E_IN // tn):
        c0 = j * tn
        z = jnp.dot(a, w_ref[:, c0:c0 + tn], preferred_element_type=F32)
        if c0 + tn <= Q_OFF:
            o_ref[:, c0:c0 + tn] = z
        elif c0 + tn <= K_OFF:
            o_ref[:, c0:c0 + tn] = _rope(z, cos_ref[...], sa_ref[...], sb_ref[...])
        else:
            o_ref[:, c0:c0 + kw] = _rope(z[:, :kw], cos_ref[...], sa_ref[...], sb_ref[...])
            o_ref[:, c0 + kw:c0 + tn] = z[:, kw:]


def _even_proj(a, w, layer, tables, period):
    m, k = a.shape
    n = w.shape[2]
    tn = 2 * KV_B * HD_B
    assert Q_OFF % tn == 0 and K_OFF % tn == 0 and n == K_OFF + tn
    tm = _tile(period, 512) if m > period else m
    per_blocks = max(period // tm, 1)
    tab_spec = pl.BlockSpec((tm, LANES), lambda i: (i % per_blocks, 0))
    return pl.pallas_call(
        functools.partial(_even_proj_kernel, tn=tn),
        name="even_proj",
        grid=(m // tm,),
        in_specs=[pl.BlockSpec((tm, k), lambda i: (i, 0)),
                  pl.BlockSpec((None, k, n), lambda i: (layer, 0, 0), pipeline_mode=pl.Buffered(1)),
                  tab_spec, tab_spec, tab_spec],
        out_specs=pl.BlockSpec((tm, n), lambda i: (i, 0)),
        out_shape=jax.ShapeDtypeStruct((m, n), F32),
        compiler_params=_params("arbitrary"),
    )(a, w, *tables)


def _residual_kernel(a_ref, w_ref, x_ref, gate_ref, g_ref, sh_ref, sc_ref, xo_ref, ho_ref, *, tn):
    tm, d = xo_ref.shape
    a = a_ref[...]
    gate = _bcast_rows(gate_ref[...], tm)
    ss = jnp.zeros((tm, 1), F32)
    for j in range(d // tn):
        cs = slice(j * tn, (j + 1) * tn)
        xn = x_ref[:, cs] + gate[:, cs] * jnp.dot(a, w_ref[:, cs], preferred_element_type=F32)
        xo_ref[:, cs] = xn
        ss = ss + jnp.sum(xn * xn, axis=-1, keepdims=True)
    gs = _bcast_rows(g_ref[...] * (1.0 + sc_ref[...]), tm)
    h = xo_ref[...] * lax.rsqrt(ss / d + EPS) * gs + _bcast_rows(sh_ref[...], tm)
    ho_ref[...] = h.astype(ho_ref.dtype)


def _residual_matmul(a, w, w_layer, x, g, mod, layer, rows_per_batch, row_off):
    m, k = a.shape
    d = w.shape[2]
    tm, nb, row_fn = _row_tiling(m, rows_per_batch, 512)
    vec = lambda chunk: _mod_spec(layer, chunk, nb, d, lambda i: (row_fn(i, row_off), 0, 0))
    return pl.pallas_call(
        functools.partial(_residual_kernel, tn=512),
        name="residual",
        grid=(m // tm,),
        in_specs=[pl.BlockSpec((tm, k), lambda i: (i, 0)),
                  pl.BlockSpec((None, k, d), lambda i: (w_layer, 0, 0)),
                  pl.BlockSpec((tm, d), lambda i: (i, 0)),
                  vec(2),
                  pl.BlockSpec((1, 1, d), lambda i: (0, 0, 0)),
                  vec(3), vec(4)],
        out_specs=[pl.BlockSpec((tm, d), lambda i: (i, 0)),
                   pl.BlockSpec((tm, d), lambda i: (i, 0))],
        out_shape=[jax.ShapeDtypeStruct((m, d), F32), jax.ShapeDtypeStruct((m, d), BF16)],
        compiler_params=_params("arbitrary"),
    )(a, w, x, mod, g.reshape(1, 1, d), mod, mod)


def _ffn_kernel(h_ref, x_ref, gate_ref, wg_ref, wu_ref, wd_ref, g_ref, *refs, final, rows):
    if final:
        y_ref, acc_ref = refs
    else:
        sh_ref, sc_ref, xo_ref, ho_ref, acc_ref = refs
    f = pl.program_id(1)
    tm, d = acc_ref.shape

    @pl.when(f == 0)
    def _():
        acc_ref[...] = jnp.zeros_like(acc_ref)

    h = h_ref[...]
    g = jnp.dot(h, wg_ref[...], preferred_element_type=F32)
    u = jnp.dot(h, wu_ref[...], preferred_element_type=F32)
    a = (g * jax.nn.sigmoid(g) * u).astype(BF16)
    acc_ref[...] += jnp.dot(a, wd_ref[...], preferred_element_type=F32)

    @pl.when(f == pl.num_programs(1) - 1)
    def _():
        nb = gate_ref.shape[0]
        per = tm // nb
        for r0 in range(0, tm, rows):
            rs = slice(r0, r0 + rows)
            b = r0 // per
            xn = x_ref[rs, :] + gate_ref[b] * acc_ref[rs, :]
            if final:
                y_ref[rs, :] = xn * _rms_scale(xn) * g_ref[0]
            else:
                xo_ref[rs, :] = xn
                hn = xn * _rms_scale(xn) * (g_ref[0] * (1.0 + sc_ref[b])) + sh_ref[b]
                ho_ref[rs, :] = hn.astype(ho_ref.dtype)


def _ffn(h, x, wg, wu, wd, g_next, mod, layer, rows_per_batch, row_off, final):
    m, d = h.shape
    ff = wg.shape[2]
    tf = _tile(ff, 512)
    tm, nb, row_fn = _row_tiling(m, rows_per_batch, 512)
    vec = lambda lyr, chunk: _mod_spec(lyr, chunk, nb, d, lambda i, f: (row_fn(i, row_off), 0, 0))
    rows_spec = pl.BlockSpec((tm, d), lambda i, f: (i, 0))
    in_specs = [rows_spec, rows_spec, vec(layer, 5),
                pl.BlockSpec((None, d, tf), lambda i, f: (layer, 0, f)),
                pl.BlockSpec((None, d, tf), lambda i, f: (layer, 0, f)),
                pl.BlockSpec((None, tf, d), lambda i, f: (layer, f, 0)),
                pl.BlockSpec((1, 1, d), lambda i, f: (0, 0, 0))]
    args = [h, x, mod, wg, wu, wd, g_next.reshape(1, 1, d)]
    if final:
        out_specs = rows_spec
        out_shape = jax.ShapeDtypeStruct((m, d), F32)
    else:
        in_specs += [vec(layer + 1, 0), vec(layer + 1, 1)]
        args += [mod, mod]
        out_specs = [rows_spec, rows_spec]
        out_shape = [jax.ShapeDtypeStruct((m, d), F32), jax.ShapeDtypeStruct((m, d), BF16)]
    return pl.pallas_call(
        functools.partial(_ffn_kernel, final=final, rows=2 * SUBLANES),
        name="ffn",
        grid=(m // tm, ff // tf),
        in_specs=in_specs,
        out_specs=out_specs,
        out_shape=out_shape,
        scratch_shapes=[pltpu.VMEM((tm, d), F32)],
        compiler_params=_params("arbitrary", "arbitrary"),
    )(*args)


def _glu_rows(z_ref, uext_ref, rows):
    step = min(rows, 4 * SUBLANES)
    for r in range(0, rows, step):
        uext_ref[HIST_PAD + r:HIST_PAD + r + step, :] = (
            z_ref[r:r + step, 0:C_CONV] * jax.nn.sigmoid(z_ref[r:r + step, C_CONV:2 * C_CONV]))


def _conv_ln_silu(uext_ref, wdw_ref, bdw_ref, lng_ref, lnb_ref, y_ref, shift_ref, out_ref, rows):
    base = HIST_PAD - (CONV_W - 1)
    for c in range(C_CONV // LANES):
        cs = slice(c * LANES, (c + 1) * LANES)
        for r in range(SUBLANES):
            n = rows + SUBLANES * ((CONV_W - 1 - r) // SUBLANES)
            shift_ref[r, 0:n, :] = uext_ref[base + r:base + r + n, cs]
        rb = min(rows, 8 * SUBLANES)

        def row_block(i, carry):
            t0 = pl.multiple_of(i * rb, rb)
            acc = jnp.zeros((rb, LANES), F32)
            for j in range(CONV_W):
                r, a = j % SUBLANES, j // SUBLANES
                acc = acc + wdw_ref[j:j + 1, cs] * shift_ref[r, pl.ds(t0 + SUBLANES * a, rb), :]
            y_ref[pl.ds(t0, rb), cs] = acc + bdw_ref[:, cs]
            return carry

        lax.fori_loop(0, rows // rb, row_block, 0)
    step = min(rows, 4 * SUBLANES)
    for r in range(0, rows, step):
        y = y_ref[r:r + step, :]
        yc = y - jnp.mean(y, axis=-1, keepdims=True)
        var = jnp.mean(yc * yc, axis=-1, keepdims=True)
        yn = yc * lax.rsqrt(var + EPS) * lng_ref[...] + lnb_ref[...]
        out_ref[r:r + step, 0:C_CONV] = (yn * jax.nn.sigmoid(yn)).astype(out_ref.dtype)


def _sink_attention(q, k, v, sink_col, mask):
    s = lax.dot_general(q, k, (((1,), (1,)), ((), ())), preferred_element_type=F32) / math.sqrt(HD_B)
    if mask is not None:
        s = jnp.where(mask, s, NEG)
    m = jnp.maximum(jnp.max(s, axis=-1, keepdims=True), sink_col)
    p = jnp.exp(s - m)
    den = jnp.sum(p, axis=-1, keepdims=True) + jnp.exp(sink_col - m)
    return jnp.dot(p.astype(BF16), v, preferred_element_type=F32) / den


def _sink_column(sink_ref, kv_head, rows):
    return jnp.concatenate(
        [jnp.full((rows, 1), sink_ref[kv_head * G_B + g], F32) for g in range(G_B)], axis=0)


def _group_attention(sink_ref, z_ref, kall, vall, r0, rows, mask, cat_ref):
    for kh in range(KV_B):
        hs = slice(kh * HD_B, (kh + 1) * HD_B)
        q = jnp.concatenate(
            [z_ref[r0:r0 + rows, Q_OFF + (kh * G_B + g) * HD_B:Q_OFF + (kh * G_B + g + 1) * HD_B]
             for g in range(G_B)], axis=0).astype(BF16)
        o = _sink_attention(q, kall[:, hs], vall[:, hs], _sink_column(sink_ref, kh, rows), mask)
        for gp in range(G_B // 2):
            pair = jnp.concatenate([o[(2 * gp) * rows:(2 * gp + 1) * rows],
                                    o[(2 * gp + 1) * rows:(2 * gp + 2) * rows]], axis=1)
            c0 = C_CONV + (kh * G_B + 2 * gp) * HD_B
            cat_ref[r0:r0 + rows, c0:c0 + 2 * HD_B] = pair.astype(cat_ref.dtype)


def _even_mix_kernel(sink_ref, zc_ref, zh_ref, zkv_ref, wdw_ref, bdw_ref, lng_ref, lnb_ref,
                     cat_ref, cst_ref, uext_ref, y_ref, shift_ref, *, tt):
    i = pl.program_id(1)
    first = i == 0

    _glu_rows(zc_ref, uext_ref, tt)
    hist = zh_ref[:, 0:C_CONV] * jax.nn.sigmoid(zh_ref[:, C_CONV:2 * C_CONV])
    uext_ref[0:HIST_PAD, :] = jnp.where(first, 0.0, hist)
    _conv_ln_silu(uext_ref, wdw_ref, bdw_ref, lng_ref, lnb_ref, y_ref, shift_ref, cat_ref, tt)

    @pl.when(i == pl.num_programs(1) - 1)
    def _():
        cst_ref[...] = uext_ref[tt:tt + HIST_PAD, :]

    kw = KV_B * HD_B
    sub = 2 * CHUNK
    kext = jnp.concatenate([zkv_ref[:, 0:kw], zc_ref[:, K_OFF:K_OFF + kw]], axis=0).astype(BF16)
    v_t = jnp.concatenate([zkv_ref[:, kw:2 * kw], zc_ref[:, V_OFF:V_OFF + kw]], axis=0).T.astype(BF16)
    cols = G_B * sub
    kc = lax.broadcasted_iota(jnp.int32, (2 * sub, cols), 0) // CHUNK
    qc = (lax.broadcasted_iota(jnp.int32, (2 * sub, cols), 1) % sub) // CHUNK
    band = jnp.logical_and(kc >= qc, kc <= qc + WINDOW // CHUNK)
    band_first = jnp.logical_and(band, jnp.logical_or(jnp.logical_not(first), kc >= WINDOW // CHUNK))
    for s in range(tt // sub):
        r0 = s * sub
        mask = band_first if s == 0 else band
        for kh in range(KV_B):
            hs = slice(kh * HD_B, (kh + 1) * HD_B)
            q = jnp.concatenate(
                [zc_ref[r0:r0 + sub, Q_OFF + (kh * G_B + g) * HD_B:Q_OFF + (kh * G_B + g + 1) * HD_B]
                 for g in range(G_B)], axis=0).astype(BF16)
            logit = lax.dot_general(kext[r0:r0 + 2 * sub, hs], q, (((1,), (1,)), ((), ())),
                                    preferred_element_type=F32) / math.sqrt(HD_B)
            logit = jnp.where(mask, logit, NEG)
            sink = jnp.concatenate(
                [jnp.full((1, sub), sink_ref[kh * G_B + g], F32) for g in range(G_B)], axis=1)
            m = jnp.maximum(jnp.max(logit, axis=0, keepdims=True), sink)
            p = jnp.exp(logit - m)
            den = jnp.sum(p, axis=0, keepdims=True) + jnp.exp(sink - m)
            o_t = jnp.dot(v_t[hs, r0:r0 + 2 * sub], p.astype(BF16), preferred_element_type=F32) / den
            for gp in range(G_B // 2):
                pair_t = jnp.concatenate([o_t[:, (2 * gp) * sub:(2 * gp + 1) * sub],
                                          o_t[:, (2 * gp + 1) * sub:(2 * gp + 2) * sub]], axis=0)
                c0 = C_CONV + (kh * G_B + 2 * gp) * HD_B
                cat_ref[r0:r0 + sub, c0:c0 + 2 * HD_B] = pair_t.T.astype(cat_ref.dtype)


def _even_mix(z, sinks, w_dw, b_dw, ln_g, ln_b):
    b, t, _ = z.shape
    tt = _tile(t, 512)
    hb = tt // HIST_PAD
    kb = tt // WINDOW
    kvw = 2 * KV_B * HD_B
    vec = pl.BlockSpec((1, C_CONV), lambda bi, i: (0, 0))
    return pl.pallas_call(
        functools.partial(_even_mix_kernel, tt=tt),
        name="even_mix",
        grid=(b, t // tt),
        in_specs=[pl.BlockSpec(memory_space=pltpu.SMEM),
                  pl.BlockSpec((None, tt, E_IN), lambda bi, i: (bi, i, 0)),
                  pl.BlockSpec((None, HIST_PAD, 2 * C_CONV),
                               lambda bi, i: (bi, jnp.maximum(i * hb - 1, 0), 0)),
                  pl.BlockSpec((None, WINDOW, kvw),
                               lambda bi, i: (bi, jnp.maximum(i * kb - 1, 0), K_OFF // kvw)),
                  pl.BlockSpec((CONV_W, C_CONV), lambda bi, i: (0, 0)),
                  vec, vec, vec],
        out_specs=[pl.BlockSpec((None, tt, D_MODEL), lambda bi, i: (bi, i, 0)),
                   pl.BlockSpec((None, HIST_PAD, C_CONV), lambda bi, i: (bi, 0, 0))],
        out_shape=[jax.ShapeDtypeStruct((b, t, D_MODEL), BF16),
                   jax.ShapeDtypeStruct((b, HIST_PAD, C_CONV), F32)],
        scratch_shapes=[pltpu.VMEM((HIST_PAD + tt, C_CONV), F32),
                        pltpu.VMEM((tt, C_CONV), F32),
                        pltpu.VMEM((SUBLANES, tt + HIST_PAD, LANES), F32)],
        compiler_params=_params("arbitrary", "arbitrary"),
    )(sinks, z, z, z, w_dw, b_dw.reshape(1, -1), ln_g.reshape(1, -1), ln_b.reshape(1, -1))


def _even_mix_sample_kernel(sink_ref, z_ref, hist_ref, kc_ref, vc_ref, wdw_ref, bdw_ref, lng_ref,
                            lnb_ref, cat_ref, cst_ref, uext_ref, y_ref, shift_ref, *, t):
    uext_ref[0:HIST_PAD, :] = hist_ref[...]
    _glu_rows(z_ref, uext_ref, t)
    _conv_ln_silu(uext_ref, wdw_ref, bdw_ref, lng_ref, lnb_ref, y_ref, shift_ref, cat_ref, t)
    cst_ref[...] = uext_ref[t:t + HIST_PAD, :]

    kw = KV_B * HD_B
    kall = jnp.concatenate([kc_ref[...], z_ref[:, K_OFF:K_OFF + kw]], axis=0).astype(BF16)
    vall = jnp.concatenate([vc_ref[...], z_ref[:, V_OFF:V_OFF + kw]], axis=0).astype(BF16)
    _group_attention(sink_ref, z_ref, kall, vall, 0, t, None, cat_ref)


def _even_mix_sample(z, hist, k_cache, v_cache, sinks, w_dw, b_dw, ln_g, ln_b):
    b, t, _ = z.shape
    win = k_cache.shape[1]
    kw = KV_B * HD_B
    vec = pl.BlockSpec((1, C_CONV), lambda bi: (0, 0))
    return pl.pallas_call(
        functools.partial(_even_mix_sample_kernel, t=t),
        name="even_mix_sample",
        grid=(b,),
        in_specs=[pl.BlockSpec(memory_space=pltpu.SMEM),
                  pl.BlockSpec((None, t, E_IN), lambda bi: (bi, 0, 0)),
                  pl.BlockSpec((None, HIST_PAD, C_CONV), lambda bi: (bi, 0, 0)),
                  pl.BlockSpec((None, win, kw), lambda bi: (bi, 0, 0)),
                  pl.BlockSpec((None, win, kw), lambda bi: (bi, 0, 0)),
                  pl.BlockSpec((CONV_W, C_CONV), lambda bi: (0, 0)),
                  vec, vec, vec],
        out_specs=[pl.BlockSpec((None, t, D_MODEL), lambda bi: (bi, 0, 0)),
                   pl.BlockSpec((None, HIST_PAD, C_CONV), lambda bi: (bi, 0, 0))],
        out_shape=[jax.ShapeDtypeStruct((b, t, D_MODEL), BF16),
                   jax.ShapeDtypeStruct((b, HIST_PAD, C_CONV), F32)],
        scratch_shapes=[pltpu.VMEM((HIST_PAD + t, C_CONV), F32),
                        pltpu.VMEM((t, C_CONV), F32),
                        pltpu.VMEM((SUBLANES, t + HIST_PAD, LANES), F32)],
        compiler_params=_params("arbitrary"),
    )(sinks, z, hist, k_cache.reshape(b, win, kw), v_cache.reshape(b, win, kw),
      w_dw, b_dw.reshape(1, -1), ln_g.reshape(1, -1), ln_b.reshape(1, -1))


def _scan_kernel(x_ref, b_ref, lf_ref, f_ref, ft_ref, carry_ref, *, tt, log_sigmoid):
    i = pl.program_id(1)

    @pl.when(i == 0)
    def _():
        carry_ref[...] = jnp.zeros_like(carry_ref)

    x = x_ref[...]
    if log_sigmoid:
        x = x + b_ref[...]
        x = jnp.minimum(x, 0.0) - jnp.log1p(jnp.exp(-jnp.abs(x)))
    lf_ref[...] = x[:, :H_C]
    row = lax.broadcasted_iota(jnp.int32, x.shape, 0)
    step = 1
    while step < tt:
        x = x + jnp.where(row >= step, pltpu.roll(x, step, 0), 0.0)
        step *= 2
    f = x + carry_ref[...]
    carry_ref[...] = f[tt - 1:tt, :]
    f_ref[...] = f
    if tt < LANES:
        f = jnp.concatenate([f, jnp.zeros((LANES - tt, LANES), F32)], axis=0)
    ft_ref[...] = f.T[:H_C, :tt]


def _forget_scan(x, bias, log_sigmoid):
    b, t, _ = x.shape
    tt = _tile(t, 512)
    return pl.pallas_call(
        functools.partial(_scan_kernel, tt=tt, log_sigmoid=log_sigmoid),
        name="forget_scan",
        grid=(b, t // tt),
        in_specs=[pl.BlockSpec((None, tt, LANES), lambda bi, i: (bi, i, 0)),
                  pl.BlockSpec((1, LANES), lambda bi, i: (0, 0))],
        out_specs=[pl.BlockSpec((None, tt, H_C), lambda bi, i: (bi, i, 0)),
                   pl.BlockSpec((None, tt, LANES), lambda bi, i: (bi, i, 0)),
                   pl.BlockSpec((None, H_C, tt), lambda bi, i: (bi, 0, i))],
        out_shape=[jax.ShapeDtypeStruct((b, t, H_C), F32),
                   jax.ShapeDtypeStruct((b, t, LANES), F32),
                   jax.ShapeDtypeStruct((b, H_C, t), F32)],
        scratch_shapes=[pltpu.VMEM((1, LANES), F32)],
        compiler_params=_params("arbitrary", "arbitrary"),
    )(x, bias)


def _lane_fold(x, op):
    out = x[:, 0:LANES]
    for c in range(1, x.shape[1] // LANES):
        out = op(out, x[:, c * LANES:(c + 1) * LANES])
    return out


def _fox_kernel(q_ref, k_ref, v_ref, fq_ref, fk_ref, o_ref, s_ref, vb_ref, *, tq, t):
    h = pl.program_id(1)
    vb_ref[:, 0:HD_C] = v_ref[...]
    vb_ref[:, HD_C:] = jnp.ones((t, HD_C), BF16)
    lane = lax.broadcasted_iota(jnp.int32, (tq, LANES), 1)
    causal = (lax.broadcasted_iota(jnp.int32, (tq, tq), 1)
              <= lax.broadcasted_iota(jnp.int32, (tq, tq), 0))
    for qi in reversed(range(t // tq)):
        rows = slice(qi * tq, (qi + 1) * tq)
        base = (qi * (qi + 1)) // 2
        q = q_ref[rows, :]
        fq = jnp.sum(jnp.where(lane == h, fq_ref[rows, :], 0.0), axis=-1, keepdims=True)
        mx = None
        for j in range(qi + 1):
            ks = slice(j * tq, (j + 1) * tq)
            s = lax.dot_general(q, k_ref[ks, :], (((1,), (1,)), ((), ())), preferred_element_type=F32)
            s = s / math.sqrt(HD_C) + fq - fk_ref[:, ks]
            if j == qi:
                s = jnp.where(causal, s, NEG)
            s_ref[base + j] = s
            fold = _lane_fold(s, jnp.maximum)
            mx = fold if mx is None else jnp.maximum(mx, fold)
        m = jnp.max(mx, axis=-1, keepdims=True)
        acc = None
        for j in range(qi + 1):
            p = jnp.exp(s_ref[base + j] - m).astype(BF16)
            pv = jnp.dot(p, vb_ref[j * tq:(j + 1) * tq, :], preferred_element_type=F32)
            acc = pv if acc is None else acc + pv
        o_ref[rows, :] = (acc[:, 0:HD_C] / acc[:, HD_C:]).astype(o_ref.dtype)


def _fox_prompt(q, k, v, f_rows, f_heads):
    b, t, _ = q.shape
    tq = _tile(t, 512)
    nq = t // tq
    head_rows = pl.BlockSpec((None, t, HD_C), lambda bi, h: (bi, 0, h))
    return pl.pallas_call(
        functools.partial(_fox_kernel, tq=tq, t=t),
        name="fox_prompt",
        grid=(b, H_C),
        in_specs=[head_rows, head_rows, head_rows,
                  pl.BlockSpec((None, t, LANES), lambda bi, h: (bi, 0, 0)),
                  pl.BlockSpec((None, None, 1, t), lambda bi, h: (bi, h, 0, 0))],
        out_specs=head_rows,
        out_shape=jax.ShapeDtypeStruct((b, t, H_C * HD_C), BF16),
        scratch_shapes=[pltpu.VMEM(((nq * (nq + 1)) // 2, tq, tq), F32),
                        pltpu.VMEM((t, 2 * HD_C), BF16)],
        compiler_params=_params("arbitrary", "arbitrary"),
    )(q, k, v, f_rows, f_heads.reshape(b, H_C, 1, t))


def _fox_sample_kernel(q_ref, kc_ref, vc_ref, kn_ref, vn_ref, fct_ref, fnc_ref, fec_ref, fnt_ref,
                       o_ref, s_ref, p_ref, m_ref, l_ref, acc_ref, kd_ref, vd_ref, *, t, tk):
    j = pl.program_id(1)
    contract_last = (((1,), (1,)), ((), ()))

    @pl.when(j == 0)
    def _():
        m_ref[...] = jnp.full_like(m_ref, NEG)
        l_ref[...] = jnp.zeros_like(l_ref)
        acc_ref[...] = jnp.zeros_like(acc_ref)

    def head_rows(h):
        return slice(h * t, (h + 1) * t)

    def update(width):
        s = s_ref[:, 0:width]
        m_prev = m_ref[...]
        m_col = jnp.maximum(m_prev[:, 0:1], jnp.max(s, axis=-1, keepdims=True))
        m_new = jnp.broadcast_to(m_col, m_prev.shape)
        alpha = jnp.exp(m_prev - m_new)
        p = jnp.exp(s - m_col)
        l_ref[...] = alpha * l_ref[...] + jnp.sum(p, axis=-1, keepdims=True)
        acc_ref[...] = alpha * acc_ref[...]
        m_ref[...] = m_new
        p_ref[:, 0:width] = p.astype(BF16)

    step = LANES
    for c in range(tk // step):
        src = slice(c * step * H_C, (c + 1) * step * H_C)
        dst = slice(c * step, (c + 1) * step)
        kd_ref[:, dst, :] = pltpu.einshape("(ph)d->hpd", kc_ref[src, :], h=H_C).astype(BF16)
        vd_ref[:, dst, :] = pltpu.einshape("(ph)d->hpd", vc_ref[src, :], h=H_C).astype(BF16)
    fq_cache = fnc_ref[...] + fec_ref[...]
    for h in range(H_C):
        hs = slice(h * HD_C, (h + 1) * HD_C)
        s = lax.dot_general(q_ref[:, hs], kd_ref[h], contract_last, preferred_element_type=F32)
        s_ref[head_rows(h), :] = s / math.sqrt(HD_C) + fq_cache[head_rows(h)] - fct_ref[h:h + 1, :]
    update(tk)
    for h in range(H_C):
        acc_ref[head_rows(h), :] += jnp.dot(p_ref[head_rows(h), :], vd_ref[h], preferred_element_type=F32)

    @pl.when(j == pl.num_programs(1) - 1)
    def _():
        causal = (lax.broadcasted_iota(jnp.int32, (t, t), 1)
                  <= lax.broadcasted_iota(jnp.int32, (t, t), 0))
        fq_new = fnc_ref[...]
        for h in range(H_C):
            hs = slice(h * HD_C, (h + 1) * HD_C)
            s = lax.dot_general(q_ref[:, hs], kn_ref[:, hs].astype(BF16), contract_last,
                                preferred_element_type=F32)
            s = s / math.sqrt(HD_C) + fq_new[head_rows(h)] - fnt_ref[h:h + 1, :]
            s_ref[head_rows(h), 0:t] = jnp.where(causal, s, NEG)
        update(t)
        for h in range(H_C):
            hs = slice(h * HD_C, (h + 1) * HD_C)
            pv = jnp.dot(p_ref[head_rows(h), 0:t], vn_ref[:, hs].astype(BF16), preferred_element_type=F32)
            o_ref[:, hs] = ((acc_ref[head_rows(h), :] + pv) / l_ref[head_rows(h), :]).astype(o_ref.dtype)


def _fox_sample(q, k_new, v_new, k_cache, v_cache, layer, fc_heads, fn_heads):
    b, t, d = q.shape
    n_layers, _, p = k_cache.shape[:3]
    tk = _tile(p, 1024)
    rows = H_C * t
    fn_col = fn_heads.reshape(b, rows, 1)
    fc_end_col = jnp.broadcast_to(fc_heads[:, :, p - 1:], (b, H_C, t)).reshape(b, rows, 1)
    cache_spec = pl.BlockSpec((None, None, tk * H_C, HD_C), lambda bi, j: (layer, bi, j, 0))
    new_spec = pl.BlockSpec((None, t, d), lambda bi, j: (bi, 0, 0))
    col_spec = pl.BlockSpec((None, rows, 1), lambda bi, j: (bi, 0, 0))
    return pl.pallas_call(
        functools.partial(_fox_sample_kernel, t=t, tk=tk),
        name="fox_sample",
        grid=(b, p // tk),
        in_specs=[new_spec, cache_spec, cache_spec, new_spec, new_spec,
                  pl.BlockSpec((None, H_C, tk), lambda bi, j: (bi, 0, j)),
                  col_spec, col_spec,
                  pl.BlockSpec((None, H_C, t), lambda bi, j: (bi, 0, 0))],
        out_specs=new_spec,
        out_shape=jax.ShapeDtypeStruct((b, t, d), BF16),
        scratch_shapes=[pltpu.VMEM((rows, tk), F32), pltpu.VMEM((rows, tk), BF16),
                        pltpu.VMEM((rows, LANES), F32), pltpu.VMEM((rows, LANES), F32),
                        pltpu.VMEM((rows, HD_C), F32),
                        pltpu.VMEM((H_C, tk, HD_C), BF16), pltpu.VMEM((H_C, tk, HD_C), BF16)],
        compiler_params=_params("arbitrary", "arbitrary"),
    )(q, k_cache.reshape(n_layers, b, p * H_C, HD_C), v_cache.reshape(n_layers, b, p * H_C, HD_C),
      k_new, v_new, fc_heads, fn_col, fc_end_col, fn_heads)


def kernel(x_prompt, x_sample, c_prompt, c_sample, cache_conv, cache_win_k, cache_win_v, cache_fox_k, cache_fox_v, cache_fox_logf, ada_w, ada_b, norm_mix_g, norm_ffn_g, e_w_in, e_w_dw, e_b_dw, e_ln_g, e_ln_b, e_sinks, e_w_out, o_w_in, o_b_f, o_w_out, ffn_w_gate, ffn_w_up, ffn_w_down, final_g):
    bp, tp, d = x_prompt.shape
    bs, ts, _ = x_sample.shape
    depth = ada_w.shape[0]
    n_odd = o_w_in.shape[0]
    past_len = cache_fox_k.shape[2]
    nqkv = H_C * HD_C

    mod = _adaln(jnp.concatenate([c_prompt, c_sample], axis=0), ada_w, ada_b)
    tab_p = _rope_tables(jnp.arange(tp))
    tab_s = tuple(jnp.tile(x, (bs, 1)) for x in _rope_tables(past_len + jnp.arange(ts)))

    xp = x_prompt.reshape(bp * tp, d)
    xs = x_sample.reshape(bs * ts, d)
    hp = _modulate(x_prompt, norm_mix_g[0], mod, 0, 0, 0).reshape(bp * tp, d)
    hs = _modulate(x_sample, norm_mix_g[0], mod, 0, 0, bp).reshape(bs * ts, d)
    outs = {name: [] for name in ("conv_p", "wk_p", "wv_p", "fl_p",
                                  "conv_s", "wk_s", "wv_s", "fk_s", "fv_s", "fl_s")}
    fk_p = fv_p = None
    e_in, e_out = e_w_in.astype(BF16), e_w_out.astype(BF16)
    o_in, o_out = o_w_in.astype(BF16), o_w_out.astype(BF16)
    wg, wu, wd = ffn_w_gate.astype(BF16), ffn_w_up.astype(BF16), ffn_w_down.astype(BF16)
    for l in range(depth):
        if l % 2 == 0:
            e = l // 2
            w_out, w_out_layer = e_out, e
            conv = (e_w_dw[e], e_b_dw[e], e_ln_g[e], e_ln_b[e])
            zp = _even_proj(hp, e_in, e, tab_p, tp).reshape(bp, tp, E_IN)
            cat, cst = _even_mix(zp, e_sinks[e], *conv)
            win = min(WINDOW, tp)
            outs["conv_p"].append(cst[:, HIST_PAD - (CONV_W - 1):])
            outs["wk_p"].append(zp[:, tp - win:, K_OFF:V_OFF].reshape(bp, win, KV_B, HD_B))
            outs["wv_p"].append(zp[:, tp - win:, V_OFF:].reshape(bp, win, KV_B, HD_B))
            yp = cat.reshape(bp * tp, d)

            zs = _even_proj(hs, e_in, e, tab_s, bs * ts).reshape(bs, ts, E_IN)
            hist = jnp.pad(cache_conv[e], ((0, 0), (HIST_PAD - (CONV_W - 1), 0), (0, 0)))
            cat, cst = _even_mix_sample(zs, hist, cache_win_k[e], cache_win_v[e], e_sinks[e], *conv)
            outs["conv_s"].append(cst[:, HIST_PAD - (CONV_W - 1):])
            outs["wk_s"].append(zs[:, :, K_OFF:V_OFF].reshape(bs, ts, KV_B, HD_B))
            outs["wv_s"].append(zs[:, :, V_OFF:].reshape(bs, ts, KV_B, HD_B))
            ys = cat.reshape(bs * ts, d)
        else:
            o = l // 2
            wf = jnp.pad(o_w_in[o][:, 3 * nqkv:], ((0, 0), (0, LANES - H_C))).astype(BF16)[None]
            bf = jnp.pad(o_b_f[o], (0, LANES - H_C)).reshape(1, LANES)
            w_out, w_out_layer = o_out, o

            q = _matmul(hp, o_in, o, 0, nqkv, BF16).reshape(bp, tp, nqkv)
            k, fk_p = _kv_proj(hp, o_in, o, nqkv, fk_p, o, n_odd)
            v, fv_p = _kv_proj(hp, o_in, o, 2 * nqkv, fv_p, o, n_odd)
            logit = _matmul(hp, wf, 0, 0, LANES, F32).reshape(bp, tp, LANES)
            lf, f_rows, f_heads = _forget_scan(logit, bf, True)
            outs["fl_p"].append(lf)
            yp = _fox_prompt(q, k.reshape(bp, tp, nqkv), v.reshape(bp, tp, nqkv),
                             f_rows, f_heads).reshape(bp * tp, d)

            q = _matmul(hs, o_in, o, 0, nqkv, BF16).reshape(bs, ts, nqkv)
            k = _matmul(hs, o_in, o, nqkv, nqkv, F32).reshape(bs, ts, nqkv)
            v = _matmul(hs, o_in, o, 2 * nqkv, nqkv, F32).reshape(bs, ts, nqkv)
            logit = _matmul(hs, wf, 0, 0, LANES, F32).reshape(bs, ts, LANES)
            lf, _, fn_heads = _forget_scan(logit, bf, True)
            cache_lf = jnp.pad(cache_fox_logf[o], ((0, 0), (0, 0), (0, LANES - H_C)))
            _, _, fc_heads = _forget_scan(cache_lf, bf, False)
            outs["fk_s"].append(k.reshape(bs, ts, H_C, HD_C))
            outs["fv_s"].append(v.reshape(bs, ts, H_C, HD_C))
            outs["fl_s"].append(lf)
            ys = _fox_sample(q, k, v, cache_fox_k, cache_fox_v, o, fc_heads,
                             fn_heads).reshape(bs * ts, d)

        xp, hp = _residual_matmul(yp, w_out, w_out_layer, xp, norm_ffn_g[l], mod, l, tp, 0)
        xs, hs = _residual_matmul(ys, w_out, w_out_layer, xs, norm_ffn_g[l], mod, l, ts, bp)

        if l + 1 < depth:
            xp, hp = _ffn(hp, xp, wg, wu, wd, norm_mix_g[l + 1], mod, l, tp, 0, False)
            xs, hs = _ffn(hs, xs, wg, wu, wd, norm_mix_g[l + 1], mod, l, ts, bp, False)
        else:
            y_prompt = _ffn(hp, xp, wg, wu, wd, final_g, mod, l, tp, 0, True).reshape(bp, tp, d)
            y_sample = _ffn(hs, xs, wg, wu, wd, final_g, mod, l, ts, bp, True).reshape(bs, ts, d)

    stack = lambda name: jnp.stack(outs[name])
    return (y_prompt, y_sample,
            stack("conv_p"), stack("wk_p"), stack("wv_p"),
            fk_p.reshape(n_odd, bp, tp, H_C, HD_C), fv_p.reshape(n_odd, bp, tp, H_C, HD_C), stack("fl_p"),
            stack("conv_s"), stack("wk_s"), stack("wv_s"), stack("fk_s"), stack("fv_s"), stack("fl_s"))
```

```python
import functools
import math

import jax
import jax.numpy as jnp
from jax import lax
from jax.experimental import pallas as pl
from jax.experimental.pallas import tpu as pltpu

F32 = jnp.float32
BF16 = jnp.bfloat16

D_MODEL = 2048
CHUNK = 64
C_CONV = D_MODEL // 2
CONV_W = 31
HIST_PAD = 32
HD_B = 64
H_B = (D_MODEL // 2) // HD_B
KV_B = H_B // 4
G_B = H_B // KV_B
WINDOW = 128
E_IN = 2 * C_CONV + H_B * HD_B + 2 * KV_B * HD_B
Q_OFF = 2 * C_CONV
K_OFF = Q_OFF + H_B * HD_B
V_OFF = K_OFF + KV_B * HD_B
HD_C = 128
H_C = D_MODEL // HD_C
ROPE_THETA = 10000.0
EPS = 1e-6
NEG = -1e30
LANES = 128
SUBLANES = 8
VMEM_LIMIT = 56 * 1024 * 1024


def _params(*sem):
    return pltpu.CompilerParams(dimension_semantics=sem, vmem_limit_bytes=VMEM_LIMIT)


def _tile(n, pref):
    t = min(n, pref)
    while n % t:
        t //= 2
    return t


def _bcast_rows(vec, rows):
    nb, _, w = vec.shape
    if nb == 1:
        return vec[0]
    return jnp.broadcast_to(vec, (nb, rows // nb, w)).reshape(rows, w)


def _mod_spec(layer, chunk, nb, width, index_fn):
    return pl.BlockSpec((None, None, nb, 1, width),
                        lambda *g: (layer, chunk) + tuple(index_fn(*g)))


def _row_tiling(m, rows_per_batch, pref):
    if m > rows_per_batch and m > pref:
        tm = _tile(rows_per_batch, pref)
        return tm, 1, lambda i, row_off: (i * tm) // rows_per_batch + row_off
    nb = max(m // rows_per_batch, 1)
    return m, nb, lambda i, row_off: row_off // nb + i


def _rms_scale(x):
    return lax.rsqrt(jnp.mean(x * x, axis=-1, keepdims=True) + EPS)


def _adaln_kernel(c_ref, w_ref, b_ref, o_ref):
    c = c_ref[...]
    a = (c * jax.nn.sigmoid(c)).astype(BF16)
    o_ref[...] = jnp.dot(a, w_ref[...].astype(BF16), preferred_element_type=F32) + b_ref[...]


def _adaln(c_all, ada_w, ada_b):
    n_layers, d, n = ada_w.shape
    nb = c_all.shape[0]
    tn = 1024
    per = d // tn
    out = pl.pallas_call(
        _adaln_kernel,
        name="adaln",
        grid=(n_layers, n // tn),
        in_specs=[pl.BlockSpec((nb, d), lambda l, j: (0, 0)),
                  pl.BlockSpec((None, d, tn), lambda l, j: (l, 0, j)),
                  pl.BlockSpec((None, 1, tn), lambda l, j: (l, 0, j))],
        out_specs=pl.BlockSpec((None, None, nb, tn), lambda l, j: (l, j // per, 0, j % per)),
        out_shape=jax.ShapeDtypeStruct((n_layers, n // d, nb, d), F32),
        compiler_params=_params("arbitrary", "arbitrary"),
    )(c_all, ada_w, ada_b.reshape(n_layers, 1, n))
    return out.reshape(n_layers, n // d, nb, 1, d)


def _modulate_kernel(x_ref, g_ref, sh_ref, sc_ref, o_ref):
    x = x_ref[...]
    y = x * _rms_scale(x) * (g_ref[...] * (1.0 + sc_ref[0]))
    o_ref[...] = (y + sh_ref[0]).astype(o_ref.dtype)


def _modulate(x, g, mod, layer, chunk, row_off):
    b, t, d = x.shape
    tt = _tile(t, 512)
    return pl.pallas_call(
        _modulate_kernel,
        name="modulate",
        grid=(b, t // tt),
        in_specs=[pl.BlockSpec((None, tt, d), lambda bi, i: (bi, i, 0)),
                  pl.BlockSpec((1, d), lambda bi, i: (0, 0)),
                  _mod_spec(layer, chunk, 1, d, lambda bi, i: (bi + row_off, 0, 0)),
                  _mod_spec(layer, chunk + 1, 1, d, lambda bi, i: (bi + row_off, 0, 0))],
        out_specs=pl.BlockSpec((None, tt, d), lambda bi, i: (bi, i, 0)),
        out_shape=jax.ShapeDtypeStruct((b, t, d), BF16),
        compiler_params=_params("arbitrary", "arbitrary"),
    )(x, g.reshape(1, d), mod, mod)


def _mm_kernel(a_ref, w_ref, o_ref):
    o_ref[...] = jnp.dot(a_ref[...], w_ref[...], preferred_element_type=F32).astype(o_ref.dtype)


def _matmul(a, w, layer, col0, n, out_dtype):
    m, k = a.shape
    tm, tn = _tile(m, 1024), _tile(n, 1024)
    assert col0 % tn == 0
    return pl.pallas_call(
        _mm_kernel,
        name="matmul",
        grid=(m // tm, n // tn),
        in_specs=[pl.BlockSpec((tm, k), lambda i, j: (i, 0)),
                  pl.BlockSpec((None, k, tn), lambda i, j: (layer, 0, col0 // tn + j))],
        out_specs=pl.BlockSpec((tm, tn), lambda i, j: (i, j)),
        out_shape=jax.ShapeDtypeStruct((m, n), out_dtype),
        compiler_params=_params("arbitrary", "arbitrary"),
    )(a, w)


def _kv_proj_kernel(a_ref, w_ref, *refs, tm, tn):
    rows_ref, heads_ref, y_ref = refs[-3:]
    a = a_ref[...]
    for j in range(w_ref.shape[1] // tn):
        y = jnp.dot(a, w_ref[:, j * tn:(j + 1) * tn], preferred_element_type=F32)
        rows_ref[:, j * tn:(j + 1) * tn] = y.astype(rows_ref.dtype)
        y_ref[:, j * tn:(j + 1) * tn] = y
    step = 8 * SUBLANES
    for r in range(0, tm, step):
        heads_ref[r * H_C:(r + step) * H_C, :] = pltpu.einshape(
            "t(hd)->(th)d", y_ref[r:r + step, :], h=H_C)


def _kv_proj(a, w, w_layer, col0, stacked, layer, n_layers):
    m, k = a.shape
    n = H_C * HD_C
    assert col0 % n == 0
    tm = _tile(m, 512)
    in_specs = [pl.BlockSpec((tm, k), lambda i: (i, 0)),
                pl.BlockSpec((None, k, n), lambda i: (w_layer, 0, col0 // n))]
    args = [a, w]
    aliases = {}
    if stacked is not None:
        in_specs.append(pl.BlockSpec(memory_space=pl.ANY))
        args.append(stacked)
        aliases = {2: 1}
    return pl.pallas_call(
        functools.partial(_kv_proj_kernel, tm=tm, tn=512),
        name="kv_proj",
        grid=(m // tm,),
        in_specs=in_specs,
        out_specs=[pl.BlockSpec((tm, n), lambda i: (i, 0)),
                   pl.BlockSpec((None, tm * H_C, HD_C), lambda i: (layer, i, 0))],
        out_shape=[jax.ShapeDtypeStruct((m, n), BF16),
                   jax.ShapeDtypeStruct((n_layers, m * H_C, HD_C), F32)],
        input_output_aliases=aliases,
        scratch_shapes=[pltpu.VMEM((tm, n), F32)],
        compiler_params=_params("arbitrary"),
    )(*args)


def _rope_tables(pos):
    half = HD_B // 2
    freqs = ROPE_THETA ** (-jnp.arange(half, dtype=F32) / half)
    ang = pos.astype(F32)[:, None] * freqs[None, :]
    cos, sin = jnp.cos(ang), jnp.sin(ang)
    zero = jnp.zeros_like(sin)
    rep = LANES // HD_B
    cos_t = jnp.tile(jnp.concatenate([cos, cos], axis=1), (1, rep))
    sa = jnp.tile(jnp.concatenate([-sin, zero], axis=1), (1, rep))
    sb = jnp.tile(jnp.concatenate([zero, sin], axis=1), (1, rep))
    return cos_t, sa, sb


def _rope(x, cos, sa, sb):
    half = HD_B // 2
    outs = []
    for g in range(x.shape[1] // LANES):
        xg = x[:, g * LANES:(g + 1) * LANES]
        outs.append(xg * cos + pltpu.roll(xg, LANES - half, 1) * sa + pltpu.roll(xg, half, 1) * sb)
    return outs[0] if len(outs) == 1 else jnp.concatenate(outs, axis=1)


def _even_proj_kernel(a_ref, w_ref, cos_ref, sa_ref, sb_ref, o_ref, *, tn):
    a = a_ref[...]
    kw = KV_B * HD_B
    for j in range(E_IN // tn):
        c0 = j * tn
        z = jnp.dot(a, w_ref[:, c0:c0 + tn], preferred_element_type=F32)
        if c0 + tn <= Q_OFF:
            o_ref[:, c0:c0 + tn] = z
        elif c0 + tn <= K_OFF:
            o_ref[:, c0:c0 + tn] = _rope(z, cos_ref[...], sa_ref[...], sb_ref[...])
        else:
            o_ref[:, c0:c0 + kw] = _rope(z[:, :kw], cos_ref[...], sa_ref[...], sb_ref[...])
            o_ref[:, c0 + kw:c0 + tn] = z[:, kw:]


def _even_proj(a, w, layer, tables, period):
    m, k = a.shape
    n = w.shape[2]
    tn = 2 * KV_B * HD_B
    assert Q_OFF % tn == 0 and K_OFF % tn == 0 and n == K_OFF + tn
    tm = _tile(period, 512) if m > period else m
    per_blocks = max(period // tm, 1)
    tab_spec = pl.BlockSpec((tm, LANES), lambda i: (i % per_blocks, 0))
    return pl.pallas_call(
        functools.partial(_even_proj_kernel, tn=tn),
        name="even_proj",
        grid=(m // tm,),
        in_specs=[pl.BlockSpec((tm, k), lambda i: (i, 0)),
                  pl.BlockSpec((None, k, n), lambda i: (layer, 0, 0), pipeline_mode=pl.Buffered(1)),
                  tab_spec, tab_spec, tab_spec],
        out_specs=pl.BlockSpec((tm, n), lambda i: (i, 0)),
        out_shape=jax.ShapeDtypeStruct((m, n), F32),
        compiler_params=_params("arbitrary"),
    )(a, w, *tables)


def _residual_kernel(a_ref, w_ref, x_ref, gate_ref, g_ref, sh_ref, sc_ref, xo_ref, ho_ref, *, tn):
    tm, d = xo_ref.shape
    a = a_ref[...]
    gate = _bcast_rows(gate_ref[...], tm)
    ss = jnp.zeros((tm, 1), F32)
    for j in range(d // tn):
        cs = slice(j * tn, (j + 1) * tn)
        xn = x_ref[:, cs] + gate[:, cs] * jnp.dot(a, w_ref[:, cs], preferred_element_type=F32)
        xo_ref[:, cs] = xn
        ss = ss + jnp.sum(xn * xn, axis=-1, keepdims=True)
    gs = _bcast_rows(g_ref[...] * (1.0 + sc_ref[...]), tm)
    h = xo_ref[...] * lax.rsqrt(ss / d + EPS) * gs + _bcast_rows(sh_ref[...], tm)
    ho_ref[...] = h.astype(ho_ref.dtype)


def _residual_matmul(a, w, w_layer, x, g, mod, layer, rows_per_batch, row_off):
    m, k = a.shape
    d = w.shape[2]
    tm, nb, row_fn = _row_tiling(m, rows_per_batch, 512)
    vec = lambda chunk: _mod_spec(layer, chunk, nb, d, lambda i: (row_fn(i, row_off), 0, 0))
    return pl.pallas_call(
        functools.partial(_residual_kernel, tn=512),
        name="residual",
        grid=(m // tm,),
        in_specs=[pl.BlockSpec((tm, k), lambda i: (i, 0)),
                  pl.BlockSpec((None, k, d), lambda i: (w_layer, 0, 0)),
                  pl.BlockSpec((tm, d), lambda i: (i, 0)),
                  vec(2),
                  pl.BlockSpec((1, 1, d), lambda i: (0, 0, 0)),
                  vec(3), vec(4)],
        out_specs=[pl.BlockSpec((tm, d), lambda i: (i, 0)),
                   pl.BlockSpec((tm, d), lambda i: (i, 0))],
        out_shape=[jax.ShapeDtypeStruct((m, d), F32), jax.ShapeDtypeStruct((m, d), BF16)],
        compiler_params=_params("arbitrary"),
    )(a, w, x, mod, g.reshape(1, 1, d), mod, mod)


def _ffn_kernel(h_ref, x_ref, gate_ref, wg_ref, wu_ref, wd_ref, g_ref, *refs, final, rows):
    if final:
        y_ref, acc_ref = refs
    else:
        sh_ref, sc_ref, xo_ref, ho_ref, acc_ref = refs
    f = pl.program_id(1)
    tm, d = acc_ref.shape

    @pl.when(f == 0)
    def _():
        acc_ref[...] = jnp.zeros_like(acc_ref)

    h = h_ref[...]
    g = jnp.dot(h, wg_ref[...], preferred_element_type=F32)
    u = jnp.dot(h, wu_ref[...], preferred_element_type=F32)
    a = (g * jax.nn.sigmoid(g) * u).astype(BF16)
    acc_ref[...] += jnp.dot(a, wd_ref[...], preferred_element_type=F32)

    @pl.when(f == pl.num_programs(1) - 1)
    def _():
        nb = gate_ref.shape[0]
        per = tm // nb
        for r0 in range(0, tm, rows):
            rs = slice(r0, r0 + rows)
            b = r0 // per
            xn = x_ref[rs, :] + gate_ref[b] * acc_ref[rs, :]
            if final:
                y_ref[rs, :] = xn * _rms_scale(xn) * g_ref[0]
            else:
                xo_ref[rs, :] = xn
                hn = xn * _rms_scale(xn) * (g_ref[0] * (1.0 + sc_ref[b])) + sh_ref[b]
                ho_ref[rs, :] = hn.astype(ho_ref.dtype)


def _ffn(h, x, wg, wu, wd, g_next, mod, layer, rows_per_batch, row_off, final):
    m, d = h.shape
    ff = wg.shape[2]
    tf = _tile(ff, 512)
    tm, nb, row_fn = _row_tiling(m, rows_per_batch, 512)
    vec = lambda lyr, chunk: _mod_spec(lyr, chunk, nb, d, lambda i, f: (row_fn(i, row_off), 0, 0))
    rows_spec = pl.BlockSpec((tm, d), lambda i, f: (i, 0))
    in_specs = [rows_spec, rows_spec, vec(layer, 5),
                pl.BlockSpec((None, d, tf), lambda i, f: (layer, 0, f)),
                pl.BlockSpec((None, d, tf), lambda i, f: (layer, 0, f)),
                pl.BlockSpec((None, tf, d), lambda i, f: (layer, f, 0)),
                pl.BlockSpec((1, 1, d), lambda i, f: (0, 0, 0))]
    args = [h, x, mod, wg, wu, wd, g_next.reshape(1, 1, d)]
    if final:
        out_specs = rows_spec
        out_shape = jax.ShapeDtypeStruct((m, d), F32)
    else:
        in_specs += [vec(layer + 1, 0), vec(layer + 1, 1)]
        args += [mod, mod]
        out_specs = [rows_spec, rows_spec]
        out_shape = [jax.ShapeDtypeStruct((m, d), F32), jax.ShapeDtypeStruct((m, d), BF16)]
    return pl.pallas_call(
        functools.partial(_ffn_kernel, final=final, rows=2 * SUBLANES),
        name="ffn",
        grid=(m // tm, ff // tf),
        in_specs=in_specs,
        out_specs=out_specs,
        out_shape=out_shape,
        scratch_shapes=[pltpu.VMEM((tm, d), F32)],
        compiler_params=_params("arbitrary", "arbitrary"),
    )(*args)


def _glu_rows(z_ref, uext_ref, rows):
    step = min(rows, 4 * SUBLANES)
    for r in range(0, rows, step):
        uext_ref[HIST_PAD + r:HIST_PAD + r + step, :] = (
            z_ref[r:r + step, 0:C_CONV] * jax.nn.sigmoid(z_ref[r:r + step, C_CONV:2 * C_CONV]))


def _conv_ln_silu(uext_ref, wdw_ref, bdw_ref, lng_ref, lnb_ref, y_ref, shift_ref, out_ref, rows):
    base = HIST_PAD - (CONV_W - 1)
    for c in range(C_CONV // LANES):
        cs = slice(c * LANES, (c + 1) * LANES)
        for r in range(SUBLANES):
            n = rows + SUBLANES * ((CONV_W - 1 - r) // SUBLANES)
            shift_ref[r, 0:n, :] = uext_ref[base + r:base + r + n, cs]
        rb = min(rows, 8 * SUBLANES)

        def row_block(i, carry):
            t0 = pl.multiple_of(i * rb, rb)
            acc = jnp.zeros((rb, LANES), F32)
            for j in range(CONV_W):
                r, a = j % SUBLANES, j // SUBLANES
                acc = acc + wdw_ref[j:j + 1, cs] * shift_ref[r, pl.ds(t0 + SUBLANES * a, rb), :]
            y_ref[pl.ds(t0, rb), cs] = acc + bdw_ref[:, cs]
            return carry

        lax.fori_loop(0, rows // rb, row_block, 0)
    step = min(rows, 4 * SUBLANES)
    for r in range(0, rows, step):
        y = y_ref[r:r + step, :]
        yc = y - jnp.mean(y, axis=-1, keepdims=True)
        var = jnp.mean(yc * yc, axis=-1, keepdims=True)
        yn = yc * lax.rsqrt(var + EPS) * lng_ref[...] + lnb_ref[...]
        out_ref[r:r + step, 0:C_CONV] = (yn * jax.nn.sigmoid(yn)).astype(out_ref.dtype)


def _sink_attention(q, k, v, sink_col, mask):
    s = lax.dot_general(q, k, (((1,), (1,)), ((), ())), preferred_element_type=F32) / math.sqrt(HD_B)
    if mask is not None:
        s = jnp.where(mask, s, NEG)
    m = jnp.maximum(jnp.max(s, axis=-1, keepdims=True), sink_col)
    p = jnp.exp(s - m)
    den = jnp.sum(p, axis=-1, keepdims=True) + jnp.exp(sink_col - m)
    return jnp.dot(p.astype(BF16), v, preferred_element_type=F32) / den


def _sink_column(sink_ref, kv_head, rows):
    return jnp.concatenate(
        [jnp.full((rows, 1), sink_ref[kv_head * G_B + g], F32) for g in range(G_B)], axis=0)


def _group_attention(sink_ref, z_ref, kall, vall, r0, rows, mask, cat_ref):
    for kh in range(KV_B):
        hs = slice(kh * HD_B, (kh + 1) * HD_B)
        q = jnp.concatenate(
            [z_ref[r0:r0 + rows, Q_OFF + (kh * G_B + g) * HD_B:Q_OFF + (kh * G_B + g + 1) * HD_B]
             for g in range(G_B)], axis=0).astype(BF16)
        o = _sink_attention(q, kall[:, hs], vall[:, hs], _sink_column(sink_ref, kh, rows), mask)
        for gp in range(G_B // 2):
            pair = jnp.concatenate([o[(2 * gp) * rows:(2 * gp + 1) * rows],
                                    o[(2 * gp + 1) * rows:(2 * gp + 2) * rows]], axis=1)
            c0 = C_CONV + (kh * G_B + 2 * gp) * HD_B
            cat_ref[r0:r0 + rows, c0:c0 + 2 * HD_B] = pair.astype(cat_ref.dtype)


def _even_mix_kernel(sink_ref, zc_ref, zh_ref, zkv_ref, wdw_ref, bdw_ref, lng_ref, lnb_ref,
                     cat_ref, cst_ref, uext_ref, y_ref, shift_ref, *, tt):
    i = pl.program_id(1)
    first = i == 0

    _glu_rows(zc_ref, uext_ref, tt)
    hist = zh_ref[:, 0:C_CONV] * jax.nn.sigmoid(zh_ref[:, C_CONV:2 * C_CONV])
    uext_ref[0:HIST_PAD, :] = jnp.where(first, 0.0, hist)
    _conv_ln_silu(uext_ref, wdw_ref, bdw_ref, lng_ref, lnb_ref, y_ref, shift_ref, cat_ref, tt)

    @pl.when(i == pl.num_programs(1) - 1)
    def _():
        cst_ref[...] = uext_ref[tt:tt + HIST_PAD, :]

    kw = KV_B * HD_B
    sub = 2 * CHUNK
    kext = jnp.concatenate([zkv_ref[:, 0:kw], zc_ref[:, K_OFF:K_OFF + kw]], axis=0).astype(BF16)
    v_t = jnp.concatenate([zkv_ref[:, kw:2 * kw], zc_ref[:, V_OFF:V_OFF + kw]], axis=0).T.astype(BF16)
    cols = G_B * sub
    kc = lax.broadcasted_iota(jnp.int32, (2 * sub, cols), 0) // CHUNK
    qc = (lax.broadcasted_iota(jnp.int32, (2 * sub, cols), 1) % sub) // CHUNK
    band = jnp.logical_and(kc >= qc, kc <= qc + WINDOW // CHUNK)
    band_first = jnp.logical_and(band, jnp.logical_or(jnp.logical_not(first), kc >= WINDOW // CHUNK))
    for s in range(tt // sub):
        r0 = s * sub
        mask = band_first if s == 0 else band
        for kh in range(KV_B):
            hs = slice(kh * HD_B, (kh + 1) * HD_B)
            q = jnp.concatenate(
                [zc_ref[r0:r0 + sub, Q_OFF + (kh * G_B + g) * HD_B:Q_OFF + (kh * G_B + g + 1) * HD_B]
                 for g in range(G_B)], axis=0).astype(BF16)
            logit = lax.dot_general(kext[r0:r0 + 2 * sub, hs], q, (((1,), (1,)), ((), ())),
                                    preferred_element_type=F32) / math.sqrt(HD_B)
            logit = jnp.where(mask, logit, NEG)
            sink = jnp.concatenate(
                [jnp.full((1, sub), sink_ref[kh * G_B + g], F32) for g in range(G_B)], axis=1)
            m = jnp.maximum(jnp.max(logit, axis=0, keepdims=True), sink)
            p = jnp.exp(logit - m)
            den = jnp.sum(p, axis=0, keepdims=True) + jnp.exp(sink - m)
            o_t = jnp.dot(v_t[hs, r0:r0 + 2 * sub], p.astype(BF16), preferred_element_type=F32) / den
            for gp in range(G_B // 2):
                pair_t = jnp.concatenate([o_t[:, (2 * gp) * sub:(2 * gp + 1) * sub],
                                          o_t[:, (2 * gp + 1) * sub:(2 * gp + 2) * sub]], axis=0)
                c0 = C_CONV + (kh * G_B + 2 * gp) * HD_B
                cat_ref[r0:r0 + sub, c0:c0 + 2 * HD_B] = pair_t.T.astype(cat_ref.dtype)


def _even_mix(z, sinks, w_dw, b_dw, ln_g, ln_b):
    b, t, _ = z.shape
    tt = _tile(t, 512)
    hb = tt // HIST_PAD
    kb = tt // WINDOW
    kvw = 2 * KV_B * HD_B
    vec = pl.BlockSpec((1, C_CONV), lambda bi, i: (0, 0))
    return pl.pallas_call(
        functools.partial(_even_mix_kernel, tt=tt),
        name="even_mix",
        grid=(b, t // tt),
        in_specs=[pl.BlockSpec(memory_space=pltpu.SMEM),
                  pl.BlockSpec((None, tt, E_IN), lambda bi, i: (bi, i, 0)),
                  pl.BlockSpec((None, HIST_PAD, 2 * C_CONV),
                               lambda bi, i: (bi, jnp.maximum(i * hb - 1, 0), 0)),
                  pl.BlockSpec((None, WINDOW, kvw),
                               lambda bi, i: (bi, jnp.maximum(i * kb - 1, 0), K_OFF // kvw)),
                  pl.BlockSpec((CONV_W, C_CONV), lambda bi, i: (0, 0)),
                  vec, vec, vec],
        out_specs=[pl.BlockSpec((None, tt, D_MODEL), lambda bi, i: (bi, i, 0)),
                   pl.BlockSpec((None, HIST_PAD, C_CONV), lambda bi, i: (bi, 0, 0))],
        out_shape=[jax.ShapeDtypeStruct((b, t, D_MODEL), BF16),
                   jax.ShapeDtypeStruct((b, HIST_PAD, C_CONV), F32)],
        scratch_shapes=[pltpu.VMEM((HIST_PAD + tt, C_CONV), F32),
                        pltpu.VMEM((tt, C_CONV), F32),
                        pltpu.VMEM((SUBLANES, tt + HIST_PAD, LANES), F32)],
        compiler_params=_params("arbitrary", "arbitrary"),
    )(sinks, z, z, z, w_dw, b_dw.reshape(1, -1), ln_g.reshape(1, -1), ln_b.reshape(1, -1))


def _even_mix_sample_kernel(sink_ref, z_ref, hist_ref, kc_ref, vc_ref, wdw_ref, bdw_ref, lng_ref,
                            lnb_ref, cat_ref, cst_ref, uext_ref, y_ref, shift_ref, *, t):
    uext_ref[0:HIST_PAD, :] = hist_ref[...]
    _glu_rows(z_ref, uext_ref, t)
    _conv_ln_silu(uext_ref, wdw_ref, bdw_ref, lng_ref, lnb_ref, y_ref, shift_ref, cat_ref, t)
    cst_ref[...] = uext_ref[t:t + HIST_PAD, :]

    kw = KV_B * HD_B
    kall = jnp.concatenate([kc_ref[...], z_ref[:, K_OFF:K_OFF + kw]], axis=0).astype(BF16)
    vall = jnp.concatenate([vc_ref[...], z_ref[:, V_OFF:V_OFF + kw]], axis=0).astype(BF16)
    _group_attention(sink_ref, z_ref, kall, vall, 0, t, None, cat_ref)


def _even_mix_sample(z, hist, k_cache, v_cache, sinks, w_dw, b_dw, ln_g, ln_b):
    b, t, _ = z.shape
    win = k_cache.shape[1]
    kw = KV_B * HD_B
    vec = pl.BlockSpec((1, C_CONV), lambda bi: (0, 0))
    return pl.pallas_call(
        functools.partial(_even_mix_sample_kernel, t=t),
        name="even_mix_sample",
        grid=(b,),
        in_specs=[pl.BlockSpec(memory_space=pltpu.SMEM),
                  pl.BlockSpec((None, t, E_IN), lambda bi: (bi, 0, 0)),
                  pl.BlockSpec((None, HIST_PAD, C_CONV), lambda bi: (bi, 0, 0)),
                  pl.BlockSpec((None, win, kw), lambda bi: (bi, 0, 0)),
                  pl.BlockSpec((None, win, kw), lambda bi: (bi, 0, 0)),
                  pl.BlockSpec((CONV_W, C_CONV), lambda bi: (0, 0)),
                  vec, vec, vec],
        out_specs=[pl.BlockSpec((None, t, D_MODEL), lambda bi: (bi, 0, 0)),
                   pl.BlockSpec((None, HIST_PAD, C_CONV), lambda bi: (bi, 0, 0))],
        out_shape=[jax.ShapeDtypeStruct((b, t, D_MODEL), BF16),
                   jax.ShapeDtypeStruct((b, HIST_PAD, C_CONV), F32)],
        scratch_shapes=[pltpu.VMEM((HIST_PAD + t, C_CONV), F32),
                        pltpu.VMEM((t, C_CONV), F32),
                        pltpu.VMEM((SUBLANES, t + HIST_PAD, LANES), F32)],
        compiler_params=_params("arbitrary"),
    )(sinks, z, hist, k_cache.reshape(b, win, kw), v_cache.reshape(b, win, kw),
      w_dw, b_dw.reshape(1, -1), ln_g.reshape(1, -1), ln_b.reshape(1, -1))


def _scan_kernel(x_ref, b_ref, lf_ref, f_ref, ft_ref, carry_ref, *, tt, log_sigmoid):
    i = pl.program_id(1)

    @pl.when(i == 0)
    def _():
        carry_ref[...] = jnp.zeros_like(carry_ref)

    x = x_ref[...]
    if log_sigmoid:
        x = x + b_ref[...]
        x = jnp.minimum(x, 0.0) - jnp.log1p(jnp.exp(-jnp.abs(x)))
    lf_ref[...] = x[:, :H_C]
    row = lax.broadcasted_iota(jnp.int32, x.shape, 0)
    step = 1
    while step < tt:
        x = x + jnp.where(row >= step, pltpu.roll(x, step, 0), 0.0)
        step *= 2
    f = x + carry_ref[...]
    carry_ref[...] = f[tt - 1:tt, :]
    f_ref[...] = f
    if tt < LANES:
        f = jnp.concatenate([f, jnp.zeros((LANES - tt, LANES), F32)], axis=0)
    ft_ref[...] = f.T[:H_C, :tt]


def _forget_scan(x, bias, log_sigmoid):
    b, t, _ = x.shape
    tt = _tile(t, 512)
    return pl.pallas_call(
        functools.partial(_scan_kernel, tt=tt, log_sigmoid=log_sigmoid),
        name="forget_scan",
        grid=(b, t // tt),
        in_specs=[pl.BlockSpec((None, tt, LANES), lambda bi, i: (bi, i, 0)),
                  pl.BlockSpec((1, LANES), lambda bi, i: (0, 0))],
        out_specs=[pl.BlockSpec((None, tt, H_C), lambda bi, i: (bi, i, 0)),
                   pl.BlockSpec((None, tt, LANES), lambda bi, i: (bi, i, 0)),
                   pl.BlockSpec((None, H_C, tt), lambda bi, i: (bi, 0, i))],
        out_shape=[jax.ShapeDtypeStruct((b, t, H_C), F32),
                   jax.ShapeDtypeStruct((b, t, LANES), F32),
                   jax.ShapeDtypeStruct((b, H_C, t), F32)],
        scratch_shapes=[pltpu.VMEM((1, LANES), F32)],
        compiler_params=_params("arbitrary", "arbitrary"),
    )(x, bias)


def _lane_fold(x, op):
    out = x[:, 0:LANES]
    for c in range(1, x.shape[1] // LANES):
        out = op(out, x[:, c * LANES:(c + 1) * LANES])
    return out


def _fox_kernel(q_ref, k_ref, v_ref, fq_ref, fk_ref, o_ref, s_ref, vb_ref, *, tq, t):
    h = pl.program_id(1)
    vb_ref[:, 0:HD_C] = v_ref[...]
    vb_ref[:, HD_C:] = jnp.ones((t, HD_C), BF16)
    lane = lax.broadcasted_iota(jnp.int32, (tq, LANES), 1)
    causal = (lax.broadcasted_iota(jnp.int32, (tq, tq), 1)
              <= lax.broadcasted_iota(jnp.int32, (tq, tq), 0))
    for qi in reversed(range(t // tq)):
        rows = slice(qi * tq, (qi + 1) * tq)
        base = (qi * (qi + 1)) // 2
        q = q_ref[rows, :]
        fq = jnp.sum(jnp.where(lane == h, fq_ref[rows, :], 0.0), axis=-1, keepdims=True)
        mx = None
        for j in range(qi + 1):
            ks = slice(j * tq, (j + 1) * tq)
            s = lax.dot_general(q, k_ref[ks, :], (((1,), (1,)), ((), ())), preferred_element_type=F32)
            s = s / math.sqrt(HD_C) + fq - fk_ref[:, ks]
            if j == qi:
                s = jnp.where(causal, s, NEG)
            s_ref[base + j] = s
            fold = _lane_fold(s, jnp.maximum)
            mx = fold if mx is None else jnp.maximum(mx, fold)
        m = jnp.max(mx, axis=-1, keepdims=True)
        acc = None
        for j in range(qi + 1):
            p = jnp.exp(s_ref[base + j] - m).astype(BF16)
            pv = jnp.dot(p, vb_ref[j * tq:(j + 1) * tq, :], preferred_element_type=F32)
            acc = pv if acc is None else acc + pv
        o_ref[rows, :] = (acc[:, 0:HD_C] / acc[:, HD_C:]).astype(o_ref.dtype)


def _fox_prompt(q, k, v, f_rows, f_heads):
    b, t, _ = q.shape
    tq = _tile(t, 256)
    nq = t // tq
    head_rows = pl.BlockSpec((None, t, HD_C), lambda bi, h: (bi, 0, h))
    return pl.pallas_call(
        functools.partial(_fox_kernel, tq=tq, t=t),
        name="fox_prompt",
        grid=(b, H_C),
        in_specs=[head_rows, head_rows, head_rows,
                  pl.BlockSpec((None, t, LANES), lambda bi, h: (bi, 0, 0)),
                  pl.BlockSpec((None, None, 1, t), lambda bi, h: (bi, h, 0, 0))],
        out_specs=head_rows,
        out_shape=jax.ShapeDtypeStruct((b, t, H_C * HD_C), BF16),
        scratch_shapes=[pltpu.VMEM(((nq * (nq + 1)) // 2, tq, tq), F32),
                        pltpu.VMEM((t, 2 * HD_C), BF16)],
        compiler_params=_params("arbitrary", "arbitrary"),
    )(q, k, v, f_rows, f_heads.reshape(b, H_C, 1, t))


def _fox_sample_kernel(q_ref, kc_ref, vc_ref, kn_ref, vn_ref, fct_ref, fnc_ref, fec_ref, fnt_ref,
                       o_ref, s_ref, p_ref, m_ref, l_ref, acc_ref, kd_ref, vd_ref, *, t, tk):
    j = pl.program_id(1)
    contract_last = (((1,), (1,)), ((), ()))

    @pl.when(j == 0)
    def _():
        m_ref[...] = jnp.full_like(m_ref, NEG)
        l_ref[...] = jnp.zeros_like(l_ref)
        acc_ref[...] = jnp.zeros_like(acc_ref)

    def head_rows(h):
        return slice(h * t, (h + 1) * t)

    def update(width):
        s = s_ref[:, 0:width]
        m_prev = m_ref[...]
        m_col = jnp.maximum(m_prev[:, 0:1], jnp.max(s, axis=-1, keepdims=True))
        m_new = jnp.broadcast_to(m_col, m_prev.shape)
        alpha = jnp.exp(m_prev - m_new)
        p = jnp.exp(s - m_col)
        l_ref[...] = alpha * l_ref[...] + jnp.sum(p, axis=-1, keepdims=True)
        acc_ref[...] = alpha * acc_ref[...]
        m_ref[...] = m_new
        p_ref[:, 0:width] = p.astype(BF16)

    step = LANES
    for c in range(tk // step):
        src = slice(c * step * H_C, (c + 1) * step * H_C)
        dst = slice(c * step, (c + 1) * step)
        kd_ref[:, dst, :] = pltpu.einshape("(ph)d->hpd", kc_ref[src, :], h=H_C).astype(BF16)
        vd_ref[:, dst, :] = pltpu.einshape("(ph)d->hpd", vc_ref[src, :], h=H_C).astype(BF16)
    fq_cache = fnc_ref[...] + fec_ref[...]
    for h in range(H_C):
        hs = slice(h * HD_C, (h + 1) * HD_C)
        s = lax.dot_general(q_ref[:, hs], kd_ref[h], contract_last, preferred_element_type=F32)
        s_ref[head_rows(h), :] = s / math.sqrt(HD_C) + fq_cache[head_rows(h)] - fct_ref[h:h + 1, :]
    update(tk)
    for h in range(H_C):
        acc_ref[head_rows(h), :] += jnp.dot(p_ref[head_rows(h), :], vd_ref[h], preferred_element_type=F32)

    @pl.when(j == pl.num_programs(1) - 1)
    def _():
        causal = (lax.broadcasted_iota(jnp.int32, (t, t), 1)
                  <= lax.broadcasted_iota(jnp.int32, (t, t), 0))
        fq_new = fnc_ref[...]
        for h in range(H_C):
            hs = slice(h * HD_C, (h + 1) * HD_C)
            s = lax.dot_general(q_ref[:, hs], kn_ref[:, hs].astype(BF16), contract_last,
                                preferred_element_type=F32)
            s = s / math.sqrt(HD_C) + fq_new[head_rows(h)] - fnt_ref[h:h + 1, :]
            s_ref[head_rows(h), 0:t] = jnp.where(causal, s, NEG)
        update(t)
        for h in range(H_C):
            hs = slice(h * HD_C, (h + 1) * HD_C)
            pv = jnp.dot(p_ref[head_rows(h), 0:t], vn_ref[:, hs].astype(BF16), preferred_element_type=F32)
            o_ref[:, hs] = ((acc_ref[head_rows(h), :] + pv) / l_ref[head_rows(h), :]).astype(o_ref.dtype)


def _fox_sample(q, k_new, v_new, k_cache, v_cache, layer, fc_heads, fn_heads):
    b, t, d = q.shape
    n_layers, _, p = k_cache.shape[:3]
    tk = _tile(p, 1024)
    rows = H_C * t
    fn_col = fn_heads.reshape(b, rows, 1)
    fc_end_col = jnp.broadcast_to(fc_heads[:, :, p - 1:], (b, H_C, t)).reshape(b, rows, 1)
    cache_spec = pl.BlockSpec((None, None, tk * H_C, HD_C), lambda bi, j: (layer, bi, j, 0))
    new_spec = pl.BlockSpec((None, t, d), lambda bi, j: (bi, 0, 0))
    col_spec = pl.BlockSpec((None, rows, 1), lambda bi, j: (bi, 0, 0))
    return pl.pallas_call(
        functools.partial(_fox_sample_kernel, t=t, tk=tk),
        name="fox_sample",
        grid=(b, p // tk),
        in_specs=[new_spec, cache_spec, cache_spec, new_spec, new_spec,
                  pl.BlockSpec((None, H_C, tk), lambda bi, j: (bi, 0, j)),
                  col_spec, col_spec,
                  pl.BlockSpec((None, H_C, t), lambda bi, j: (bi, 0, 0))],
        out_specs=new_spec,
        out_shape=jax.ShapeDtypeStruct((b, t, d), BF16),
        scratch_shapes=[pltpu.VMEM((rows, tk), F32), pltpu.VMEM((rows, tk), BF16),
                        pltpu.VMEM((rows, LANES), F32), pltpu.VMEM((rows, LANES), F32),
                        pltpu.VMEM((rows, HD_C), F32),
                        pltpu.VMEM((H_C, tk, HD_C), BF16), pltpu.VMEM((H_C, tk, HD_C), BF16)],
        compiler_params=_params("arbitrary", "arbitrary"),
    )(q, k_cache.reshape(n_layers, b, p * H_C, HD_C), v_cache.reshape(n_layers, b, p * H_C, HD_C),
      k_new, v_new, fc_heads, fn_col, fc_end_col, fn_heads)


def kernel(x_prompt, x_sample, c_prompt, c_sample, cache_conv, cache_win_k, cache_win_v, cache_fox_k, cache_fox_v, cache_fox_logf, ada_w, ada_b, norm_mix_g, norm_ffn_g, e_w_in, e_w_dw, e_b_dw, e_ln_g, e_ln_b, e_sinks, e_w_out, o_w_in, o_b_f, o_w_out, ffn_w_gate, ffn_w_up, ffn_w_down, final_g):
    bp, tp, d = x_prompt.shape
    bs, ts, _ = x_sample.shape
    depth = ada_w.shape[0]
    n_odd = o_w_in.shape[0]
    past_len = cache_fox_k.shape[2]
    nqkv = H_C * HD_C

    mod = _adaln(jnp.concatenate([c_prompt, c_sample], axis=0), ada_w, ada_b)
    tab_p = _rope_tables(jnp.arange(tp))
    tab_s = tuple(jnp.tile(x, (bs, 1)) for x in _rope_tables(past_len + jnp.arange(ts)))

    xp = x_prompt.reshape(bp * tp, d)
    xs = x_sample.reshape(bs * ts, d)
    hp = _modulate(x_prompt, norm_mix_g[0], mod, 0, 0, 0).reshape(bp * tp, d)
    hs = _modulate(x_sample, norm_mix_g[0], mod, 0, 0, bp).reshape(bs * ts, d)
    outs = {name: [] for name in ("conv_p", "wk_p", "wv_p", "fl_p",
                                  "conv_s", "wk_s", "wv_s", "fk_s", "fv_s", "fl_s")}
    fk_p = fv_p = None
    e_in, e_out = e_w_in.astype(BF16), e_w_out.astype(BF16)
    o_in, o_out = o_w_in.astype(BF16), o_w_out.astype(BF16)
    wg, wu, wd = ffn_w_gate.astype(BF16), ffn_w_up.astype(BF16), ffn_w_down.astype(BF16)
    for l in range(depth):
        if l % 2 == 0:
            e = l // 2
            w_out, w_out_layer = e_out, e
            conv = (e_w_dw[e], e_b_dw[e], e_ln_g[e], e_ln_b[e])
            zp = _even_proj(hp, e_in, e, tab_p, tp).reshape(bp, tp, E_IN)
            cat, cst = _even_mix(zp, e_sinks[e], *conv)
            win = min(WINDOW, tp)
            outs["conv_p"].append(cst[:, HIST_PAD - (CONV_W - 1):])
            outs["wk_p"].append(zp[:, tp - win:, K_OFF:V_OFF].reshape(bp, win, KV_B, HD_B))
            outs["wv_p"].append(zp[:, tp - win:, V_OFF:].reshape(bp, win, KV_B, HD_B))
            yp = cat.reshape(bp * tp, d)

            zs = _even_proj(hs, e_in, e, tab_s, bs * ts).reshape(bs, ts, E_IN)
            hist = jnp.pad(cache_conv[e], ((0, 0), (HIST_PAD - (CONV_W - 1), 0), (0, 0)))
            cat, cst = _even_mix_sample(zs, hist, cache_win_k[e], cache_win_v[e], e_sinks[e], *conv)
            outs["conv_s"].append(cst[:, HIST_PAD - (CONV_W - 1):])
            outs["wk_s"].append(zs[:, :, K_OFF:V_OFF].reshape(bs, ts, KV_B, HD_B))
            outs["wv_s"].append(zs[:, :, V_OFF:].reshape(bs, ts, KV_B, HD_B))
            ys = cat.reshape(bs * ts, d)
        else:
            o = l // 2
            wf = jnp.pad(o_w_in[o][:, 3 * nqkv:], ((0, 0), (0, LANES - H_C))).astype(BF16)[None]
            bf = jnp.pad(o_b_f[o], (0, LANES - H_C)).reshape(1, LANES)
            w_out, w_out_layer = o_out, o

            q = _matmul(hp, o_in, o, 0, nqkv, BF16).reshape(bp, tp, nqkv)
            k, fk_p = _kv_proj(hp, o_in, o, nqkv, fk_p, o, n_odd)
            v, fv_p = _kv_proj(hp, o_in, o, 2 * nqkv, fv_p, o, n_odd)
            logit = _matmul(hp, wf, 0, 0, LANES, F32).reshape(bp, tp, LANES)
            lf, f_rows, f_heads = _forget_scan(logit, bf, True)
            outs["fl_p"].append(lf)
            yp = _fox_prompt(q, k.reshape(bp, tp, nqkv), v.reshape(bp, tp, nqkv),
                             f_rows, f_heads).reshape(bp * tp, d)

            q = _matmul(hs, o_in, o, 0, nqkv, BF16).reshape(bs, ts, nqkv)
            k = _matmul(hs, o_in, o, nqkv, nqkv, F32).reshape(bs, ts, nqkv)
            v = _matmul(hs, o_in, o, 2 * nqkv, nqkv, F32).reshape(bs, ts, nqkv)
            logit = _matmul(hs, wf, 0, 0, LANES, F32).reshape(bs, ts, LANES)
            lf, _, fn_heads = _forget_scan(logit, bf, True)
            cache_lf = jnp.pad(cache_fox_logf[o], ((0, 0), (0, 0), (0, LANES - H_C)))
            _, _, fc_heads = _forget_scan(cache_lf, bf, False)
            outs["fk_s"].append(k.reshape(bs, ts, H_C, HD_C))
            outs["fv_s"].append(v.reshape(bs, ts, H_C, HD_C))
            outs["fl_s"].append(lf)
            ys = _fox_sample(q, k, v, cache_fox_k, cache_fox_v, o, fc_heads,
                             fn_heads).reshape(bs * ts, d)

        xp, hp = _residual_matmul(yp, w_out, w_out_layer, xp, norm_ffn_g[l], mod, l, tp, 0)
        xs, hs = _residual_matmul(ys, w_out, w_out_layer, xs, norm_ffn_g[l], mod, l, ts, bp)

        if l + 1 < depth:
            xp, hp = _ffn(hp, xp, wg, wu, wd, norm_mix_g[l + 1], mod, l, tp, 0, False)
            xs, hs = _ffn(hs, xs, wg, wu, wd, norm_mix_g[l + 1], mod, l, ts, bp, False)
        else:
            y_prompt = _ffn(hp, xp, wg, wu, wd, final_g, mod, l, tp, 0, True).reshape(bp, tp, d)
            y_sample = _ffn(hs, xs, wg, wu, wd, final_g, mod, l, ts, bp, True).reshape(bs, ts, d)

    stack = lambda name: jnp.stack(outs[name])
    return (y_prompt, y_sample,
            stack("conv_p"), stack("wk_p"), stack("wv_p"),
            fk_p.reshape(n_odd, bp, tp, H_C, HD_C), fv_p.reshape(n_odd, bp, tp, H_C, HD_C), stack("fl_p"),
            stack("conv_s"), stack("wk_s"), stack("wv_s"), stack("fk_s"), stack("fv_s"), stack("fl_s"))
```
